```python
import jax
import jax.numpy as jnp
from jax import lax
import numpy as np

D_MODEL = 2048
BATCH = 4
SEQ = 2048
DEPTH = 4
DEC_BATCH = 32
DEC_SEQ = 8
PAST_LEN = 16384
PAGE_SIZE = 128

D_FF = 5632
FFN_RES_WEIGHT = 0.5
POOL_WIDTH = D_MODEL // 2
POOL_WINDOWS = (2, 4, 8, 16)
POOL_GROUPS = len(POOL_WINDOWS)
POOL_GROUP_DIM = POOL_WIDTH // POOL_GROUPS
POOL_KEEP = max(POOL_WINDOWS) - 1
HEAD_DIM = 64
N_HEADS = D_MODEL // 128
N_KV_HEADS = 4
GQA_GROUP = N_HEADS // N_KV_HEADS
ATTN_WIDTH = N_HEADS * HEAD_DIM
KV_WIDTH = N_KV_HEADS * HEAD_DIM
WINDOW = 128
Q_BLOCK = 128
ROT_DIM = HEAD_DIM // 4
ROPE_THETA = 500000.0
LRU_WIDTH = D_MODEL // 2
LRU_BLOCKS = 16
LRU_BLOCK_DIM = LRU_WIDTH // LRU_BLOCKS
CONV_WIDTH = 4
LRU_C = 8.0
N_BRANCHES = 3
IN_WIDTHS = (POOL_WIDTH, ATTN_WIDTH, KV_WIDTH, KV_WIDTH, LRU_WIDTH, LRU_WIDTH, N_BRANCHES * D_MODEL)
IN_COLS = sum(IN_WIDTHS)
RMS_EPS = 1e-6
NEG_INF = -1e30

kernel_name = "hybrid_pool_swa_rglru_macaron_step"


def _split_points():
    pts, acc = [], 0
    for w in IN_WIDTHS[:-1]:
        acc += w
        pts.append(acc)
    return pts


def rms_norm(x, g):
    xf = x.astype(jnp.float32)
    y = xf * lax.rsqrt(jnp.mean(xf * xf, axis=-1, keepdims=True) + RMS_EPS)
    return (y * g.astype(jnp.float32)).astype(x.dtype)


def swiglu_ffn(x, w_gu, w_down):
    g, u = jnp.split(x @ w_gu, 2, axis=-1)
    return (jax.nn.silu(g) * u) @ w_down


def rotary(x, pos):
    half = ROT_DIM // 2
    inv = ROPE_THETA ** (-jnp.arange(half, dtype=jnp.float32) / half)
    ang = pos.astype(jnp.float32)[:, None] * inv[None, :]
    cos = jnp.cos(ang)[None, :, None, :]
    sin = jnp.sin(ang)[None, :, None, :]
    xr = x[..., :ROT_DIM].astype(jnp.float32)
    x1, x2 = xr[..., :half], xr[..., half:]
    rot = jnp.concatenate([x1 * cos - x2 * sin, x2 * cos + x1 * sin], axis=-1).astype(x.dtype)
    return jnp.concatenate([rot, x[..., ROT_DIM:]], axis=-1)


def pool_mixer(u, past, pos, w_grp, scale):
    B, T, _ = u.shape
    full = jnp.concatenate([past, u], axis=1)
    cs = jnp.cumsum(full.astype(jnp.float32), axis=1)
    cs = jnp.concatenate([jnp.zeros((B, 1, POOL_WIDTH), jnp.float32), cs], axis=1)
    means = []
    for g, w in enumerate(POOL_WINDOWS):
        lo_c, hi_c = g * POOL_GROUP_DIM, (g + 1) * POOL_GROUP_DIM
        hi = cs[:, POOL_KEEP + 1:POOL_KEEP + 1 + T, lo_c:hi_c]
        lo = cs[:, POOL_KEEP + 1 - w:POOL_KEEP + 1 - w + T, lo_c:hi_c]
        cnt = jnp.minimum(pos + 1, w).astype(jnp.float32)[None, :, None]
        means.append((hi - lo) / cnt)
    mean = jnp.concatenate(means, axis=-1).astype(u.dtype)
    d = (mean - u).reshape(B, T, POOL_GROUPS, POOL_GROUP_DIM)
    out = jnp.einsum('btgc,gcd->btgd', d, w_grp).reshape(B, T, POOL_WIDTH)
    return out * scale, full[:, -POOL_KEEP:]


def sliding_window_attention(q, k, v, k_past, v_past, pos0, sinks):
    B, T = q.shape[0], q.shape[1]
    qb = min(Q_BLOCK, T)
    nb = -(-T // qb)
    tp = nb * qb
    pad = ((0, 0), (0, tp - T), (0, 0), (0, 0))
    k_all = jnp.concatenate([k_past, jnp.pad(k, pad)], axis=1)
    v_all = jnp.concatenate([v_past, jnp.pad(v, pad)], axis=1)
    span = WINDOW + qb
    idx = jnp.arange(nb)[:, None] * qb + jnp.arange(span)[None, :]
    kb = jnp.take(k_all, idx, axis=1)
    vb = jnp.take(v_all, idx, axis=1)
    qblk = jnp.pad(q, pad).reshape(B, nb, qb, N_KV_HEADS, GQA_GROUP, HEAD_DIM)
    s = jnp.einsum('bnqkgd,bnskd->bnkgqs', qblk, kb).astype(jnp.float32) * (HEAD_DIM ** -0.5)
    q_pos = pos0 + jnp.arange(tp).reshape(nb, qb)
    k_pos = pos0 - WINDOW + idx
    rel = q_pos[:, :, None] - k_pos[:, None, :]
    valid = (k_pos[:, None, :] >= 0) & (rel >= 0) & (rel <= WINDOW)
    s = jnp.where(valid[None, :, None, None, :, :], s, NEG_INF)
    sink = sinks.astype(jnp.float32).reshape(N_KV_HEADS, GQA_GROUP)[None, None, :, :, None, None]
    m = jnp.maximum(jnp.max(s, axis=-1, keepdims=True), sink)
    p = jnp.exp(s - m)
    p = (p / (jnp.sum(p, axis=-1, keepdims=True) + jnp.exp(sink - m))).astype(v.dtype)
    o = jnp.einsum('bnkgqs,bnskd->bnqkgd', p, vb).reshape(B, tp, ATTN_WIDTH)[:, :T]
    return o, k_all[:, T:T + WINDOW], v_all[:, T:T + WINDOW]


def rglru_mixer(xb, g_in, conv_past, h0, conv_w, conv_b, wa, ba, wx, bx, lam):
    B, T, _ = xb.shape
    full = jnp.concatenate([conv_past, xb], axis=1)
    xc = conv_b + full[:, 0:T] * conv_w[0]
    for j in range(1, CONV_WIDTH):
        xc = xc + full[:, j:j + T] * conv_w[j]
    xr = xc.reshape(B, T, LRU_BLOCKS, LRU_BLOCK_DIM)
    r = jax.nn.sigmoid((jnp.einsum('bthc,hcd->bthd', xr, wa).reshape(B, T, LRU_WIDTH) + ba).astype(jnp.float32))
    i = jax.nn.sigmoid((jnp.einsum('bthc,hcd->bthd', xr, wx).reshape(B, T, LRU_WIDTH) + bx).astype(jnp.float32))
    log_a = -LRU_C * r * jax.nn.softplus(-lam.astype(jnp.float32))
    a = jnp.exp(log_a)
    b = jnp.sqrt(-jnp.expm1(2.0 * log_a)) * (i * xc.astype(jnp.float32))
    b = b.at[:, 0].add(a[:, 0] * h0.astype(jnp.float32))

    def combine(left, right):
        a1, b1 = left
        a2, b2 = right
        return a1 * a2, a2 * b1 + b2

    _, h = lax.associative_scan(combine, (a, b), axis=1)
    y = h.astype(xb.dtype) * jax.nn.gelu(g_in)
    return y, full[:, -(CONV_WIDTH - 1):], h[:, -1].astype(h0.dtype)


def trunk_layer(x, pos0, pool_past, k_past, v_past, conv_past, h0, lw):
    B, T, _ = x.shape
    x = x + FFN_RES_WEIGHT * swiglu_ffn(rms_norm(x, lw['norm_ffa']), lw['ffa_w_gu'], lw['ffa_w_down'])
    h = rms_norm(x, lw['norm_mix'])
    z = h @ lw['w_in']
    u_pool, q, k, v, x_lru, g_lru, gates = jnp.split(z, _split_points(), axis=-1)
    pos = pos0 + jnp.arange(T)
    pool_out, new_pool = pool_mixer(u_pool, pool_past, pos, lw['pool_w'], lw['pool_scale'])
    q = rotary(rms_norm(q.reshape(B, T, N_HEADS, HEAD_DIM), lw['q_norm']), pos)
    k = rotary(rms_norm(k.reshape(B, T, N_KV_HEADS, HEAD_DIM), lw['k_norm']), pos)
    v = v.reshape(B, T, N_KV_HEADS, HEAD_DIM)
    attn_out, new_k, new_v = sliding_window_attention(q, k, v, k_past, v_past, pos0, lw['attn_sinks'])
    lru_out, new_conv, new_h = rglru_mixer(x_lru, g_lru, conv_past, h0, lw['conv_w'], lw['conv_b'],
                                           lw['lru_gate_a_w'], lw['lru_gate_a_b'],
                                           lw['lru_gate_x_w'], lw['lru_gate_x_b'], lw['lru_lambda'])
    g = jax.nn.sigmoid(gates.reshape(B, T, N_BRANCHES, D_MODEL))
    merged = (g[:, :, 0] * (pool_out @ lw['w_branch_pool'])
              + g[:, :, 1] * (attn_out @ lw['w_branch_attn'])
              + g[:, :, 2] * (lru_out @ lw['w_branch_lru']))
    x = x + merged @ lw['w_out']
    x = x + FFN_RES_WEIGHT * swiglu_ffn(rms_norm(x, lw['norm_ffb']), lw['ffb_w_gu'], lw['ffb_w_down'])
    return x, (new_pool, new_k, new_v, new_conv, new_h)


def setup_inputs(seed: int = 0) -> dict:
    key = jax.random.key(seed)
    ks = iter(jax.random.split(key, 48))
    f32 = jnp.float32

    def nrm(shape, scale):
        return scale * jax.random.normal(next(ks), shape, f32)

    def gain(shape):
        return 1.0 + nrm(shape, 0.02)

    sw_keep = min(WINDOW, PAST_LEN)
    a8 = jax.random.uniform(next(ks), (DEPTH, LRU_WIDTH), f32, 0.9, 0.999)
    a_base = a8 ** (1.0 / LRU_C)
    return {
        'x_prompt': nrm((BATCH, SEQ, D_MODEL), 1.0),
        'x_sample': nrm((DEC_BATCH, DEC_SEQ, D_MODEL), 1.0),
        'state_pool': nrm((DEPTH, DEC_BATCH, POOL_KEEP, POOL_WIDTH), 1.0),
        'cache_k_win': nrm((DEPTH, DEC_BATCH, sw_keep, N_KV_HEADS, HEAD_DIM), 1.0),
        'cache_v_win': nrm((DEPTH, DEC_BATCH, sw_keep, N_KV_HEADS, HEAD_DIM), 1.0),
        'state_conv': nrm((DEPTH, DEC_BATCH, CONV_WIDTH - 1, LRU_WIDTH), 1.0),
        'state_rglru': nrm((DEPTH, DEC_BATCH, LRU_WIDTH), 0.5),
        'norm_ffa': gain((DEPTH, D_MODEL)),
        'ffa_w_gu': nrm((DEPTH, D_MODEL, 2 * D_FF), D_MODEL ** -0.5),
        'ffa_w_down': nrm((DEPTH, D_FF, D_MODEL), D_FF ** -0.5),
        'norm_mix': gain((DEPTH, D_MODEL)),
        'w_in': nrm((DEPTH, D_MODEL, IN_COLS), D_MODEL ** -0.5),
        'pool_w': nrm((DEPTH, POOL_GROUPS, POOL_GROUP_DIM, POOL_GROUP_DIM), POOL_GROUP_DIM ** -0.5),
        'pool_scale': gain((DEPTH, POOL_WIDTH)),
        'q_norm': gain((DEPTH, HEAD_DIM)),
        'k_norm': gain((DEPTH, HEAD_DIM)),
        'attn_sinks': nrm((DEPTH, N_HEADS), 0.5),
        'conv_w': nrm((DEPTH, CONV_WIDTH, LRU_WIDTH), CONV_WIDTH ** -0.5),
        'conv_b': nrm((DEPTH, LRU_WIDTH), 0.01),
        'lru_gate_a_w': nrm((DEPTH, LRU_BLOCKS, LRU_BLOCK_DIM, LRU_BLOCK_DIM), LRU_BLOCK_DIM ** -0.5),
        'lru_gate_a_b': nrm((DEPTH, LRU_WIDTH), 0.01),
        'lru_gate_x_w': nrm((DEPTH, LRU_BLOCKS, LRU_BLOCK_DIM, LRU_BLOCK_DIM), LRU_BLOCK_DIM ** -0.5),
        'lru_gate_x_b': nrm((DEPTH, LRU_WIDTH), 0.01),
        'lru_lambda': jnp.log(a_base) - jnp.log1p(-a_base),
        'w_branch_pool': nrm((DEPTH, POOL_WIDTH, D_MODEL), POOL_WIDTH ** -0.5),
        'w_branch_attn': nrm((DEPTH, ATTN_WIDTH, D_MODEL), ATTN_WIDTH ** -0.5),
        'w_branch_lru': nrm((DEPTH, LRU_WIDTH, D_MODEL), LRU_WIDTH ** -0.5),
        'w_out': nrm((DEPTH, D_MODEL, D_MODEL), D_MODEL ** -0.5),
        'norm_ffb': gain((DEPTH, D_MODEL)),
        'ffb_w_gu': nrm((DEPTH, D_MODEL, 2 * D_FF), D_MODEL ** -0.5),
        'ffb_w_down': nrm((DEPTH, D_FF, D_MODEL), D_FF ** -0.5),
    }


def reference(x_prompt, x_sample, state_pool, cache_k_win, cache_v_win, state_conv, state_rglru,
              norm_ffa, ffa_w_gu, ffa_w_down, norm_mix, w_in, pool_w, pool_scale, q_norm, k_norm,
              attn_sinks, conv_w, conv_b, lru_gate_a_w, lru_gate_a_b, lru_gate_x_w, lru_gate_x_b,
              lru_lambda, w_branch_pool, w_branch_attn, w_branch_lru, w_out, norm_ffb, ffb_w_gu,
              ffb_w_down):
    bp = x_prompt.shape[0]
    dt = x_prompt.dtype
    z_pool = jnp.zeros((bp, POOL_KEEP, POOL_WIDTH), dt)
    z_kv = jnp.zeros((bp, WINDOW, N_KV_HEADS, HEAD_DIM), dt)
    z_conv = jnp.zeros((bp, CONV_WIDTH - 1, LRU_WIDTH), dt)
    z_h = jnp.zeros((bp, LRU_WIDTH), state_rglru.dtype)

    yp, ys = x_prompt, x_sample
    st_p = ([], [], [], [], [])
    st_s = ([], [], [], [], [])
    for l in range(DEPTH):
        lw = {
            'norm_ffa': norm_ffa[l], 'ffa_w_gu': ffa_w_gu[l], 'ffa_w_down': ffa_w_down[l],
            'norm_mix': norm_mix[l], 'w_in': w_in[l], 'pool_w': pool_w[l], 'pool_scale': pool_scale[l],
            'q_norm': q_norm[l], 'k_norm': k_norm[l], 'attn_sinks': attn_sinks[l],
            'conv_w': conv_w[l], 'conv_b': conv_b[l],
            'lru_gate_a_w': lru_gate_a_w[l], 'lru_gate_a_b': lru_gate_a_b[l],
            'lru_gate_x_w': lru_gate_x_w[l], 'lru_gate_x_b': lru_gate_x_b[l],
            'lru_lambda': lru_lambda[l],
            'w_branch_pool': w_branch_pool[l], 'w_branch_attn': w_branch_attn[l],
            'w_branch_lru': w_branch_lru[l], 'w_out': w_out[l],
            'norm_ffb': norm_ffb[l], 'ffb_w_gu': ffb_w_gu[l], 'ffb_w_down': ffb_w_down[l],
        }
        yp, sp = trunk_layer(yp, 0, z_pool, z_kv, z_kv, z_conv, z_h, lw)
        ys, ss = trunk_layer(ys, PAST_LEN, state_pool[l], cache_k_win[l], cache_v_win[l],
                             state_conv[l], state_rglru[l], lw)
        for j in range(5):
            st_p[j].append(sp[j])
            st_s[j].append(ss[j])

    new_pool_p = jnp.stack(st_p[0])
    new_pool_s = jnp.stack(st_s[0])
    new_k_p = jnp.stack(st_p[1])
    new_k_s = jnp.stack(st_s[1])
    new_v_p = jnp.stack(st_p[2])
    new_v_s = jnp.stack(st_s[2])
    new_conv_p = jnp.stack(st_p[3])
    new_conv_s = jnp.stack(st_s[3])
    new_h_p = jnp.stack(st_p[4])
    new_h_s = jnp.stack(st_s[4])
    return (yp, ys, new_pool_p, new_pool_s, new_k_p, new_k_s, new_v_p, new_v_s,
            new_conv_p, new_conv_s, new_h_p, new_h_s)
```

```python
import functools

import jax
import jax.numpy as jnp
from jax import lax
from jax.experimental import pallas as pl
from jax.experimental.pallas import tpu as pltpu

F32 = jnp.float32
BF16 = jnp.bfloat16

RMS_EPS = 1e-6
NEG_INF = -1e30
FFN_RES_WEIGHT = 0.5
POOL_WINDOWS = (2, 4, 8, 16)
POOL_KEEP = max(POOL_WINDOWS) - 1
POOL_HIST = 16
HEAD_DIM = 64
N_KV_HEADS = 4
GQA_GROUP = 4
WINDOW = 128
ROT_DIM = HEAD_DIM // 4
ROPE_THETA = 500000.0
CONV_WIDTH = 4
CONV_HIST = 8
LRU_C = 8.0
LRU_BLOCK_DIM = 64
PAST_LEN = 16384
LANES = 128
SUBLANES = 8
MIB = 1024 * 1024


def _bdot(a, b):
    return jnp.dot(a, b, preferred_element_type=F32)


def _rms_bf16(x, g):
    ms = jnp.mean(x * x, axis=-1, keepdims=True)
    return ((x * lax.rsqrt(ms + RMS_EPS)) * g).astype(BF16)


def _params(sem, vmem_mib):
    return pltpu.CompilerParams(dimension_semantics=sem, vmem_limit_bytes=vmem_mib * MIB)


def _ffn_kernel(x_ref, g_ref, wg_ref, wu_ref, wd_ref, o_ref, xn_ref):
    @pl.when(pl.program_id(1) == 0)
    def _():
        x = x_ref[...]
        xn_ref[...] = _rms_bf16(x, g_ref[...])
        o_ref[...] = x

    xn = xn_ref[...]
    g = _bdot(xn, wg_ref[...].astype(BF16))
    u = _bdot(xn, wu_ref[...].astype(BF16))
    h = (FFN_RES_WEIGHT * ((g * jax.nn.sigmoid(g)) * u)).astype(BF16)
    o_ref[...] += _bdot(h, wd_ref[...].astype(BF16))


def _ffn_call(x, norm, w_gu, w_down, layer, *, tm, tf):
    T, D = x.shape
    dff = w_down.shape[1]
    nj = dff // tf
    return pl.pallas_call(
        _ffn_kernel,
        grid=(T // tm, nj),
        in_specs=[
            pl.BlockSpec((tm, D), lambda i, j: (i, 0)),
            pl.BlockSpec((None, 1, D), lambda i, j: (layer, 0, 0)),
            pl.BlockSpec((None, D, tf), lambda i, j: (layer, 0, j)),
            pl.BlockSpec((None, D, tf), lambda i, j: (layer, 0, j + nj)),
            pl.BlockSpec((None, tf, D), lambda i, j: (layer, j, 0)),
        ],
        out_specs=pl.BlockSpec((tm, D), lambda i, j: (i, 0)),
        out_shape=jax.ShapeDtypeStruct((T, D), F32),
        scratch_shapes=[pltpu.VMEM((tm, D), BF16)],
        compiler_params=_params(("arbitrary", "arbitrary"), 60),
        name="ffn",
    )(x, norm, w_gu, w_gu, w_down)


def _inproj_kernel(x_ref, g_ref, w_ref, o_ref, xn_ref):
    @pl.when(pl.program_id(1) == 0)
    def _():
        xn_ref[...] = _rms_bf16(x_ref[...], g_ref[...])

    o_ref[...] = _bdot(xn_ref[...], w_ref[...].astype(BF16))


def _inproj_call(x, norm, w_in, layer, *, tm, tn):
    T, D = x.shape
    N = w_in.shape[2]
    return pl.pallas_call(
        _inproj_kernel,
        grid=(T // tm, N // tn),
        in_specs=[
            pl.BlockSpec((tm, D), lambda i, j: (i, 0)),
            pl.BlockSpec((None, 1, D), lambda i, j: (layer, 0, 0)),
            pl.BlockSpec((None, D, tn), lambda i, j: (layer, 0, j)),
        ],
        out_specs=pl.BlockSpec((tm, tn), lambda i, j: (i, j)),
        out_shape=jax.ShapeDtypeStruct((T, N), F32),
        scratch_shapes=[pltpu.VMEM((tm, D), BF16)],
        compiler_params=_params(("arbitrary", "arbitrary"), 56),
        name="inproj",
    )(x, norm, w_in)


def _merge_kernel(gp_ref, ga_ref, gl_ref, bp_ref, ba_ref, bl_ref, wp_ref, wa_ref, wl_ref, o_ref):
    def term(g_ref, b_ref, w_ref):
        return jax.nn.sigmoid(g_ref[...]) * _bdot(b_ref[...], w_ref[...].astype(BF16))

    m = term(gp_ref, bp_ref, wp_ref) + term(ga_ref, ba_ref, wa_ref) + term(gl_ref, bl_ref, wl_ref)
    o_ref[...] = m.astype(BF16)


def _merge_call(z, bp, ba, bl, wp, wa, wl, layer, gate_col0, *, tm, tc):
    T = z.shape[0]
    W = bp.shape[1]
    D = wp.shape[2]
    g0 = gate_col0 // tc
    gstep = D // tc
    gate_spec = lambda b: pl.BlockSpec((tm, tc), lambda i, c: (i, g0 + b * gstep + c))
    br_spec = pl.BlockSpec((tm, W), lambda i, c: (i, 0))
    w_spec = pl.BlockSpec((None, W, tc), lambda i, c: (layer, 0, c))
    return pl.pallas_call(
        _merge_kernel,
        grid=(T // tm, D // tc),
        in_specs=[gate_spec(0), gate_spec(1), gate_spec(2), br_spec, br_spec, br_spec,
                  w_spec, w_spec, w_spec],
        out_specs=pl.BlockSpec((tm, tc), lambda i, c: (i, c)),
        out_shape=jax.ShapeDtypeStruct((T, D), BF16),
        compiler_params=_params(("arbitrary", "arbitrary"), 56),
        name="merge",
    )(z, z, z, bp, ba, bl, wp, wa, wl)


def _outproj_kernel(x_ref, m_ref, w_ref, o_ref):
    o_ref[...] = x_ref[...] + _bdot(m_ref[...], w_ref[...].astype(BF16))


def _outproj_call(x, m, w_out, layer, *, tm, tn):
    T, D = x.shape
    return pl.pallas_call(
        _outproj_kernel,
        grid=(T // tm, D // tn),
        in_specs=[
            pl.BlockSpec((tm, tn), lambda i, c: (i, c)),
            pl.BlockSpec((tm, D), lambda i, c: (i, 0)),
            pl.BlockSpec((None, D, tn), lambda i, c: (layer, 0, c)),
        ],
        out_specs=pl.BlockSpec((tm, tn), lambda i, c: (i, c)),
        out_shape=jax.ShapeDtypeStruct((T, D), F32),
        compiler_params=_params(("arbitrary", "arbitrary"), 56),
        name="outproj",
    )(x, m, w_out)


def _pool_kernel(*refs, bs, tt, nt, pos0, has_past):
    if has_past:
        u_ref, past_ref, w_ref, s_ref, o_ref, np_ref, e_ref = refs
    else:
        u_ref, w_ref, s_ref, o_ref, np_ref, e_ref = refs
    t = pl.program_id(1)
    C = e_ref.shape[-1]
    H = POOL_HIST

    @pl.when(t == 0)
    def _():
        if has_past:
            e_ref[:, 0:H, :] = past_ref[...]
        else:
            e_ref[:, 0:H, :] = jnp.zeros((bs, H, C), F32)

    e_ref[:, H:H + tt, :] = u_ref[...].reshape(bs, tt, C)
    posp1 = lax.broadcasted_iota(jnp.int32, (1, tt, 1), 1) + (t * tt + pos0 + 1)
    gd = C // len(POOL_WINDOWS)
    for g, w in enumerate(POOL_WINDOWS):
        sl = slice(g * gd, (g + 1) * gd)
        e = e_ref[:, :, sl]
        p = e
        s = 1
        while s < w:
            p = p + pltpu.roll(p, s, 1)
            s *= 2
        cnt = jnp.minimum(posp1, w).astype(F32)
        d = (p[:, H:, :] / cnt - e[:, H:, :]).reshape(bs * tt, gd).astype(BF16)
        out = _bdot(d, w_ref[g].astype(BF16)) * s_ref[:, sl]
        o_ref[:, sl] = out.astype(BF16)

    carry = e_ref[:, tt:tt + H, :]
    e_ref[:, 0:H, :] = carry

    @pl.when(t == nt - 1)
    def _():
        np_ref[...] = carry


def _pool_call(z, past, pool_w, pool_scale, layer, *, nseq, seqlen, row0, bs, tt, pos0):
    C = pool_w.shape[1] * pool_w.shape[2]
    nt = seqlen // tt
    rb0 = row0 // (bs * tt)
    has_past = past is not None
    in_specs = [pl.BlockSpec((bs * tt, C), lambda b, t: (rb0 + b * nt + t, 0))]
    args = [z]
    if has_past:
        in_specs.append(pl.BlockSpec((None, bs, POOL_HIST, C), lambda b, t: (layer, b, 0, 0)))
        args.append(past)
    in_specs += [
        pl.BlockSpec((None,) + pool_w.shape[1:], lambda b, t: (layer, 0, 0, 0)),
        pl.BlockSpec((None, 1, C), lambda b, t: (layer, 0, 0)),
    ]
    args += [pool_w, pool_scale]
    return pl.pallas_call(
        functools.partial(_pool_kernel, bs=bs, tt=tt, nt=nt, pos0=pos0, has_past=has_past),
        grid=(nseq // bs, nt),
        in_specs=in_specs,
        out_specs=[
            pl.BlockSpec((bs * tt, C), lambda b, t: (b * nt + t, 0)),
            pl.BlockSpec((bs, POOL_HIST, C), lambda b, t: (b, 0, 0)),
        ],
        out_shape=[
            jax.ShapeDtypeStruct((nseq * seqlen, C), BF16),
            jax.ShapeDtypeStruct((nseq, POOL_HIST, C), F32),
        ],
        scratch_shapes=[pltpu.VMEM((bs, POOL_HIST + tt, C), F32)],
        compiler_params=_params(("arbitrary", "arbitrary"), 48),
        name="pool",
    )(*args)


def _lru_kernel(*refs, bs, tt, nt, has_past):
    if has_past:
        (x_ref, g_ref, cp_ref, h0_ref, cw_ref, cb_ref, wa_ref, ba_ref, wx_ref, bx_ref, lam_ref,
         y_ref, nc_ref, nh_ref, xe_ref, h_ref, a_ref, b_ref, bd_ref) = refs
    else:
        (x_ref, g_ref, cw_ref, cb_ref, wa_ref, ba_ref, wx_ref, bx_ref, lam_ref,
         y_ref, nc_ref, nh_ref, xe_ref, h_ref, a_ref, b_ref, bd_ref) = refs
    t = pl.program_id(2)
    Cc = xe_ref.shape[-1]
    R = bs * tt
    ncol = Cc // LANES
    HC = CONV_HIST
    blk = LRU_BLOCK_DIM

    @pl.when(t == 0)
    def _():
        if has_past:
            xe_ref[:, 0:HC, :] = cp_ref[...]
            h_ref[...] = jnp.broadcast_to(h0_ref[...], (bs, SUBLANES, Cc))
        else:
            xe_ref[:, 0:HC, :] = jnp.zeros((bs, HC, Cc), F32)
            h_ref[...] = jnp.zeros((bs, SUBLANES, Cc), F32)
        rep = (lax.broadcasted_iota(jnp.int32, (blk, LANES), 0)
               == (lax.broadcasted_iota(jnp.int32, (blk, LANES), 1) & (blk - 1))).astype(BF16)
        diag = ((lax.broadcasted_iota(jnp.int32, (LANES, LANES), 0) >= blk)
                == (lax.broadcasted_iota(jnp.int32, (LANES, LANES), 1) >= blk))
        for p in range(ncol):
            for k, w_ref in enumerate((wa_ref, wx_ref)):
                w2 = w_ref[2 * p:2 * p + 2].reshape(2 * blk, blk).astype(BF16)
                full = _bdot(w2, rep)
                bd_ref[p, :, k * LANES:(k + 1) * LANES] = jnp.where(diag, full, 0.0).astype(BF16)

    xe_ref[:, HC:HC + tt, :] = x_ref[...].reshape(bs, tt, Cc)
    xe = xe_ref[...]
    cw = cw_ref[...]
    xc = cb_ref[...] + pltpu.roll(xe, 3, 1)[:, HC:, :] * cw[0:1]
    xc = xc + pltpu.roll(xe, 2, 1)[:, HC:, :] * cw[1:2]
    xc = xc + pltpu.roll(xe, 1, 1)[:, HC:, :] * cw[2:3]
    xc = xc + xe[:, HC:, :] * cw[3:4]
    xc = xc.reshape(R, Cc)

    row8 = lax.broadcasted_iota(jnp.int32, (R, 1), 0) & (SUBLANES - 1)
    for p in range(ncol):
        col = slice(p * LANES, (p + 1) * LANES)
        xcp = xc[:, col]
        pre = _bdot(xcp.astype(BF16), bd_ref[p])
        r = jax.nn.sigmoid(pre[:, :LANES] + ba_ref[:, col])
        i = jax.nn.sigmoid(pre[:, LANES:] + bx_ref[:, col])
        nl = -lam_ref[:, col]
        sp = jnp.maximum(nl, 0.0) + jnp.log1p(jnp.exp(-jnp.abs(nl)))
        la = (-LRU_C * r) * sp
        a = jnp.exp(la)
        bv = jnp.sqrt(jnp.tanh(-la) * (a * a + 1.0)) * (i * xcp)
        for s in (1, 2, 4):
            keep = row8 >= s
            a_sh = pltpu.roll(a, s, 0)
            b_sh = pltpu.roll(bv, s, 0)
            bv = jnp.where(keep, a * b_sh + bv, bv)
            a = jnp.where(keep, a * a_sh, a)
        a_ref[...] = a.reshape(bs, tt, LANES)
        b_ref[...] = bv.reshape(bs, tt, LANES)

        def carry_step(k, h):
            o = pl.multiple_of(k * SUBLANES, SUBLANES)
            hb = a_ref[:, pl.ds(o, SUBLANES), :] * h + b_ref[:, pl.ds(o, SUBLANES), :]
            b_ref[:, pl.ds(o, SUBLANES), :] = hb
            return jnp.broadcast_to(hb[:, SUBLANES - 1:SUBLANES, :], hb.shape)

        h = lax.fori_loop(0, tt // SUBLANES, carry_step, h_ref[:, :, col])
        h_ref[:, :, col] = h
        hs = b_ref[...].reshape(R, LANES)
        y_ref[:, col] = (hs * jax.nn.gelu(g_ref[:, col], approximate=True)).astype(BF16)

    tail = xe_ref[:, tt:tt + HC, :]
    xe_ref[:, 0:HC, :] = tail

    @pl.when(t == nt - 1)
    def _():
        nc_ref[...] = tail
        nh_ref[...] = h_ref[:, 0:1, :]


def _lru_call(z, conv_past, h0, conv_w, conv_b, wa, ba, wx, bx, lam, layer, xcol0, gcol0,
              *, nseq, seqlen, row0, bs, tt, cc):
    C = conv_w.shape[2]
    nt = seqlen // tt
    nc = C // cc
    rb0 = row0 // (bs * tt)
    xb0 = xcol0 // cc
    gb0 = gcol0 // cc
    nblk = cc // LRU_BLOCK_DIM
    has_past = conv_past is not None
    in_specs = [
        pl.BlockSpec((bs * tt, cc), lambda b, c, t: (rb0 + b * nt + t, xb0 + c)),
        pl.BlockSpec((bs * tt, cc), lambda b, c, t: (rb0 + b * nt + t, gb0 + c)),
    ]
    args = [z, z]
    if has_past:
        in_specs += [
            pl.BlockSpec((None, bs, CONV_HIST, cc), lambda b, c, t: (layer, b, 0, c)),
            pl.BlockSpec((None, bs, 1, cc), lambda b, c, t: (layer, b, 0, c)),
        ]
        args += [conv_past, h0]
    vec = pl.BlockSpec((None, 1, cc), lambda b, c, t: (layer, 0, c))
    gw = pl.BlockSpec((None, nblk, LRU_BLOCK_DIM, LRU_BLOCK_DIM), lambda b, c, t: (layer, c, 0, 0))
    in_specs += [pl.BlockSpec((None, CONV_WIDTH, cc), lambda b, c, t: (layer, 0, c)), vec, gw, vec, gw, vec, vec]
    args += [conv_w, conv_b, wa, ba, wx, bx, lam]
    return pl.pallas_call(
        functools.partial(_lru_kernel, bs=bs, tt=tt, nt=nt, has_past=has_past),
        grid=(nseq // bs, nc, nt),
        in_specs=in_specs,
        out_specs=[
            pl.BlockSpec((bs * tt, cc), lambda b, c, t: (b * nt + t, c)),
            pl.BlockSpec((bs, CONV_HIST, cc), lambda b, c, t: (b, 0, c)),
            pl.BlockSpec((bs, 1, cc), lambda b, c, t: (b, 0, c)),
        ],
        out_shape=[
            jax.ShapeDtypeStruct((nseq * seqlen, C), BF16),
            jax.ShapeDtypeStruct((nseq, CONV_HIST, C), F32),
            jax.ShapeDtypeStruct((nseq, 1, C), F32),
        ],
        scratch_shapes=[
            pltpu.VMEM((bs, CONV_HIST + tt, cc), F32),
            pltpu.VMEM((bs, SUBLANES, cc), F32),
            pltpu.VMEM((bs, tt, LANES), F32),
            pltpu.VMEM((bs, tt, LANES), F32),
            pltpu.VMEM((cc // LANES, LANES, 2 * LANES), BF16),
        ],
        compiler_params=_params(("arbitrary", "arbitrary", "arbitrary"), 48),
        name="lru",
    )(*args)


def _attn_kernel(*refs, bs, qb, nb, pos0, has_past, layer):
    if has_past:
        (q_ref, kv_ref, kp_ref, vp_ref, qn_ref, kn_ref, cos_ref, s1_ref, s2_ref, sink_ref,
         o_ref, nk_ref, nv_ref, kb_ref, vb_ref, o_scr) = refs
    else:
        (q_ref, kv_ref, qn_ref, kn_ref, cos_ref, s1_ref, s2_ref, sink_ref,
         o_ref, nk_ref, nv_ref, kb_ref, vb_ref, o_scr) = refs
    n = pl.program_id(1)
    KV = kb_ref.shape[-1]
    nkeys = kb_ref.shape[0]
    half = HEAD_DIM
    lane = lax.broadcasted_iota(jnp.int32, (1, LANES), 1)
    seg_ones = ((lax.broadcasted_iota(jnp.int32, (LANES, LANES), 0) >= half)
                == (lax.broadcasted_iota(jnp.int32, (LANES, LANES), 1) >= half)).astype(BF16)
    cosw = cos_ref[...]
    s1w = s1_ref[...]
    s2w = s2_ref[...]

    def norm_rot(xw, gain):
        sq = xw * xw
        hi = sq.astype(BF16)
        lo = (sq - hi.astype(F32)).astype(BF16)
        ss = _bdot(hi, seg_ones) + _bdot(lo, seg_ones)
        y = (xw * lax.rsqrt(ss * (1.0 / HEAD_DIM) + RMS_EPS)) * gain
        return y * cosw + pltpu.roll(y, LANES - ROT_DIM // 2, 1) * s1w + pltpu.roll(y, ROT_DIM // 2, 1) * s2w

    qi = lax.broadcasted_iota(jnp.int32, (qb, nkeys), 0)
    si = lax.broadcasted_iota(jnp.int32, (qb, nkeys), 1)
    kpos0 = pos0 + n * qb - WINDOW
    valid = (si >= qi) & (si <= qi + WINDOW) & (si + kpos0 >= 0)

    if qb < WINDOW:
        kb_ref[WINDOW + qb:, :] = jnp.zeros((nkeys - WINDOW - qb, KV), F32)
        vb_ref[WINDOW + qb:, :] = jnp.zeros((nkeys - WINDOW - qb, KV), F32)

    def attend(s):
        r0 = s * qb if isinstance(s, int) else pl.multiple_of(s * qb, qb)
        rows = pl.ds(r0, qb)
        if has_past:
            kb_ref[0:WINDOW, :] = kp_ref[s]
            vb_ref[0:WINDOW, :] = vp_ref[s]
        kvx = kv_ref[rows, :]
        for w in range(KV // LANES):
            kb_ref[WINDOW:WINDOW + qb, w * LANES:(w + 1) * LANES] = norm_rot(
                kvx[:, w * LANES:(w + 1) * LANES], kn_ref[...])
        vb_ref[WINDOW:WINDOW + qb, :] = kvx[:, KV:]
        qx = q_ref[rows, :]
        for kv in range(N_KV_HEADS):
            w, off = divmod(kv, 2)
            in_head = (lane >= half) == (off == 1)

            def spread(ref):
                lo = jnp.where(in_head, ref[:, w * LANES:(w + 1) * LANES], 0.0)
                if off == 1:
                    lo = pltpu.roll(lo, half, 1)
                return jnp.concatenate([lo, pltpu.roll(lo, half, 1)], axis=0).astype(BF16)

            kk = spread(kb_ref)
            vv = spread(vb_ref)
            qst = jnp.concatenate(
                [norm_rot(qx[:, (2 * kv + h) * LANES:(2 * kv + h + 1) * LANES], qn_ref[...]) for h in range(2)],
                axis=0).astype(BF16)
            sc = lax.dot_general(qst, kk, (((1,), (1,)), ((), ())),
                                 preferred_element_type=F32) * (HEAD_DIM ** -0.5)
            prob_rows = []
            for rh in range(2):
                prob_cols = []
                for ch in range(2):
                    sink = sink_ref[layer, kv * GQA_GROUP + 2 * rh + ch]
                    sblk = jnp.where(valid, sc[rh * qb:(rh + 1) * qb, ch * nkeys:(ch + 1) * nkeys], NEG_INF)
                    m = jnp.maximum(jnp.max(sblk, axis=-1, keepdims=True), sink)
                    p = jnp.exp(sblk - m)
                    den = jnp.sum(p, axis=-1, keepdims=True) + jnp.exp(sink - m)
                    prob_cols.append(p * (1.0 / den))
                prob_rows.append(jnp.concatenate(prob_cols, axis=1))
            probs = jnp.concatenate(prob_rows, axis=0).astype(BF16)
            out = _bdot(probs, vv)
            for rh in range(2):
                o_scr[rows, (2 * kv + rh) * LANES:(2 * kv + rh + 1) * LANES] = out[rh * qb:(rh + 1) * qb]
        new_k = kb_ref[qb:qb + WINDOW, :]
        new_v = vb_ref[qb:qb + WINDOW, :]
        kb_ref[0:WINDOW, :] = new_k
        vb_ref[0:WINDOW, :] = new_v
        return new_k, new_v

    if bs == 1:
        if not has_past:
            @pl.when(n == 0)
            def _():
                kb_ref[0:WINDOW, :] = jnp.zeros((WINDOW, KV), F32)
                vb_ref[0:WINDOW, :] = jnp.zeros((WINDOW, KV), F32)

        new_k, new_v = attend(0)

        @pl.when(n == nb - 1)
        def _():
            nk_ref[0] = new_k
            nv_ref[0] = new_v
    else:
        def seq_step(s, c):
            new_k, new_v = attend(s)
            nk_ref[s] = new_k
            nv_ref[s] = new_v
            return c

        lax.fori_loop(0, bs, seq_step, 0)
    o_ref[...] = o_scr[...].astype(BF16)


def _attn_call(z, k_past, v_past, qn, kn, cos_t, s1_t, s2_t, sinks, layer, qcol0, kvcol0,
               *, nseq, seqlen, row0, bs, qb, pos0):
    A = N_KV_HEADS * GQA_GROUP * HEAD_DIM
    KV = N_KV_HEADS * HEAD_DIM
    nb = seqlen // qb
    assert bs == 1 or nb == 1
    rb0 = row0 // (bs * qb)
    has_past = k_past is not None
    in_specs = [
        pl.BlockSpec((bs * qb, A), lambda b, n: (rb0 + b * nb + n, qcol0 // A)),
        pl.BlockSpec((bs * qb, 2 * KV), lambda b, n: (rb0 + b * nb + n, kvcol0 // (2 * KV))),
    ]
    args = [z, z]
    if has_past:
        past_spec = pl.BlockSpec((None, bs, WINDOW, KV), lambda b, n: (layer, b, 0, 0))
        in_specs += [past_spec, past_spec]
        args += [k_past, v_past]
    gain = pl.BlockSpec((None, 1, LANES), lambda b, n: (layer, 0, 0))
    tab = pl.BlockSpec((qb, LANES), lambda b, n: (n, 0))
    in_specs += [gain, gain, tab, tab, tab, pl.BlockSpec(memory_space=pltpu.SMEM)]
    args += [qn, kn, cos_t, s1_t, s2_t, sinks]
    state = pl.BlockSpec((bs, WINDOW, KV), lambda b, n: (b, 0, 0))
    return pl.pallas_call(
        functools.partial(_attn_kernel, bs=bs, qb=qb, nb=nb, pos0=pos0, has_past=has_past, layer=layer),
        grid=(nseq // bs, nb),
        in_specs=in_specs,
        out_specs=[pl.BlockSpec((bs * qb, A), lambda b, n: (b * nb + n, 0)), state, state],
        out_shape=[
            jax.ShapeDtypeStruct((nseq * seqlen, A), BF16),
            jax.ShapeDtypeStruct((nseq, WINDOW, KV), F32),
            jax.ShapeDtypeStruct((nseq, WINDOW, KV), F32),
        ],
        scratch_shapes=[
            pltpu.VMEM((2 * WINDOW, KV), F32),
            pltpu.VMEM((2 * WINDOW, KV), F32),
            pltpu.VMEM((bs * qb, A), F32),
        ],
        compiler_params=_params(("arbitrary", "arbitrary"), 56),
        name="attn",
    )(*args)


def _rope_tables(pos):
    half = ROT_DIM // 2
    inv = ROPE_THETA ** (-jnp.arange(half, dtype=F32) / half)
    ang = pos.astype(F32)[:, None] * inv[None, :]
    cos, sin = jnp.cos(ang), jnp.sin(ang)
    T = pos.shape[0]
    ones = jnp.ones((T, HEAD_DIM - ROT_DIM), F32)
    zeros = jnp.zeros((T, HEAD_DIM - ROT_DIM), F32)
    zh = jnp.zeros((T, half), F32)
    cos_t = jnp.concatenate([cos, cos, ones], axis=1)
    s1_t = jnp.concatenate([-sin, zh, zeros], axis=1)
    s2_t = jnp.concatenate([zh, sin, zeros], axis=1)
    rep = LANES // HEAD_DIM
    return tuple(jnp.tile(t, (1, rep)) for t in (cos_t, s1_t, s2_t))


def kernel(x_prompt, x_sample, state_pool, cache_k_win, cache_v_win, state_conv, state_rglru, norm_ffa, ffa_w_gu, ffa_w_down, norm_mix, w_in, pool_w, pool_scale, q_norm, k_norm, attn_sinks, conv_w, conv_b, lru_gate_a_w, lru_gate_a_b, lru_gate_x_w, lru_gate_x_b, lru_lambda, w_branch_pool, w_branch_attn, w_branch_lru, w_out, norm_ffb, ffb_w_gu, ffb_w_down):
    Bp, Sp, D = x_prompt.shape
    Bs, Ss, _ = x_sample.shape
    L = norm_ffa.shape[0]
    Tp, Ts = Bp * Sp, Bs * Ss
    T = Tp + Ts
    pool_c = pool_scale.shape[1]
    attn_c = N_KV_HEADS * GQA_GROUP * HEAD_DIM
    kv_c = N_KV_HEADS * HEAD_DIM
    lru_c = conv_w.shape[2]
    q0 = pool_c
    kv0 = q0 + attn_c
    xl0 = kv0 + 2 * kv_c
    gl0 = xl0 + lru_c
    gate0 = gl0 + lru_c

    tm = T // 8
    x = jnp.concatenate([x_prompt.reshape(Tp, D), x_sample.reshape(Ts, D)], axis=0)

    vec3 = lambda a: a.reshape(L, 1, a.shape[-1])
    norm_ffa3, norm_mix3, norm_ffb3 = vec3(norm_ffa), vec3(norm_mix), vec3(norm_ffb)
    pool_scale3 = vec3(pool_scale)
    conv_b3, ba3, bx3, lam3 = vec3(conv_b), vec3(lru_gate_a_b), vec3(lru_gate_x_b), vec3(lru_lambda)
    qn3 = vec3(jnp.tile(q_norm, (1, LANES // HEAD_DIM)))
    kn3 = vec3(jnp.tile(k_norm, (1, LANES // HEAD_DIM)))
    pool_past = jnp.pad(state_pool, ((0, 0), (0, 0), (POOL_HIST - POOL_KEEP, 0), (0, 0)))
    conv_past = jnp.pad(state_conv, ((0, 0), (0, 0), (CONV_HIST - (CONV_WIDTH - 1), 0), (0, 0)))
    h0 = state_rglru.reshape(L, Bs, 1, lru_c)
    k_past = cache_k_win.reshape(L, Bs, WINDOW, kv_c)
    v_past = cache_v_win.reshape(L, Bs, WINDOW, kv_c)
    rope_p = _rope_tables(jnp.arange(Sp))
    rope_s = _rope_tables(PAST_LEN + jnp.arange(Ss))

    prompt = dict(nseq=Bp, seqlen=Sp, row0=0)
    sample = dict(nseq=Bs, seqlen=Ss, row0=Tp)
    st_p = ([], [], [], [], [])
    st_s = ([], [], [], [], [])
    for l in range(L):
        x = _ffn_call(x, norm_ffa3, ffa_w_gu, ffa_w_down, l, tm=tm, tf=256)
        z = _inproj_call(x, norm_mix3, w_in, l, tm=tm, tn=768)

        pool_p, np_p = _pool_call(z, None, pool_w, pool_scale3, l, bs=1, tt=512, pos0=0, **prompt)
        pool_s, np_s = _pool_call(z, pool_past, pool_w, pool_scale3, l, bs=Bs, tt=Ss, pos0=PAST_LEN, **sample)

        att_p, nk_p, nv_p = _attn_call(z, None, None, qn3, kn3, *rope_p, attn_sinks, l, q0, kv0,
                                       bs=1, qb=WINDOW, pos0=0, **prompt)
        att_s, nk_s, nv_s = _attn_call(z, k_past, v_past, qn3, kn3, *rope_s, attn_sinks, l, q0, kv0,
                                       bs=Bs, qb=Ss, pos0=PAST_LEN, **sample)

        lru_args = (conv_w, conv_b3, lru_gate_a_w, ba3, lru_gate_x_w, bx3, lam3, l, xl0, gl0)
        lru_p, nc_p, nh_p = _lru_call(z, None, None, *lru_args, bs=1, tt=512, cc=512, **prompt)
        lru_s, nc_s, nh_s = _lru_call(z, conv_past, h0, *lru_args, bs=Bs, tt=Ss, cc=512, **sample)

        cat = lambda a, b: jnp.concatenate([a, b], axis=0)
        m = _merge_call(z, cat(pool_p, pool_s), cat(att_p, att_s), cat(lru_p, lru_s),
                        w_branch_pool, w_branch_attn, w_branch_lru, l, gate0, tm=tm, tc=512)
        x = _outproj_call(x, m, w_out, l, tm=tm, tn=512)
        x = _ffn_call(x, norm_ffb3, ffb_w_gu, ffb_w_down, l, tm=tm, tf=256)

        for lst, val in zip(st_p, (np_p, nk_p, nv_p, nc_p, nh_p)):
            lst.append(val)
        for lst, val in zip(st_s, (np_s, nk_s, nv_s, nc_s, nh_s)):
            lst.append(val)

    def states(st, nseq):
        pool = jnp.stack(st[0])[:, :, POOL_HIST - POOL_KEEP:, :]
        k = jnp.stack(st[1]).reshape(L, nseq, WINDOW, N_KV_HEADS, HEAD_DIM)
        v = jnp.stack(st[2]).reshape(L, nseq, WINDOW, N_KV_HEADS, HEAD_DIM)
        conv = jnp.stack(st[3])[:, :, CONV_HIST - (CONV_WIDTH - 1):, :]
        h = jnp.stack(st[4]).reshape(L, nseq, lru_c)
        return pool, k, v, conv, h

    pool_p, k_p, v_p, conv_p, h_p = states(st_p, Bp)
    pool_s, k_s, v_s, conv_s, h_s = states(st_s, Bs)
    y_p = x[:Tp].reshape(Bp, Sp, D)
    y_s = x[Tp:].reshape(Bs, Ss, D)
    return (y_p, y_s, pool_p, pool_s, k_p, k_s, v_p, v_s, conv_p, conv_s, h_p, h_s)
```

```python
import functools

import jax
import jax.numpy as jnp
from jax import lax
from jax.experimental import pallas as pl
from jax.experimental.pallas import tpu as pltpu

F32 = jnp.float32
BF16 = jnp.bfloat16

RMS_EPS = 1e-6
NEG_INF = -1e30
FFN_RES_WEIGHT = 0.5
POOL_WINDOWS = (2, 4, 8, 16)
POOL_KEEP = max(POOL_WINDOWS) - 1
POOL_HIST = 16
HEAD_DIM = 64
N_KV_HEADS = 4
GQA_GROUP = 4
WINDOW = 128
ROT_DIM = HEAD_DIM // 4
ROPE_THETA = 500000.0
CONV_WIDTH = 4
CONV_HIST = 8
LRU_C = 8.0
LRU_BLOCK_DIM = 64
PAST_LEN = 16384
LANES = 128
SUBLANES = 8
SOFTMAX_ROWS = 256
MIB = 1024 * 1024


def _bdot(a, b):
    return jnp.dot(a, b, preferred_element_type=F32)


def _rms_bf16(x, g):
    ms = jnp.mean(x * x, axis=-1, keepdims=True)
    return ((x * lax.rsqrt(ms + RMS_EPS)) * g).astype(BF16)


def _sigmoid(x):
    return 0.5 * jnp.tanh(0.5 * x) + 0.5


def _params(sem, vmem_mib):
    return pltpu.CompilerParams(dimension_semantics=sem, vmem_limit_bytes=vmem_mib * MIB)


def _ffn_kernel(x_ref, g_ref, wg_ref, wu_ref, wd_ref, o_ref, xn_ref):
    @pl.when(pl.program_id(1) == 0)
    def _():
        x = x_ref[...]
        xn_ref[...] = _rms_bf16(x, g_ref[...])
        o_ref[...] = x

    xn = xn_ref[...]
    g = _bdot(xn, wg_ref[...].astype(BF16))
    u = _bdot(xn, wu_ref[...].astype(BF16))
    h = (FFN_RES_WEIGHT * ((g * _sigmoid(g)) * u)).astype(BF16)
    o_ref[...] += _bdot(h, wd_ref[...].astype(BF16))


def _ffn_call(x, norm, w_gu, w_down, layer, *, tm, tf):
    T, D = x.shape
    dff = w_down.shape[1]
    nj = dff // tf
    return pl.pallas_call(
        _ffn_kernel,
        grid=(T // tm, nj),
        in_specs=[
            pl.BlockSpec((tm, D), lambda i, j: (i, 0)),
            pl.BlockSpec((None, 1, D), lambda i, j: (layer, 0, 0)),
            pl.BlockSpec((None, D, tf), lambda i, j: (layer, 0, j)),
            pl.BlockSpec((None, D, tf), lambda i, j: (layer, 0, j + nj)),
            pl.BlockSpec((None, tf, D), lambda i, j: (layer, j, 0)),
        ],
        out_specs=pl.BlockSpec((tm, D), lambda i, j: (i, 0)),
        out_shape=jax.ShapeDtypeStruct((T, D), F32),
        scratch_shapes=[pltpu.VMEM((tm, D), BF16)],
        compiler_params=_params(("arbitrary", "arbitrary"), 60),
        name="ffn",
    )(x, norm, w_gu, w_gu, w_down)


def _inproj_kernel(x_ref, g_ref, w_ref, o_ref, xn_ref):
    @pl.when(pl.program_id(1) == 0)
    def _():
        xn_ref[...] = _rms_bf16(x_ref[...], g_ref[...])

    o_ref[...] = _bdot(xn_ref[...], w_ref[...].astype(BF16))


def _inproj_call(x, norm, w_in, layer, *, tm, tn):
    T, D = x.shape
    N = w_in.shape[2]
    return pl.pallas_call(
        _inproj_kernel,
        grid=(T // tm, N // tn),
        in_specs=[
            pl.BlockSpec((tm, D), lambda i, j: (i, 0)),
            pl.BlockSpec((None, 1, D), lambda i, j: (layer, 0, 0)),
            pl.BlockSpec((None, D, tn), lambda i, j: (layer, 0, j)),
        ],
        out_specs=pl.BlockSpec((tm, tn), lambda i, j: (i, j)),
        out_shape=jax.ShapeDtypeStruct((T, N), F32),
        scratch_shapes=[pltpu.VMEM((tm, D), BF16)],
        compiler_params=_params(("arbitrary", "arbitrary"), 56),
        name="inproj",
    )(x, norm, w_in)


def _merge_kernel(gp_ref, ga_ref, gl_ref, bp_ref, ba_ref, bl_ref, wp_ref, wa_ref, wl_ref, o_ref):
    def term(g_ref, b_ref, w_ref):
        return _sigmoid(g_ref[...]) * _bdot(b_ref[...], w_ref[...].astype(BF16))

    m = term(gp_ref, bp_ref, wp_ref) + term(ga_ref, ba_ref, wa_ref) + term(gl_ref, bl_ref, wl_ref)
    o_ref[...] = m.astype(BF16)


def _merge_call(z, bp, ba, bl, wp, wa, wl, layer, gate_col0, *, tm, tc):
    T = z.shape[0]
    W = bp.shape[1]
    D = wp.shape[2]
    g0 = gate_col0 // tc
    gstep = D // tc
    gate_spec = lambda b: pl.BlockSpec((tm, tc), lambda i, c: (i, g0 + b * gstep + c))
    br_spec = pl.BlockSpec((tm, W), lambda i, c: (i, 0))
    w_spec = pl.BlockSpec((None, W, tc), lambda i, c: (layer, 0, c))
    return pl.pallas_call(
        _merge_kernel,
        grid=(T // tm, D // tc),
        in_specs=[gate_spec(0), gate_spec(1), gate_spec(2), br_spec, br_spec, br_spec,
                  w_spec, w_spec, w_spec],
        out_specs=pl.BlockSpec((tm, tc), lambda i, c: (i, c)),
        out_shape=jax.ShapeDtypeStruct((T, D), BF16),
        compiler_params=_params(("arbitrary", "arbitrary"), 56),
        name="merge",
    )(z, z, z, bp, ba, bl, wp, wa, wl)


def _outproj_kernel(x_ref, m_ref, w_ref, o_ref):
    o_ref[...] = x_ref[...] + _bdot(m_ref[...], w_ref[...].astype(BF16))


def _outproj_call(x, m, w_out, layer, *, tm, tn):
    T, D = x.shape
    return pl.pallas_call(
        _outproj_kernel,
        grid=(T // tm, D // tn),
        in_specs=[
            pl.BlockSpec((tm, tn), lambda i, c: (i, c)),
            pl.BlockSpec((tm, D), lambda i, c: (i, 0)),
            pl.BlockSpec((None, D, tn), lambda i, c: (layer, 0, c)),
        ],
        out_specs=pl.BlockSpec((tm, tn), lambda i, c: (i, c)),
        out_shape=jax.ShapeDtypeStruct((T, D), F32),
        compiler_params=_params(("arbitrary", "arbitrary"), 56),
        name="outproj",
    )(x, m, w_out)


def _pool_kernel(*refs, bs, tt, nt, pos0, has_past):
    if has_past:
        u_ref, past_ref, w_ref, s_ref, o_ref, np_ref, e_ref = refs
    else:
        u_ref, w_ref, s_ref, o_ref, np_ref, e_ref = refs
    t = pl.program_id(1)
    C = e_ref.shape[-1]
    H = POOL_HIST

    @pl.when(t == 0)
    def _():
        if has_past:
            e_ref[:, 0:H, :] = past_ref[...]
        else:
            e_ref[:, 0:H, :] = jnp.zeros((bs, H, C), F32)

    e_ref[:, H:H + tt, :] = u_ref[...].reshape(bs, tt, C)
    posp1 = lax.broadcasted_iota(jnp.int32, (1, tt, 1), 1) + (t * tt + pos0 + 1)
    gd = C // len(POOL_WINDOWS)
    for g, w in enumerate(POOL_WINDOWS):
        sl = slice(g * gd, (g + 1) * gd)
        e = e_ref[:, :, sl]
        p = e
        s = 1
        while s < w:
            p = p + pltpu.roll(p, s, 1)
            s *= 2
        cnt = jnp.minimum(posp1, w).astype(F32)
        d = (p[:, H:, :] / cnt - e[:, H:, :]).reshape(bs * tt, gd).astype(BF16)
        out = _bdot(d, w_ref[g].astype(BF16)) * s_ref[:, sl]
        o_ref[:, sl] = out.astype(BF16)

    carry = e_ref[:, tt:tt + H, :]
    e_ref[:, 0:H, :] = carry

    @pl.when(t == nt - 1)
    def _():
        np_ref[...] = carry


def _pool_call(z, past, pool_w, pool_scale, layer, *, nseq, seqlen, row0, bs, tt, pos0):
    C = pool_w.shape[1] * pool_w.shape[2]
    nt = seqlen // tt
    rb0 = row0 // (bs * tt)
    has_past = past is not None
    in_specs = [pl.BlockSpec((bs * tt, C), lambda b, t: (rb0 + b * nt + t, 0))]
    args = [z]
    if has_past:
        in_specs.append(pl.BlockSpec((None, bs, POOL_HIST, C), lambda b, t: (layer, b, 0, 0)))
        args.append(past)
    in_specs += [
        pl.BlockSpec((None,) + pool_w.shape[1:], lambda b, t: (layer, 0, 0, 0)),
        pl.BlockSpec((None, 1, C), lambda b, t: (layer, 0, 0)),
    ]
    args += [pool_w, pool_scale]
    return pl.pallas_call(
        functools.partial(_pool_kernel, bs=bs, tt=tt, nt=nt, pos0=pos0, has_past=has_past),
        grid=(nseq // bs, nt),
        in_specs=in_specs,
        out_specs=[
            pl.BlockSpec((bs * tt, C), lambda b, t: (b * nt + t, 0)),
            pl.BlockSpec((bs, POOL_HIST, C), lambda b, t: (b, 0, 0)),
        ],
        out_shape=[
            jax.ShapeDtypeStruct((nseq * seqlen, C), BF16),
            jax.ShapeDtypeStruct((nseq, POOL_HIST, C), F32),
        ],
        scratch_shapes=[pltpu.VMEM((bs, POOL_HIST + tt, C), F32)],
        compiler_params=_params(("arbitrary", "arbitrary"), 48),
        name="pool",
    )(*args)


def _lru_kernel(*refs, bs, tt, nt, has_past):
    if has_past:
        (x_ref, g_ref, cp_ref, h0_ref, cw_ref, cb_ref, wa_ref, ba_ref, wx_ref, bx_ref, lam_ref,
         y_ref, nc_ref, nh_ref, xe_ref, h_ref, a_ref, b_ref, bd_ref) = refs
    else:
        (x_ref, g_ref, cw_ref, cb_ref, wa_ref, ba_ref, wx_ref, bx_ref, lam_ref,
         y_ref, nc_ref, nh_ref, xe_ref, h_ref, a_ref, b_ref, bd_ref) = refs
    t = pl.program_id(2)
    Cc = xe_ref.shape[-1]
    R = bs * tt
    ncol = Cc // LANES
    HC = CONV_HIST
    blk = LRU_BLOCK_DIM

    @pl.when(t == 0)
    def _():
        if has_past:
            xe_ref[:, 0:HC, :] = cp_ref[...]
            h_ref[...] = jnp.broadcast_to(h0_ref[...], (bs, SUBLANES, Cc))
        else:
            xe_ref[:, 0:HC, :] = jnp.zeros((bs, HC, Cc), F32)
            h_ref[...] = jnp.zeros((bs, SUBLANES, Cc), F32)
        rep = (lax.broadcasted_iota(jnp.int32, (blk, LANES), 0)
               == (lax.broadcasted_iota(jnp.int32, (blk, LANES), 1) & (blk - 1))).astype(BF16)
        diag = ((lax.broadcasted_iota(jnp.int32, (LANES, LANES), 0) >= blk)
                == (lax.broadcasted_iota(jnp.int32, (LANES, LANES), 1) >= blk))
        for p in range(ncol):
            for k, w_ref in enumerate((wa_ref, wx_ref)):
                w2 = w_ref[2 * p:2 * p + 2].reshape(2 * blk, blk).astype(BF16)
                full = _bdot(w2, rep)
                bd_ref[p, :, k * LANES:(k + 1) * LANES] = jnp.where(diag, full, 0.0).astype(BF16)

    xe_ref[:, HC:HC + tt, :] = x_ref[...].reshape(bs, tt, Cc)
    xe = xe_ref[...]
    cw = cw_ref[...]
    xc = cb_ref[...] + pltpu.roll(xe, 3, 1)[:, HC:, :] * cw[0:1]
    xc = xc + pltpu.roll(xe, 2, 1)[:, HC:, :] * cw[1:2]
    xc = xc + pltpu.roll(xe, 1, 1)[:, HC:, :] * cw[2:3]
    xc = xc + xe[:, HC:, :] * cw[3:4]
    xc = xc.reshape(R, Cc)

    row8 = lax.broadcasted_iota(jnp.int32, (1, SUBLANES, 1), 1)
    for p in range(ncol):
        col = slice(p * LANES, (p + 1) * LANES)
        xcp = xc[:, col]
        pre = _bdot(xcp.astype(BF16), bd_ref[p])
        r = _sigmoid(pre[:, :LANES] + ba_ref[:, col])
        i = _sigmoid(pre[:, LANES:] + bx_ref[:, col])
        nl = -lam_ref[:, col]
        sp = jnp.maximum(nl, 0.0) + jnp.log1p(jnp.exp(-jnp.abs(nl)))
        la = (-LRU_C * r) * sp
        a = jnp.exp(la)
        bv = jnp.sqrt(jnp.tanh(-la) * (a * a + 1.0)) * (i * xcp)
        a = a.reshape(R // SUBLANES, SUBLANES, LANES)
        bv = bv.reshape(R // SUBLANES, SUBLANES, LANES)
        for s in (1, 2, 4):
            keep = row8 >= s
            a_sh = pltpu.roll(a, s, 1)
            b_sh = pltpu.roll(bv, s, 1)
            bv = jnp.where(keep, a * b_sh + bv, bv)
            a = jnp.where(keep, a * a_sh, a)
        a_ref[:, :, col] = a.reshape(bs, tt, LANES)
        b_ref[:, :, col] = bv.reshape(bs, tt, LANES)

    def carry_step(k, h):
        o = pl.multiple_of(k * SUBLANES, SUBLANES)
        hb = a_ref[:, pl.ds(o, SUBLANES), :] * h + b_ref[:, pl.ds(o, SUBLANES), :]
        b_ref[:, pl.ds(o, SUBLANES), :] = hb
        return jnp.broadcast_to(hb[:, SUBLANES - 1:SUBLANES, :], hb.shape)

    ngroups = tt // SUBLANES
    h_ref[...] = lax.fori_loop(0, ngroups, carry_step, h_ref[...], unroll=min(ngroups, SUBLANES))
    for p in range(ncol):
        col = slice(p * LANES, (p + 1) * LANES)
        hs = b_ref[:, :, col].reshape(R, LANES)
        y_ref[:, col] = (hs * jax.nn.gelu(g_ref[:, col], approximate=True)).astype(BF16)

    tail = xe_ref[:, tt:tt + HC, :]
    xe_ref[:, 0:HC, :] = tail

    @pl.when(t == nt - 1)
    def _():
        nc_ref[...] = tail
        nh_ref[...] = h_ref[:, 0:1, :]


def _lru_call(z, conv_past, h0, conv_w, conv_b, wa, ba, wx, bx, lam, layer, xcol0, gcol0,
              *, nseq, seqlen, row0, bs, tt, cc):
    C = conv_w.shape[2]
    nt = seqlen // tt
    nc = C // cc
    rb0 = row0 // (bs * tt)
    xb0 = xcol0 // cc
    gb0 = gcol0 // cc
    nblk = cc // LRU_BLOCK_DIM
    has_past = conv_past is not None
    in_specs = [
        pl.BlockSpec((bs * tt, cc), lambda b, c, t: (rb0 + b * nt + t, xb0 + c)),
        pl.BlockSpec((bs * tt, cc), lambda b, c, t: (rb0 + b * nt + t, gb0 + c)),
    ]
    args = [z, z]
    if has_past:
        in_specs += [
            pl.BlockSpec((None, bs, CONV_HIST, cc), lambda b, c, t: (layer, b, 0, c)),
            pl.BlockSpec((None, bs, 1, cc), lambda b, c, t: (layer, b, 0, c)),
        ]
        args += [conv_past, h0]
    vec = pl.BlockSpec((None, 1, cc), lambda b, c, t: (layer, 0, c))
    gw = pl.BlockSpec((None, nblk, LRU_BLOCK_DIM, LRU_BLOCK_DIM), lambda b, c, t: (layer, c, 0, 0))
    in_specs += [pl.BlockSpec((None, CONV_WIDTH, cc), lambda b, c, t: (layer, 0, c)), vec, gw, vec, gw, vec, vec]
    args += [conv_w, conv_b, wa, ba, wx, bx, lam]
    return pl.pallas_call(
        functools.partial(_lru_kernel, bs=bs, tt=tt, nt=nt, has_past=has_past),
        grid=(nseq // bs, nc, nt),
        in_specs=in_specs,
        out_specs=[
            pl.BlockSpec((bs * tt, cc), lambda b, c, t: (b * nt + t, c)),
            pl.BlockSpec((bs, CONV_HIST, cc), lambda b, c, t: (b, 0, c)),
            pl.BlockSpec((bs, 1, cc), lambda b, c, t: (b, 0, c)),
        ],
        out_shape=[
            jax.ShapeDtypeStruct((nseq * seqlen, C), BF16),
            jax.ShapeDtypeStruct((nseq, CONV_HIST, C), F32),
            jax.ShapeDtypeStruct((nseq, 1, C), F32),
        ],
        scratch_shapes=[
            pltpu.VMEM((bs, CONV_HIST + tt, cc), F32),
            pltpu.VMEM((bs, SUBLANES, cc), F32),
            pltpu.VMEM((bs, tt, cc), F32),
            pltpu.VMEM((bs, tt, cc), F32),
            pltpu.VMEM((cc // LANES, LANES, 2 * LANES), BF16),
        ],
        compiler_params=_params(("arbitrary", "arbitrary", "arbitrary"), 48),
        name="lru",
    )(*args)


def _attn_stream_kernel(q_ref, kv_ref, qn_ref, kn_ref, cos_ref, s1_ref, s2_ref, sink_ref,
                        o_ref, nk_ref, nv_ref, kb_ref, vb_ref, *, qb, nb, pos0, layer):
    n = pl.program_id(1)
    KV = kb_ref.shape[-1]
    nkeys = kb_ref.shape[0]
    half = HEAD_DIM
    lane = lax.broadcasted_iota(jnp.int32, (1, LANES), 1)
    seg_ones = ((lax.broadcasted_iota(jnp.int32, (LANES, LANES), 0) >= half)
                == (lax.broadcasted_iota(jnp.int32, (LANES, LANES), 1) >= half)).astype(BF16)
    cosw = cos_ref[...]
    s1w = s1_ref[...]
    s2w = s2_ref[...]
    win = lambda w: slice(w * LANES, (w + 1) * LANES)

    def norm_rot(xw, gain):
        sq = xw * xw
        hi = sq.astype(BF16)
        lo = (sq - hi.astype(F32)).astype(BF16)
        ss = _bdot(hi, seg_ones) + _bdot(lo, seg_ones)
        y = (xw * lax.rsqrt(ss * (1.0 / HEAD_DIM) + RMS_EPS)) * gain
        return y * cosw + pltpu.roll(y, LANES - ROT_DIM // 2, 1) * s1w + pltpu.roll(y, ROT_DIM // 2, 1) * s2w

    qi = lax.broadcasted_iota(jnp.int32, (qb, nkeys), 0)
    si = lax.broadcasted_iota(jnp.int32, (qb, nkeys), 1)
    kpos0 = pos0 + n * qb - WINDOW
    valid = (si >= qi) & (si <= qi + WINDOW) & (si + kpos0 >= 0)

    @pl.when(n == 0)
    def _():
        kb_ref[0:WINDOW, :] = jnp.zeros((WINDOW, KV), F32)
        vb_ref[0:WINDOW, :] = jnp.zeros((WINDOW, KV), F32)

    kvx = kv_ref[...]
    for w in range(KV // LANES):
        kb_ref[WINDOW:, win(w)] = norm_rot(kvx[:, win(w)], kn_ref[...])
    vb_ref[WINDOW:, :] = kvx[:, KV:]
    qx = q_ref[...]
    for kv in range(N_KV_HEADS):
        w, off = divmod(kv, 2)
        in_head = (lane >= half) == (off == 1)

        def spread(ref):
            lo = jnp.where(in_head, ref[:, win(w)], 0.0)
            if off == 1:
                lo = pltpu.roll(lo, half, 1)
            return jnp.concatenate([lo, pltpu.roll(lo, half, 1)], axis=0).astype(BF16)

        kk = spread(kb_ref)
        vv = spread(vb_ref)
        qst = jnp.concatenate([norm_rot(qx[:, win(2 * kv + h)], qn_ref[...]) for h in range(2)],
                              axis=0).astype(BF16)
        sc = lax.dot_general(qst, kk, (((1,), (1,)), ((), ())),
                             preferred_element_type=F32) * (HEAD_DIM ** -0.5)
        prob_rows = []
        for rh in range(2):
            prob_cols = []
            for ch in range(2):
                sink = sink_ref[layer, kv * GQA_GROUP + 2 * rh + ch]
                sblk = jnp.where(valid, sc[rh * qb:(rh + 1) * qb, ch * nkeys:(ch + 1) * nkeys], NEG_INF)
                m = jnp.maximum(jnp.max(sblk, axis=-1, keepdims=True), sink)
                p = jnp.exp(sblk - m)
                den = jnp.sum(p, axis=-1, keepdims=True) + jnp.exp(sink - m)
                prob_cols.append(p * (1.0 / den))
            prob_rows.append(jnp.concatenate(prob_cols, axis=1))
        probs = jnp.concatenate(prob_rows, axis=0).astype(BF16)
        out = _bdot(probs, vv)
        for rh in range(2):
            o_ref[:, win(2 * kv + rh)] = out[rh * qb:(rh + 1) * qb].astype(BF16)
    new_k = kb_ref[WINDOW:, :]
    new_v = vb_ref[WINDOW:, :]
    kb_ref[0:WINDOW, :] = new_k
    vb_ref[0:WINDOW, :] = new_v

    @pl.when(n == nb - 1)
    def _():
        nk_ref[0] = new_k
        nv_ref[0] = new_v


def _attn_seq_kernel(q_ref, kv_ref, kp_ref, vp_ref, qn_ref, kn_ref, cos_ref, s1_ref, s2_ref, sink_ref,
                     o_ref, nk_ref, nv_ref, q_scr, k_scr, o_scr, *, bs, qb, pos0, layer):
    KV = k_scr.shape[-1]
    nkeys = 2 * WINDOW
    half = HEAD_DIM
    lane = lax.broadcasted_iota(jnp.int32, (1, LANES), 1)
    seg_ones = ((lax.broadcasted_iota(jnp.int32, (LANES, LANES), 0) >= half)
                == (lax.broadcasted_iota(jnp.int32, (LANES, LANES), 1) >= half)).astype(BF16)
    R = q_scr.shape[0]
    nq = q_scr.shape[1] // LANES
    nk = KV // LANES
    win = lambda w: slice(w * LANES, (w + 1) * LANES)

    xs = jnp.concatenate([q_ref[:, win(w)] for w in range(nq)] + [kv_ref[:, win(w)] for w in range(nk)], axis=0)
    sq = xs * xs
    hi = sq.astype(BF16)
    lo = (sq - hi.astype(F32)).astype(BF16)
    ss = _bdot(hi, seg_ones) + _bdot(lo, seg_ones)
    y = (xs * lax.rsqrt(ss * (1.0 / HEAD_DIM) + RMS_EPS)).reshape(nq + nk, R, LANES)
    gains = jnp.concatenate([jnp.broadcast_to(qn_ref[...] * (HEAD_DIM ** -0.5), (nq, 1, LANES)),
                             jnp.broadcast_to(kn_ref[...], (nk, 1, LANES))], axis=0)
    y = y * gains
    y = (y * cos_ref[...] + pltpu.roll(y, LANES - ROT_DIM // 2, 2) * s1_ref[...]
         + pltpu.roll(y, ROT_DIM // 2, 2) * s2_ref[...])
    for w in range(nq):
        q_scr[:, win(w)] = y[w]
    for w in range(nk):
        k_scr[:, win(w)] = y[nq + w]

    group = max(1, min(N_KV_HEADS, SOFTMAX_ROWS // (2 * qb)))
    nrow = 2 * group * qb
    qi = lax.broadcasted_iota(jnp.int32, (nrow, nkeys), 0) & (qb - 1)
    si = lax.broadcasted_iota(jnp.int32, (nrow, nkeys), 1)
    valid = (si >= qi) & (si <= qi + WINDOW) & (si + (pos0 - WINDOW) >= 0)
    sink_cols = [
        [jnp.concatenate([jnp.full((qb, 1), sink_ref[layer, kv * GQA_GROUP + 2 * rh + ch], F32)
                          for kv in range(g0, g0 + group) for rh in range(2)], axis=0)
         for ch in range(2)]
        for g0 in range(0, N_KV_HEADS, group)]

    def seq_step(s, c):
        rows = pl.ds(pl.multiple_of(s * qb, qb), qb)
        prev_k = kp_ref[s]
        prev_v = vp_ref[s]
        cur_k = k_scr[rows, :]
        cur_v = kv_ref[rows, KV:]
        pad = jnp.zeros((WINDOW - qb, KV), F32)
        k_all = jnp.concatenate([prev_k, cur_k, pad], axis=0)
        v_all = jnp.concatenate([prev_v, cur_v, pad], axis=0)

        def spread(x, kv):
            w, off = divmod(kv, 2)
            lo = jnp.where((lane >= half) == (off == 1), x[:, win(w)], 0.0)
            if off == 1:
                lo = pltpu.roll(lo, half, 1)
            return jnp.concatenate([lo, pltpu.roll(lo, half, 1)], axis=0).astype(BF16)

        for gi, g0 in enumerate(range(0, N_KV_HEADS, group)):
            scores = []
            for kv in range(g0, g0 + group):
                qst = jnp.concatenate([q_scr[rows, win(2 * kv + h)] for h in range(2)], axis=0).astype(BF16)
                scores.append(lax.dot_general(qst, spread(k_all, kv), (((1,), (1,)), ((), ())),
                                              preferred_element_type=F32))
            sc = jnp.concatenate(scores, axis=0)
            prob_cols = []
            for ch in range(2):
                sink = sink_cols[gi][ch]
                sblk = jnp.where(valid, sc[:, ch * nkeys:(ch + 1) * nkeys], NEG_INF)
                m = jnp.maximum(jnp.max(sblk, axis=-1, keepdims=True), sink)
                p = jnp.exp(sblk - m)
                den = jnp.sum(p, axis=-1, keepdims=True) + jnp.exp(sink - m)
                prob_cols.append(p * (1.0 / den))
            probs = jnp.concatenate(prob_cols, axis=1).astype(BF16)
            for j, kv in enumerate(range(g0, g0 + group)):
                out = _bdot(probs[2 * j * qb:2 * (j + 1) * qb], spread(v_all, kv))
                for rh in range(2):
                    o_scr[rows, win(2 * kv + rh)] = out[rh * qb:(rh + 1) * qb]
        nk_ref[s] = jnp.concatenate([prev_k[qb:], cur_k], axis=0)
        nv_ref[s] = jnp.concatenate([prev_v[qb:], cur_v], axis=0)
        return c

    lax.fori_loop(0, bs, seq_step, 0, unroll=min(bs, 4))
    o_ref[...] = o_scr[...].astype(BF16)


def _attn_call(z, k_past, v_past, qn, kn, cos_t, s1_t, s2_t, sinks, layer, qcol0, kvcol0,
               *, nseq, seqlen, row0, bs, qb, pos0):
    A = N_KV_HEADS * GQA_GROUP * HEAD_DIM
    KV = N_KV_HEADS * HEAD_DIM
    nb = seqlen // qb
    rb0 = row0 // (bs * qb)
    has_past = k_past is not None
    R = bs * qb
    if has_past:
        assert nb == 1 and qb < WINDOW
        body = functools.partial(_attn_seq_kernel, bs=bs, qb=qb, pos0=pos0, layer=layer)
        scratch = [pltpu.VMEM((R, A), F32), pltpu.VMEM((R, KV), F32), pltpu.VMEM((R, A), F32)]
    else:
        assert bs == 1 and qb == WINDOW
        body = functools.partial(_attn_stream_kernel, qb=qb, nb=nb, pos0=pos0, layer=layer)
        scratch = [pltpu.VMEM((2 * WINDOW, KV), F32), pltpu.VMEM((2 * WINDOW, KV), F32)]
    in_specs = [
        pl.BlockSpec((bs * qb, A), lambda b, n: (rb0 + b * nb + n, qcol0 // A)),
        pl.BlockSpec((bs * qb, 2 * KV), lambda b, n: (rb0 + b * nb + n, kvcol0 // (2 * KV))),
    ]
    args = [z, z]
    if has_past:
        past_spec = pl.BlockSpec((None, bs, WINDOW, KV), lambda b, n: (layer, b, 0, 0))
        in_specs += [past_spec, past_spec]
        args += [k_past, v_past]
    gain = pl.BlockSpec((None, 1, LANES), lambda b, n: (layer, 0, 0))
    tab = pl.BlockSpec((R, LANES), lambda b, n: (n, 0))
    in_specs += [gain, gain, tab, tab, tab, pl.BlockSpec(memory_space=pltpu.SMEM)]
    args += [qn, kn, cos_t, s1_t, s2_t, sinks]
    state = pl.BlockSpec((bs, WINDOW, KV), lambda b, n: (b, 0, 0))
    return pl.pallas_call(
        body,
        grid=(nseq // bs, nb),
        in_specs=in_specs,
        out_specs=[pl.BlockSpec((bs * qb, A), lambda b, n: (b * nb + n, 0)), state, state],
        out_shape=[
            jax.ShapeDtypeStruct((nseq * seqlen, A), BF16),
            jax.ShapeDtypeStruct((nseq, WINDOW, KV), F32),
            jax.ShapeDtypeStruct((nseq, WINDOW, KV), F32),
        ],
        scratch_shapes=scratch,
        compiler_params=_params(("arbitrary", "arbitrary"), 56),
        name="attn",
    )(*args)


def _rope_tables(pos, nrep):
    half = ROT_DIM // 2
    inv = ROPE_THETA ** (-jnp.arange(half, dtype=F32) / half)
    ang = pos.astype(F32)[:, None] * inv[None, :]
    cos, sin = jnp.cos(ang), jnp.sin(ang)
    T = pos.shape[0]
    ones = jnp.ones((T, HEAD_DIM - ROT_DIM), F32)
    zeros = jnp.zeros((T, HEAD_DIM - ROT_DIM), F32)
    zh = jnp.zeros((T, half), F32)
    cos_t = jnp.concatenate([cos, cos, ones], axis=1)
    s1_t = jnp.concatenate([-sin, zh, zeros], axis=1)
    s2_t = jnp.concatenate([zh, sin, zeros], axis=1)
    rep = LANES // HEAD_DIM
    return tuple(jnp.tile(t, (nrep, rep)) for t in (cos_t, s1_t, s2_t))


def kernel(x_prompt, x_sample, state_pool, cache_k_win, cache_v_win, state_conv, state_rglru, norm_ffa, ffa_w_gu, ffa_w_down, norm_mix, w_in, pool_w, pool_scale, q_norm, k_norm, attn_sinks, conv_w, conv_b, lru_gate_a_w, lru_gate_a_b, lru_gate_x_w, lru_gate_x_b, lru_lambda, w_branch_pool, w_branch_attn, w_branch_lru, w_out, norm_ffb, ffb_w_gu, ffb_w_down):
    Bp, Sp, D = x_prompt.shape
    Bs, Ss, _ = x_sample.shape
    L = norm_ffa.shape[0]
    Tp, Ts = Bp * Sp, Bs * Ss
    T = Tp + Ts
    pool_c = pool_scale.shape[1]
    attn_c = N_KV_HEADS * GQA_GROUP * HEAD_DIM
    kv_c = N_KV_HEADS * HEAD_DIM
    lru_c = conv_w.shape[2]
    q0 = pool_c
    kv0 = q0 + attn_c
    xl0 = kv0 + 2 * kv_c
    gl0 = xl0 + lru_c
    gate0 = gl0 + lru_c

    tm = T // 8
    x = jnp.concatenate([x_prompt.reshape(Tp, D), x_sample.reshape(Ts, D)], axis=0)

    vec3 = lambda a: a.reshape(L, 1, a.shape[-1])
    norm_ffa3, norm_mix3, norm_ffb3 = vec3(norm_ffa), vec3(norm_mix), vec3(norm_ffb)
    pool_scale3 = vec3(pool_scale)
    conv_b3, ba3, bx3, lam3 = vec3(conv_b), vec3(lru_gate_a_b), vec3(lru_gate_x_b), vec3(lru_lambda)
    qn3 = vec3(jnp.tile(q_norm, (1, LANES // HEAD_DIM)))
    kn3 = vec3(jnp.tile(k_norm, (1, LANES // HEAD_DIM)))
    pool_past = jnp.pad(state_pool, ((0, 0), (0, 0), (POOL_HIST - POOL_KEEP, 0), (0, 0)))
    conv_past = jnp.pad(state_conv, ((0, 0), (0, 0), (CONV_HIST - (CONV_WIDTH - 1), 0), (0, 0)))
    h0 = state_rglru.reshape(L, Bs, 1, lru_c)
    k_past = cache_k_win.reshape(L, Bs, WINDOW, kv_c)
    v_past = cache_v_win.reshape(L, Bs, WINDOW, kv_c)
    rope_p = _rope_tables(jnp.arange(Sp), 1)
    rope_s = _rope_tables(PAST_LEN + jnp.arange(Ss), Bs)

    prompt = dict(nseq=Bp, seqlen=Sp, row0=0)
    sample = dict(nseq=Bs, seqlen=Ss, row0=Tp)
    st_p = ([], [], [], [], [])
    st_s = ([], [], [], [], [])
    for l in range(L):
        x = _ffn_call(x, norm_ffa3, ffa_w_gu, ffa_w_down, l, tm=tm, tf=256)
        z = _inproj_call(x, norm_mix3, w_in, l, tm=tm, tn=768)

        pool_p, np_p = _pool_call(z, None, pool_w, pool_scale3, l, bs=1, tt=512, pos0=0, **prompt)
        pool_s, np_s = _pool_call(z, pool_past, pool_w, pool_scale3, l, bs=Bs, tt=Ss, pos0=PAST_LEN, **sample)

        att_p, nk_p, nv_p = _attn_call(z, None, None, qn3, kn3, *rope_p, attn_sinks, l, q0, kv0,
                                       bs=1, qb=WINDOW, pos0=0, **prompt)
        att_s, nk_s, nv_s = _attn_call(z, k_past, v_past, qn3, kn3, *rope_s, attn_sinks, l, q0, kv0,
                                       bs=Bs, qb=Ss, pos0=PAST_LEN, **sample)

        lru_args = (conv_w, conv_b3, lru_gate_a_w, ba3, lru_gate_x_w, bx3, lam3, l, xl0, gl0)
        lru_p, nc_p, nh_p = _lru_call(z, None, None, *lru_args, bs=1, tt=512, cc=512, **prompt)
        lru_s, nc_s, nh_s = _lru_call(z, conv_past, h0, *lru_args, bs=Bs, tt=Ss, cc=512, **sample)

        cat = lambda a, b: jnp.concatenate([a, b], axis=0)
        m = _merge_call(z, cat(pool_p, pool_s), cat(att_p, att_s), cat(lru_p, lru_s),
                        w_branch_pool, w_branch_attn, w_branch_lru, l, gate0, tm=tm, tc=512)
        x = _outproj_call(x, m, w_out, l, tm=tm, tn=512)
        x = _ffn_call(x, norm_ffb3, ffb_w_gu, ffb_w_down, l, tm=tm, tf=256)

        for lst, val in zip(st_p, (np_p, nk_p, nv_p, nc_p, nh_p)):
            lst.append(val)
        for lst, val in zip(st_s, (np_s, nk_s, nv_s, nc_s, nh_s)):
            lst.append(val)

    def states(st, nseq):
        pool = jnp.stack(st[0])[:, :, POOL_HIST - POOL_KEEP:, :]
        k = jnp.stack(st[1]).reshape(L, nseq, WINDOW, N_KV_HEADS, HEAD_DIM)
        v = jnp.stack(st[2]).reshape(L, nseq, WINDOW, N_KV_HEADS, HEAD_DIM)
        conv = jnp.stack(st[3])[:, :, CONV_HIST - (CONV_WIDTH - 1):, :]
        h = jnp.stack(st[4]).reshape(L, nseq, lru_c)
        return pool, k, v, conv, h

    pool_p, k_p, v_p, conv_p, h_p = states(st_p, Bp)
    pool_s, k_s, v_s, conv_s, h_s = states(st_s, Bs)
    y_p = x[:Tp].reshape(Bp, Sp, D)
    y_s = x[Tp:].reshape(Bs, Ss, D)
    return (y_p, y_s, pool_p, pool_s, k_p, k_s, v_p, v_s, conv_p, conv_s, h_p, h_s)
```

```python
import functools

import jax
import jax.numpy as jnp
from jax import lax
from jax.experimental import pallas as pl
from jax.experimental.pallas import tpu as pltpu

F32 = jnp.float32
BF16 = jnp.bfloat16

RMS_EPS = 1e-6
NEG_INF = -1e30
FFN_RES_WEIGHT = 0.5
POOL_WINDOWS = (2, 4, 8, 16)
POOL_KEEP = max(POOL_WINDOWS) - 1
POOL_HIST = 16
HEAD_DIM = 64
N_KV_HEADS = 4
GQA_GROUP = 4
WINDOW = 128
ROT_DIM = HEAD_DIM // 4
ROPE_THETA = 500000.0
CONV_WIDTH = 4
CONV_HIST = 8
LRU_C = 8.0
LRU_BLOCK_DIM = 64
PAST_LEN = 16384
LANES = 128
SUBLANES = 8
SOFTMAX_ROWS = 256
TOKEN_TILES = 8
MIB = 1024 * 1024


def _bdot(a, b):
    return jnp.dot(a, b, preferred_element_type=F32)


def _rms_bf16(x, g):
    ms = jnp.mean(x * x, axis=-1, keepdims=True)
    return ((x * lax.rsqrt(ms + RMS_EPS)) * g).astype(BF16)


def _sigmoid(x):
    return 0.5 * jnp.tanh(0.5 * x) + 0.5


def _params(sem, vmem_mib):
    return pltpu.CompilerParams(dimension_semantics=sem, vmem_limit_bytes=vmem_mib * MIB)


def _ffn_kernel(xp_ref, xs_ref, g_ref, wg_ref, wu_ref, wd_ref, op_ref, os_ref, xn_ref):
    tp = xp_ref.shape[0]

    @pl.when(pl.program_id(1) == 0)
    def _():
        xp = xp_ref[...]
        xs = xs_ref[...]
        xn_ref[0:tp, :] = _rms_bf16(xp, g_ref[...])
        xn_ref[tp:, :] = _rms_bf16(xs, g_ref[...])
        op_ref[...] = xp
        os_ref[...] = xs

    xn = xn_ref[...]
    g = _bdot(xn, wg_ref[...].astype(BF16))
    u = _bdot(xn, wu_ref[...].astype(BF16))
    h = (FFN_RES_WEIGHT * ((g * _sigmoid(g)) * u)).astype(BF16)
    res = _bdot(h, wd_ref[...].astype(BF16))
    op_ref[...] += res[:tp]
    os_ref[...] += res[tp:]


def _ffn_call(xp, xs, norm, w_gu, w_down, layer, *, ntiles, tf):
    Tp, D = xp.shape
    Ts = xs.shape[0]
    tp, ts = Tp // ntiles, Ts // ntiles
    dff = w_down.shape[1]
    nj = dff // tf
    rows = lambda t: pl.BlockSpec((t, D), lambda i, j: (i, 0))
    return pl.pallas_call(
        _ffn_kernel,
        grid=(ntiles, nj),
        in_specs=[
            rows(tp), rows(ts),
            pl.BlockSpec((None, 1, D), lambda i, j: (layer, 0, 0)),
            pl.BlockSpec((None, D, tf), lambda i, j: (layer, 0, j)),
            pl.BlockSpec((None, D, tf), lambda i, j: (layer, 0, j + nj)),
            pl.BlockSpec((None, tf, D), lambda i, j: (layer, j, 0)),
        ],
        out_specs=[rows(tp), rows(ts)],
        out_shape=[jax.ShapeDtypeStruct((Tp, D), F32), jax.ShapeDtypeStruct((Ts, D), F32)],
        scratch_shapes=[pltpu.VMEM((tp + ts, D), BF16)],
        compiler_params=_params(("arbitrary", "arbitrary"), 60),
        name="ffn",
    )(xp, xs, norm, w_gu, w_gu, w_down)


def _inproj_kernel(xp_ref, xs_ref, g_ref, w_ref, zp_ref, zs_ref, xn_ref):
    tp = xp_ref.shape[0]

    @pl.when(pl.program_id(1) == 0)
    def _():
        xn_ref[0:tp, :] = _rms_bf16(xp_ref[...], g_ref[...])
        xn_ref[tp:, :] = _rms_bf16(xs_ref[...], g_ref[...])

    res = _bdot(xn_ref[...], w_ref[...].astype(BF16))
    zp_ref[...] = res[:tp]
    zs_ref[...] = res[tp:]


def _inproj_call(xp, xs, norm, w_in, layer, ncols, *, ntiles, tn):
    Tp, D = xp.shape
    Ts = xs.shape[0]
    tp, ts = Tp // ntiles, Ts // ntiles
    rows = lambda t: pl.BlockSpec((t, D), lambda i, j: (i, 0))
    cols = lambda t: pl.BlockSpec((t, tn), lambda i, j: (i, j))
    return pl.pallas_call(
        _inproj_kernel,
        grid=(ntiles, ncols // tn),
        in_specs=[
            rows(tp), rows(ts),
            pl.BlockSpec((None, 1, D), lambda i, j: (layer, 0, 0)),
            pl.BlockSpec((None, D, tn), lambda i, j: (layer, 0, j)),
        ],
        out_specs=[cols(tp), cols(ts), rows(tp + ts)],
        out_shape=[jax.ShapeDtypeStruct((Tp, ncols), F32), jax.ShapeDtypeStruct((Ts, ncols), F32),
                   jax.ShapeDtypeStruct((Tp + Ts, D), BF16)],
        compiler_params=_params(("arbitrary", "arbitrary"), 56),
        name="inproj",
    )(xp, xs, norm, w_in)


def _merge_kernel(xn_ref, wg0_ref, wg1_ref, wg2_ref, b0p_ref, b0s_ref, b1p_ref, b1s_ref, b2p_ref, b2s_ref,
                  w0_ref, w1_ref, w2_ref, o_ref, lhs_ref):
    tp = b0p_ref.shape[0]

    @pl.when(pl.program_id(1) == 0)
    def _():
        for b, (p_ref, s_ref) in enumerate(((b0p_ref, b0s_ref), (b1p_ref, b1s_ref), (b2p_ref, b2s_ref))):
            lhs_ref[b, 0:tp, :] = p_ref[...]
            lhs_ref[b, tp:, :] = s_ref[...]

    xn = xn_ref[...]

    def term(b, wg_ref, w_ref):
        gate = _sigmoid(_bdot(xn, wg_ref[...].astype(BF16)))
        return gate * _bdot(lhs_ref[b], w_ref[...].astype(BF16))

    m = term(0, wg0_ref, w0_ref) + term(1, wg1_ref, w1_ref) + term(2, wg2_ref, w2_ref)
    o_ref[...] = m.astype(BF16)


def _merge_call(xn, w_in, gate_col0, branches, branch_ws, layer, *, ntiles, tc):
    T, D = xn.shape
    tm = T // ntiles
    W = branches[0][0].shape[1]
    tp, ts = branches[0][0].shape[0] // ntiles, branches[0][1].shape[0] // ntiles
    g0 = gate_col0 // tc
    gstep = D // tc
    gate_w = lambda b: pl.BlockSpec((None, D, tc), lambda i, c: (layer, 0, g0 + b * gstep + c))
    rows = lambda t: pl.BlockSpec((t, W), lambda i, c: (i, 0))
    w_spec = pl.BlockSpec((None, W, tc), lambda i, c: (layer, 0, c))
    return pl.pallas_call(
        _merge_kernel,
        grid=(ntiles, D // tc),
        in_specs=[pl.BlockSpec((tm, D), lambda i, c: (i, 0)), gate_w(0), gate_w(1), gate_w(2)]
                 + [rows(tp), rows(ts)] * 3 + [w_spec] * 3,
        out_specs=pl.BlockSpec((tm, tc), lambda i, c: (i, c)),
        out_shape=jax.ShapeDtypeStruct((T, D), BF16),
        scratch_shapes=[pltpu.VMEM((3, tm, W), BF16)],
        compiler_params=_params(("arbitrary", "arbitrary"), 58),
        name="merge",
    )(xn, w_in, w_in, w_in, *[a for pair in branches for a in pair], *branch_ws)


def _outproj_kernel(xp_ref, xs_ref, m_ref, w_ref, op_ref, os_ref):
    tp = xp_ref.shape[0]
    res = _bdot(m_ref[...], w_ref[...].astype(BF16))
    op_ref[...] = xp_ref[...] + res[:tp]
    os_ref[...] = xs_ref[...] + res[tp:]


def _outproj_call(xp, xs, m, w_out, layer, *, ntiles, tn):
    Tp, D = xp.shape
    Ts = xs.shape[0]
    tp, ts = Tp // ntiles, Ts // ntiles
    cols = lambda t: pl.BlockSpec((t, tn), lambda i, c: (i, c))
    return pl.pallas_call(
        _outproj_kernel,
        grid=(ntiles, D // tn),
        in_specs=[
            cols(tp), cols(ts),
            pl.BlockSpec((tp + ts, D), lambda i, c: (i, 0)),
            pl.BlockSpec((None, D, tn), lambda i, c: (layer, 0, c)),
        ],
        out_specs=[cols(tp), cols(ts)],
        out_shape=[jax.ShapeDtypeStruct((Tp, D), F32), jax.ShapeDtypeStruct((Ts, D), F32)],
        compiler_params=_params(("arbitrary", "arbitrary"), 56),
        name="outproj",
    )(xp, xs, m, w_out)


def _pool_kernel(*refs, bs, tt, nt, pos0, has_past):
    if has_past:
        u_ref, past_ref, w_ref, s_ref, o_ref, np_ref, e_ref = refs
    else:
        u_ref, w_ref, s_ref, o_ref, np_ref, e_ref = refs
    t = pl.program_id(1)
    C = e_ref.shape[-1]
    H = POOL_HIST

    @pl.when(t == 0)
    def _():
        if has_past:
            e_ref[:, 0:H, :] = past_ref[...]
        else:
            e_ref[:, 0:H, :] = jnp.zeros((bs, H, C), F32)

    e_ref[:, H:H + tt, :] = u_ref[...].reshape(bs, tt, C)
    posp1 = lax.broadcasted_iota(jnp.int32, (1, tt, 1), 1) + (t * tt + pos0 + 1)
    gd = C // len(POOL_WINDOWS)
    for g, w in enumerate(POOL_WINDOWS):
        sl = slice(g * gd, (g + 1) * gd)
        e = e_ref[:, :, sl]
        p = e
        s = 1
        while s < w:
            p = p + pltpu.roll(p, s, 1)
            s *= 2
        cnt = jnp.minimum(posp1, w).astype(F32)
        d = (p[:, H:, :] / cnt - e[:, H:, :]).reshape(bs * tt, gd).astype(BF16)
        out = _bdot(d, w_ref[g].astype(BF16)) * s_ref[:, sl]
        o_ref[:, sl] = out.astype(BF16)

    carry = e_ref[:, tt:tt + H, :]
    e_ref[:, 0:H, :] = carry

    @pl.when(t == nt - 1)
    def _():
        np_ref[...] = carry


def _pool_call(z, past, pool_w, pool_scale, layer, *, nseq, seqlen, row0, bs, tt, pos0):
    C = pool_w.shape[1] * pool_w.shape[2]
    nt = seqlen // tt
    rb0 = row0 // (bs * tt)
    has_past = past is not None
    in_specs = [pl.BlockSpec((bs * tt, C), lambda b, t: (rb0 + b * nt + t, 0))]
    args = [z]
    if has_past:
        in_specs.append(pl.BlockSpec((None, bs, POOL_HIST, C), lambda b, t: (layer, b, 0, 0)))
        args.append(past)
    in_specs += [
        pl.BlockSpec((None,) + pool_w.shape[1:], lambda b, t: (layer, 0, 0, 0)),
        pl.BlockSpec((None, 1, C), lambda b, t: (layer, 0, 0)),
    ]
    args += [pool_w, pool_scale]
    return pl.pallas_call(
        functools.partial(_pool_kernel, bs=bs, tt=tt, nt=nt, pos0=pos0, has_past=has_past),
        grid=(nseq // bs, nt),
        in_specs=in_specs,
        out_specs=[
            pl.BlockSpec((bs * tt, C), lambda b, t: (b * nt + t, 0)),
            pl.BlockSpec((bs, POOL_HIST, C), lambda b, t: (b, 0, 0)),
        ],
        out_shape=[
            jax.ShapeDtypeStruct((nseq * seqlen, C), BF16),
            jax.ShapeDtypeStruct((nseq, POOL_HIST, C), F32),
        ],
        scratch_shapes=[pltpu.VMEM((bs, POOL_HIST + tt, C), F32)],
        compiler_params=_params(("arbitrary", "arbitrary"), 48),
        name="pool",
    )(*args)


def _lru_kernel(*refs, bs, tt, nt, has_past):
    if has_past:
        (x_ref, g_ref, cp_ref, h0_ref, cw_ref, cb_ref, wa_ref, ba_ref, wx_ref, bx_ref, lam_ref,
         y_ref, nc_ref, nh_ref, xe_ref, h_ref, a_ref, b_ref, bd_ref) = refs
    else:
        (x_ref, g_ref, cw_ref, cb_ref, wa_ref, ba_ref, wx_ref, bx_ref, lam_ref,
         y_ref, nc_ref, nh_ref, xe_ref, h_ref, a_ref, b_ref, bd_ref) = refs
    t = pl.program_id(2)
    Cc = xe_ref.shape[-1]
    R = bs * tt
    ncol = Cc // LANES
    HC = CONV_HIST
    blk = LRU_BLOCK_DIM

    @pl.when(t == 0)
    def _():
        if has_past:
            xe_ref[:, 0:HC, :] = cp_ref[...]
            h_ref[...] = jnp.broadcast_to(h0_ref[...], (bs, SUBLANES, Cc))
        else:
            xe_ref[:, 0:HC, :] = jnp.zeros((bs, HC, Cc), F32)
            h_ref[...] = jnp.zeros((bs, SUBLANES, Cc), F32)
        rep = (lax.broadcasted_iota(jnp.int32, (blk, LANES), 0)
               == (lax.broadcasted_iota(jnp.int32, (blk, LANES), 1) & (blk - 1))).astype(BF16)
        diag = ((lax.broadcasted_iota(jnp.int32, (LANES, LANES), 0) >= blk)
                == (lax.broadcasted_iota(jnp.int32, (LANES, LANES), 1) >= blk))
        for p in range(ncol):
            for k, w_ref in enumerate((wa_ref, wx_ref)):
                w2 = w_ref[2 * p:2 * p + 2].reshape(2 * blk, blk).astype(BF16)
                full = _bdot(w2, rep)
                bd_ref[p, :, k * LANES:(k + 1) * LANES] = jnp.where(diag, full, 0.0).astype(BF16)

    xe_ref[:, HC:HC + tt, :] = x_ref[...].reshape(bs, tt, Cc)
    xe = xe_ref[...]
    cw = cw_ref[...]
    xc = cb_ref[...] + pltpu.roll(xe, 3, 1)[:, HC:, :] * cw[0:1]
    xc = xc + pltpu.roll(xe, 2, 1)[:, HC:, :] * cw[1:2]
    xc = xc + pltpu.roll(xe, 1, 1)[:, HC:, :] * cw[2:3]
    xc = xc + xe[:, HC:, :] * cw[3:4]
    xc = xc.reshape(R, Cc)

    row8 = lax.broadcasted_iota(jnp.int32, (1, SUBLANES, 1), 1)
    for p in range(ncol):
        col = slice(p * LANES, (p + 1) * LANES)
        xcp = xc[:, col]
        pre = _bdot(xcp.astype(BF16), bd_ref[p])
        r = _sigmoid(pre[:, :LANES] + ba_ref[:, col])
        i = _sigmoid(pre[:, LANES:] + bx_ref[:, col])
        nl = -lam_ref[:, col]
        sp = jnp.maximum(nl, 0.0) + jnp.log1p(jnp.exp(-jnp.abs(nl)))
        la = (-LRU_C * r) * sp
        a = jnp.exp(la)
        bv = jnp.sqrt(jnp.tanh(-la) * (a * a + 1.0)) * (i * xcp)
        a = a.reshape(R // SUBLANES, SUBLANES, LANES)
        bv = bv.reshape(R // SUBLANES, SUBLANES, LANES)
        for s in (1, 2, 4):
            keep = row8 >= s
            a_sh = pltpu.roll(a, s, 1)
            b_sh = pltpu.roll(bv, s, 1)
            bv = jnp.where(keep, a * b_sh + bv, bv)
            a = jnp.where(keep, a * a_sh, a)
        a_ref[:, :, col] = a.reshape(bs, tt, LANES)
        b_ref[:, :, col] = bv.reshape(bs, tt, LANES)

    def carry_step(k, h):
        o = pl.multiple_of(k * SUBLANES, SUBLANES)
        hb = a_ref[:, pl.ds(o, SUBLANES), :] * h + b_ref[:, pl.ds(o, SUBLANES), :]
        b_ref[:, pl.ds(o, SUBLANES), :] = hb
        return jnp.broadcast_to(hb[:, SUBLANES - 1:SUBLANES, :], hb.shape)

    ngroups = tt // SUBLANES
    h_ref[...] = lax.fori_loop(0, ngroups, carry_step, h_ref[...], unroll=min(ngroups, SUBLANES))
    for p in range(ncol):
        col = slice(p * LANES, (p + 1) * LANES)
        hs = b_ref[:, :, col].reshape(R, LANES)
        y_ref[:, col] = (hs * jax.nn.gelu(g_ref[:, col], approximate=True)).astype(BF16)

    tail = xe_ref[:, tt:tt + HC, :]
    xe_ref[:, 0:HC, :] = tail

    @pl.when(t == nt - 1)
    def _():
        nc_ref[...] = tail
        nh_ref[...] = h_ref[:, 0:1, :]


def _lru_call(z, conv_past, h0, conv_w, conv_b, wa, ba, wx, bx, lam, layer, xcol0, gcol0,
              *, nseq, seqlen, row0, bs, tt, cc):
    C = conv_w.shape[2]
    nt = seqlen // tt
    nc = C // cc
    rb0 = row0 // (bs * tt)
    xb0 = xcol0 // cc
    gb0 = gcol0 // cc
    nblk = cc // LRU_BLOCK_DIM
    has_past = conv_past is not None
    in_specs = [
        pl.BlockSpec((bs * tt, cc), lambda b, c, t: (rb0 + b * nt + t, xb0 + c)),
        pl.BlockSpec((bs * tt, cc), lambda b, c, t: (rb0 + b * nt + t, gb0 + c)),
    ]
    args = [z, z]
    if has_past:
        in_specs += [
            pl.BlockSpec((None, bs, CONV_HIST, cc), lambda b, c, t: (layer, b, 0, c)),
            pl.BlockSpec((None, bs, 1, cc), lambda b, c, t: (layer, b, 0, c)),
        ]
        args += [conv_past, h0]
    vec = pl.BlockSpec((None, 1, cc), lambda b, c, t: (layer, 0, c))
    gw = pl.BlockSpec((None, nblk, LRU_BLOCK_DIM, LRU_BLOCK_DIM), lambda b, c, t: (layer, c, 0, 0))
    in_specs += [pl.BlockSpec((None, CONV_WIDTH, cc), lambda b, c, t: (layer, 0, c)), vec, gw, vec, gw, vec, vec]
    args += [conv_w, conv_b, wa, ba, wx, bx, lam]
    return pl.pallas_call(
        functools.partial(_lru_kernel, bs=bs, tt=tt, nt=nt, has_past=has_past),
        grid=(nseq // bs, nc, nt),
        in_specs=in_specs,
        out_specs=[
            pl.BlockSpec((bs * tt, cc), lambda b, c, t: (b * nt + t, c)),
            pl.BlockSpec((bs, CONV_HIST, cc), lambda b, c, t: (b, 0, c)),
            pl.BlockSpec((bs, 1, cc), lambda b, c, t: (b, 0, c)),
        ],
        out_shape=[
            jax.ShapeDtypeStruct((nseq * seqlen, C), BF16),
            jax.ShapeDtypeStruct((nseq, CONV_HIST, C), F32),
            jax.ShapeDtypeStruct((nseq, 1, C), F32),
        ],
        scratch_shapes=[
            pltpu.VMEM((bs, CONV_HIST + tt, cc), F32),
            pltpu.VMEM((bs, SUBLANES, cc), F32),
            pltpu.VMEM((bs, tt, cc), F32),
            pltpu.VMEM((bs, tt, cc), F32),
            pltpu.VMEM((cc // LANES, LANES, 2 * LANES), BF16),
        ],
        compiler_params=_params(("arbitrary", "arbitrary", "arbitrary"), 48),
        name="lru",
    )(*args)


def _attn_stream_kernel(q_ref, kv_ref, qn_ref, kn_ref, cos_ref, s1_ref, s2_ref, sink_ref,
                        o_ref, nk_ref, nv_ref, kb_ref, vb_ref, *, qb, nb, pos0, layer):
    n = pl.program_id(1)
    KV = kb_ref.shape[-1]
    nkeys = kb_ref.shape[0]
    half = HEAD_DIM
    lane = lax.broadcasted_iota(jnp.int32, (1, LANES), 1)
    seg_ones = ((lax.broadcasted_iota(jnp.int32, (LANES, LANES), 0) >= half)
                == (lax.broadcasted_iota(jnp.int32, (LANES, LANES), 1) >= half)).astype(BF16)
    cosw = cos_ref[...]
    s1w = s1_ref[...]
    s2w = s2_ref[...]
    win = lambda w: slice(w * LANES, (w + 1) * LANES)

    def norm_rot(xw, gain):
        sq = xw * xw
        hi = sq.astype(BF16)
        lo = (sq - hi.astype(F32)).astype(BF16)
        ss = _bdot(hi, seg_ones) + _bdot(lo, seg_ones)
        y = (xw * lax.rsqrt(ss * (1.0 / HEAD_DIM) + RMS_EPS)) * gain
        return y * cosw + pltpu.roll(y, LANES - ROT_DIM // 2, 1) * s1w + pltpu.roll(y, ROT_DIM // 2, 1) * s2w

    qi = lax.broadcasted_iota(jnp.int32, (qb, nkeys), 0)
    si = lax.broadcasted_iota(jnp.int32, (qb, nkeys), 1)
    kpos0 = pos0 + n * qb - WINDOW
    valid = (si >= qi) & (si <= qi + WINDOW) & (si + kpos0 >= 0)

    @pl.when(n == 0)
    def _():
        kb_ref[0:WINDOW, :] = jnp.zeros((WINDOW, KV), F32)
        vb_ref[0:WINDOW, :] = jnp.zeros((WINDOW, KV), F32)

    kvx = kv_ref[...]
    for w in range(KV // LANES):
        kb_ref[WINDOW:, win(w)] = norm_rot(kvx[:, win(w)], kn_ref[...])
    vb_ref[WINDOW:, :] = kvx[:, KV:]
    qx = q_ref[...]
    for kv in range(N_KV_HEADS):
        w, off = divmod(kv, 2)
        in_head = (lane >= half) == (off == 1)

        def spread(ref):
            lo = jnp.where(in_head, ref[:, win(w)], 0.0)
            if off == 1:
                lo = pltpu.roll(lo, half, 1)
            return jnp.concatenate([lo, pltpu.roll(lo, half, 1)], axis=0).astype(BF16)

        kk = spread(kb_ref)
        vv = spread(vb_ref)
        qst = jnp.concatenate([norm_rot(qx[:, win(2 * kv + h)], qn_ref[...]) for h in range(2)],
                              axis=0).astype(BF16)
        sc = lax.dot_general(qst, kk, (((1,), (1,)), ((), ())),
                             preferred_element_type=F32) * (HEAD_DIM ** -0.5)
        prob_rows = []
        for rh in range(2):
            prob_cols = []
            for ch in range(2):
                sink = sink_ref[layer, kv * GQA_GROUP + 2 * rh + ch]
                sblk = jnp.where(valid, sc[rh * qb:(rh + 1) * qb, ch * nkeys:(ch + 1) * nkeys], NEG_INF)
                m = jnp.maximum(jnp.max(sblk, axis=-1, keepdims=True), sink)
                p = jnp.exp(sblk - m)
                den = jnp.sum(p, axis=-1, keepdims=True) + jnp.exp(sink - m)
                prob_cols.append(p * (1.0 / den))
            prob_rows.append(jnp.concatenate(prob_cols, axis=1))
        probs = jnp.concatenate(prob_rows, axis=0).astype(BF16)
        out = _bdot(probs, vv)
        for rh in range(2):
            o_ref[:, win(2 * kv + rh)] = out[rh * qb:(rh + 1) * qb].astype(BF16)
    new_k = kb_ref[WINDOW:, :]
    new_v = vb_ref[WINDOW:, :]
    kb_ref[0:WINDOW, :] = new_k
    vb_ref[0:WINDOW, :] = new_v

    @pl.when(n == nb - 1)
    def _():
        nk_ref[0] = new_k
        nv_ref[0] = new_v


def _attn_seq_kernel(q_ref, kv_ref, kp_ref, vp_ref, qn_ref, kn_ref, cos_ref, s1_ref, s2_ref, sink_ref,
                     o_ref, nk_ref, nv_ref, q_scr, k_scr, o_scr, *, bs, qb, pos0, layer):
    KV = k_scr.shape[-1]
    nkeys = 2 * WINDOW
    half = HEAD_DIM
    lane = lax.broadcasted_iota(jnp.int32, (1, LANES), 1)
    seg_ones = ((lax.broadcasted_iota(jnp.int32, (LANES, LANES), 0) >= half)
                == (lax.broadcasted_iota(jnp.int32, (LANES, LANES), 1) >= half)).astype(BF16)
    R = q_scr.shape[0]
    nq = q_scr.shape[1] // LANES
    nk = KV // LANES
    win = lambda w: slice(w * LANES, (w + 1) * LANES)

    xs = jnp.concatenate([q_ref[:, win(w)] for w in range(nq)] + [kv_ref[:, win(w)] for w in range(nk)], axis=0)
    sq = xs * xs
    hi = sq.astype(BF16)
    lo = (sq - hi.astype(F32)).astype(BF16)
    ss = _bdot(hi, seg_ones) + _bdot(lo, seg_ones)
    y = (xs * lax.rsqrt(ss * (1.0 / HEAD_DIM) + RMS_EPS)).reshape(nq + nk, R, LANES)
    gains = jnp.concatenate([jnp.broadcast_to(qn_ref[...] * (HEAD_DIM ** -0.5), (nq, 1, LANES)),
                             jnp.broadcast_to(kn_ref[...], (nk, 1, LANES))], axis=0)
    y = y * gains
    y = (y * cos_ref[...] + pltpu.roll(y, LANES - ROT_DIM // 2, 2) * s1_ref[...]
         + pltpu.roll(y, ROT_DIM // 2, 2) * s2_ref[...])
    for w in range(nq):
        q_scr[:, win(w)] = y[w]
    for w in range(nk):
        k_scr[:, win(w)] = y[nq + w]

    group = max(1, min(N_KV_HEADS, SOFTMAX_ROWS // (2 * qb)))
    nrow = 2 * group * qb
    qi = lax.broadcasted_iota(jnp.int32, (nrow, nkeys), 0) & (qb - 1)
    si = lax.broadcasted_iota(jnp.int32, (nrow, nkeys), 1)
    valid = (si >= qi) & (si <= qi + WINDOW) & (si + (pos0 - WINDOW) >= 0)
    sink_cols = [
        [jnp.concatenate([jnp.full((qb, 1), sink_ref[layer, kv * GQA_GROUP + 2 * rh + ch], F32)
                          for kv in range(g0, g0 + group) for rh in range(2)], axis=0)
         for ch in range(2)]
        for g0 in range(0, N_KV_HEADS, group)]

    def seq_step(s, c):
        rows = pl.ds(pl.multiple_of(s * qb, qb), qb)
        prev_k = kp_ref[s]
        prev_v = vp_ref[s]
        cur_k = k_scr[rows, :]
        cur_v = kv_ref[rows, KV:]
        pad = jnp.zeros((WINDOW - qb, KV), F32)
        k_all = jnp.concatenate([prev_k, cur_k, pad], axis=0)
        v_all = jnp.concatenate([prev_v, cur_v, pad], axis=0)

        def spread(x, kv):
            w, off = divmod(kv, 2)
            lo = jnp.where((lane >= half) == (off == 1), x[:, win(w)], 0.0)
            if off == 1:
                lo = pltpu.roll(lo, half, 1)
            return jnp.concatenate([lo, pltpu.roll(lo, half, 1)], axis=0).astype(BF16)

        for gi, g0 in enumerate(range(0, N_KV_HEADS, group)):
            scores = []
            for kv in range(g0, g0 + group):
                qst = jnp.concatenate([q_scr[rows, win(2 * kv + h)] for h in range(2)], axis=0).astype(BF16)
                scores.append(lax.dot_general(qst, spread(k_all, kv), (((1,), (1,)), ((), ())),
                                              preferred_element_type=F32))
            sc = jnp.concatenate(scores, axis=0)
            prob_cols = []
            for ch in range(2):
                sink = sink_cols[gi][ch]
                sblk = jnp.where(valid, sc[:, ch * nkeys:(ch + 1) * nkeys], NEG_INF)
                m = jnp.maximum(jnp.max(sblk, axis=-1, keepdims=True), sink)
                p = jnp.exp(sblk - m)
                den = jnp.sum(p, axis=-1, keepdims=True) + jnp.exp(sink - m)
                prob_cols.append(p * (1.0 / den))
            probs = jnp.concatenate(prob_cols, axis=1).astype(BF16)
            for j, kv in enumerate(range(g0, g0 + group)):
                out = _bdot(probs[2 * j * qb:2 * (j + 1) * qb], spread(v_all, kv))
                for rh in range(2):
                    o_scr[rows, win(2 * kv + rh)] = out[rh * qb:(rh + 1) * qb]
        nk_ref[s] = jnp.concatenate([prev_k[qb:], cur_k], axis=0)
        nv_ref[s] = jnp.concatenate([prev_v[qb:], cur_v], axis=0)
        return c

    lax.fori_loop(0, bs, seq_step, 0, unroll=min(bs, 4))
    o_ref[...] = o_scr[...].astype(BF16)


def _attn_call(z, k_past, v_past, qn, kn, cos_t, s1_t, s2_t, sinks, layer, qcol0, kvcol0,
               *, nseq, seqlen, row0, bs, qb, pos0):
    A = N_KV_HEADS * GQA_GROUP * HEAD_DIM
    KV = N_KV_HEADS * HEAD_DIM
    nb = seqlen // qb
    rb0 = row0 // (bs * qb)
    has_past = k_past is not None
    R = bs * qb
    if has_past:
        assert nb == 1 and qb < WINDOW
        body = functools.partial(_attn_seq_kernel, bs=bs, qb=qb, pos0=pos0, layer=layer)
        scratch = [pltpu.VMEM((R, A), F32), pltpu.VMEM((R, KV), F32), pltpu.VMEM((R, A), F32)]
    else:
        assert bs == 1 and qb == WINDOW
        body = functools.partial(_attn_stream_kernel, qb=qb, nb=nb, pos0=pos0, layer=layer)
        scratch = [pltpu.VMEM((2 * WINDOW, KV), F32), pltpu.VMEM((2 * WINDOW, KV), F32)]
    in_specs = [
        pl.BlockSpec((bs * qb, A), lambda b, n: (rb0 + b * nb + n, qcol0 // A)),
        pl.BlockSpec((bs * qb, 2 * KV), lambda b, n: (rb0 + b * nb + n, kvcol0 // (2 * KV))),
    ]
    args = [z, z]
    if has_past:
        past_spec = pl.BlockSpec((None, bs, WINDOW, KV), lambda b, n: (layer, b, 0, 0))
        in_specs += [past_spec, past_spec]
        args += [k_past, v_past]
    gain = pl.BlockSpec((None, 1, LANES), lambda b, n: (layer, 0, 0))
    tab = pl.BlockSpec((R, LANES), lambda b, n: (n, 0))
    in_specs += [gain, gain, tab, tab, tab, pl.BlockSpec(memory_space=pltpu.SMEM)]
    args += [qn, kn, cos_t, s1_t, s2_t, sinks]
    state = pl.BlockSpec((bs, WINDOW, KV), lambda b, n: (b, 0, 0))
    return pl.pallas_call(
        body,
        grid=(nseq // bs, nb),
        in_specs=in_specs,
        out_specs=[pl.BlockSpec((bs * qb, A), lambda b, n: (b * nb + n, 0)), state, state],
        out_shape=[
            jax.ShapeDtypeStruct((nseq * seqlen, A), BF16),
            jax.ShapeDtypeStruct((nseq, WINDOW, KV), F32),
            jax.ShapeDtypeStruct((nseq, WINDOW, KV), F32),
        ],
        scratch_shapes=scratch,
        compiler_params=_params(("arbitrary", "arbitrary"), 56),
        name="attn",
    )(*args)


def _rope_tables(pos, nrep):
    half = ROT_DIM // 2
    inv = ROPE_THETA ** (-jnp.arange(half, dtype=F32) / half)
    ang = pos.astype(F32)[:, None] * inv[None, :]
    cos, sin = jnp.cos(ang), jnp.sin(ang)
    T = pos.shape[0]
    ones = jnp.ones((T, HEAD_DIM - ROT_DIM), F32)
    zeros = jnp.zeros((T, HEAD_DIM - ROT_DIM), F32)
    zh = jnp.zeros((T, half), F32)
    cos_t = jnp.concatenate([cos, cos, ones], axis=1)
    s1_t = jnp.concatenate([-sin, zh, zeros], axis=1)
    s2_t = jnp.concatenate([zh, sin, zeros], axis=1)
    rep = LANES // HEAD_DIM
    return tuple(jnp.tile(t, (nrep, rep)) for t in (cos_t, s1_t, s2_t))


def kernel(x_prompt, x_sample, state_pool, cache_k_win, cache_v_win, state_conv, state_rglru, norm_ffa, ffa_w_gu, ffa_w_down, norm_mix, w_in, pool_w, pool_scale, q_norm, k_norm, attn_sinks, conv_w, conv_b, lru_gate_a_w, lru_gate_a_b, lru_gate_x_w, lru_gate_x_b, lru_lambda, w_branch_pool, w_branch_attn, w_branch_lru, w_out, norm_ffb, ffb_w_gu, ffb_w_down):
    Bp, Sp, D = x_prompt.shape
    Bs, Ss, _ = x_sample.shape
    L = norm_ffa.shape[0]
    Tp, Ts = Bp * Sp, Bs * Ss
    pool_c = pool_scale.shape[1]
    attn_c = N_KV_HEADS * GQA_GROUP * HEAD_DIM
    kv_c = N_KV_HEADS * HEAD_DIM
    lru_c = conv_w.shape[2]
    q0 = pool_c
    kv0 = q0 + attn_c
    xl0 = kv0 + 2 * kv_c
    gl0 = xl0 + lru_c
    gate0 = gl0 + lru_c

    tok = dict(ntiles=TOKEN_TILES)
    xp = x_prompt.reshape(Tp, D)
    xs = x_sample.reshape(Ts, D)

    vec3 = lambda a: a.reshape(L, 1, a.shape[-1])
    norm_ffa3, norm_mix3, norm_ffb3 = vec3(norm_ffa), vec3(norm_mix), vec3(norm_ffb)
    pool_scale3 = vec3(pool_scale)
    conv_b3, ba3, bx3, lam3 = vec3(conv_b), vec3(lru_gate_a_b), vec3(lru_gate_x_b), vec3(lru_lambda)
    qn3 = vec3(jnp.tile(q_norm, (1, LANES // HEAD_DIM)))
    kn3 = vec3(jnp.tile(k_norm, (1, LANES // HEAD_DIM)))
    pool_past = jnp.pad(state_pool, ((0, 0), (0, 0), (POOL_HIST - POOL_KEEP, 0), (0, 0)))
    conv_past = jnp.pad(state_conv, ((0, 0), (0, 0), (CONV_HIST - (CONV_WIDTH - 1), 0), (0, 0)))
    h0 = state_rglru.reshape(L, Bs, 1, lru_c)
    k_past = cache_k_win.reshape(L, Bs, WINDOW, kv_c)
    v_past = cache_v_win.reshape(L, Bs, WINDOW, kv_c)
    rope_p = _rope_tables(jnp.arange(Sp), 1)
    rope_s = _rope_tables(PAST_LEN + jnp.arange(Ss), Bs)

    prompt = dict(nseq=Bp, seqlen=Sp, row0=0)
    sample = dict(nseq=Bs, seqlen=Ss, row0=0)
    st_p = ([], [], [], [], [])
    st_s = ([], [], [], [], [])
    for l in range(L):
        xp, xs = _ffn_call(xp, xs, norm_ffa3, ffa_w_gu, ffa_w_down, l, tf=256, **tok)
        zp, zs, xn = _inproj_call(xp, xs, norm_mix3, w_in, l, gate0, tn=768, **tok)

        pool_p, np_p = _pool_call(zp, None, pool_w, pool_scale3, l, bs=1, tt=512, pos0=0, **prompt)
        pool_s, np_s = _pool_call(zs, pool_past, pool_w, pool_scale3, l, bs=Bs, tt=Ss, pos0=PAST_LEN, **sample)

        att_p, nk_p, nv_p = _attn_call(zp, None, None, qn3, kn3, *rope_p, attn_sinks, l, q0, kv0,
                                       bs=1, qb=WINDOW, pos0=0, **prompt)
        att_s, nk_s, nv_s = _attn_call(zs, k_past, v_past, qn3, kn3, *rope_s, attn_sinks, l, q0, kv0,
                                       bs=Bs, qb=Ss, pos0=PAST_LEN, **sample)

        lru_args = (conv_w, conv_b3, lru_gate_a_w, ba3, lru_gate_x_w, bx3, lam3, l, xl0, gl0)
        lru_p, nc_p, nh_p = _lru_call(zp, None, None, *lru_args, bs=1, tt=512, cc=512, **prompt)
        lru_s, nc_s, nh_s = _lru_call(zs, conv_past, h0, *lru_args, bs=Bs, tt=Ss, cc=512, **sample)

        m = _merge_call(xn, w_in, gate0, ((pool_p, pool_s), (att_p, att_s), (lru_p, lru_s)),
                        (w_branch_pool, w_branch_attn, w_branch_lru), l, tc=256, **tok)
        xp, xs = _outproj_call(xp, xs, m, w_out, l, tn=512, **tok)
        xp, xs = _ffn_call(xp, xs, norm_ffb3, ffb_w_gu, ffb_w_down, l, tf=256, **tok)

        for lst, val in zip(st_p, (np_p, nk_p, nv_p, nc_p, nh_p)):
            lst.append(val)
        for lst, val in zip(st_s, (np_s, nk_s, nv_s, nc_s, nh_s)):
            lst.append(val)

    def states(st, nseq):
        pool = jnp.stack(st[0])[:, :, POOL_HIST - POOL_KEEP:, :]
        k = jnp.stack(st[1]).reshape(L, nseq, WINDOW, N_KV_HEADS, HEAD_DIM)
        v = jnp.stack(st[2]).reshape(L, nseq, WINDOW, N_KV_HEADS, HEAD_DIM)
        conv = jnp.stack(st[3])[:, :, CONV_HIST - (CONV_WIDTH - 1):, :]
        h = jnp.stack(st[4]).reshape(L, nseq, lru_c)
        return pool, k, v, conv, h

    pool_p, k_p, v_p, conv_p, h_p = states(st_p, Bp)
    pool_s, k_s, v_s, conv_s, h_s = states(st_s, Bs)
    y_p = xp.reshape(Bp, Sp, D)
    y_s = xs.reshape(Bs, Ss, D)
    return (y_p, y_s, pool_p, pool_s, k_p, k_s, v_p, v_s, conv_p, conv_s, h_p, h_s)
```

```python
import functools

import jax
import jax.numpy as jnp
from jax import lax
from jax.experimental import pallas as pl
from jax.experimental.pallas import tpu as pltpu

F32 = jnp.float32
BF16 = jnp.bfloat16

RMS_EPS = 1e-6
NEG_INF = -1e30
FFN_RES_WEIGHT = 0.5
POOL_WINDOWS = (2, 4, 8, 16)
POOL_KEEP = max(POOL_WINDOWS) - 1
POOL_HIST = 16
HEAD_DIM = 64
N_KV_HEADS = 4
GQA_GROUP = 4
WINDOW = 128
ROT_DIM = HEAD_DIM // 4
ROPE_THETA = 500000.0
CONV_WIDTH = 4
CONV_HIST = 8
LRU_C = 8.0
LRU_BLOCK_DIM = 64
PAST_LEN = 16384
LANES = 128
SUBLANES = 8
SOFTMAX_ROWS = 256
TOKEN_TILES = 8
MIB = 1024 * 1024


def _bdot(a, b):
    return jnp.dot(a, b, preferred_element_type=F32)


def _rms_bf16(x, g):
    ms = jnp.mean(x * x, axis=-1, keepdims=True)
    return ((x * lax.rsqrt(ms + RMS_EPS)) * g).astype(BF16)


def _sigmoid(x):
    return 0.5 * jnp.tanh(0.5 * x) + 0.5


def _params(sem, vmem_mib):
    return pltpu.CompilerParams(dimension_semantics=sem, vmem_limit_bytes=vmem_mib * MIB)


def _ffn_kernel(xp_ref, xs_ref, g_ref, wg_ref, wu_ref, wd_ref, op_ref, os_ref, xn_ref):
    tp = xp_ref.shape[0]

    @pl.when(pl.program_id(1) == 0)
    def _():
        xp = xp_ref[...]
        xs = xs_ref[...]
        xn_ref[0:tp, :] = _rms_bf16(xp, g_ref[...])
        xn_ref[tp:, :] = _rms_bf16(xs, g_ref[...])
        op_ref[...] = xp
        os_ref[...] = xs

    xn = xn_ref[...]
    g = _bdot(xn, wg_ref[...].astype(BF16))
    u = _bdot(xn, wu_ref[...].astype(BF16))
    h = (FFN_RES_WEIGHT * ((g * _sigmoid(g)) * u)).astype(BF16)
    res = _bdot(h, wd_ref[...].astype(BF16))
    op_ref[...] += res[:tp]
    os_ref[...] += res[tp:]


def _ffn_call(xp, xs, norm, w_gu, w_down, layer, *, ntiles, tf):
    Tp, D = xp.shape
    Ts = xs.shape[0]
    tp, ts = Tp // ntiles, Ts // ntiles
    dff = w_down.shape[1]
    nj = dff // tf
    rows = lambda t: pl.BlockSpec((t, D), lambda i, j: (i, 0))
    once = lambda t: pl.BlockSpec((t, D), lambda i, j: (i, 0), pipeline_mode=pl.Buffered(1))
    return pl.pallas_call(
        _ffn_kernel,
        grid=(ntiles, nj),
        in_specs=[
            once(tp), once(ts),
            pl.BlockSpec((None, 1, D), lambda i, j: (layer, 0, 0)),
            pl.BlockSpec((None, D, tf), lambda i, j: (layer, 0, j)),
            pl.BlockSpec((None, D, tf), lambda i, j: (layer, 0, j + nj)),
            pl.BlockSpec((None, tf, D), lambda i, j: (layer, j, 0)),
        ],
        out_specs=[rows(tp), rows(ts)],
        out_shape=[jax.ShapeDtypeStruct((Tp, D), F32), jax.ShapeDtypeStruct((Ts, D), F32)],
        scratch_shapes=[pltpu.VMEM((tp + ts, D), BF16)],
        compiler_params=_params(("arbitrary", "arbitrary"), 60),
        name="ffn",
    )(xp, xs, norm, w_gu, w_gu, w_down)


def _inproj_kernel(xp_ref, xs_ref, g_ref, w_ref, zp_ref, zs_ref, xn_ref):
    tp = xp_ref.shape[0]

    @pl.when(pl.program_id(1) == 0)
    def _():
        xn_ref[0:tp, :] = _rms_bf16(xp_ref[...], g_ref[...])
        xn_ref[tp:, :] = _rms_bf16(xs_ref[...], g_ref[...])

    res = _bdot(xn_ref[...], w_ref[...].astype(BF16))
    zp_ref[...] = res[:tp]
    zs_ref[...] = res[tp:]


def _inproj_call(xp, xs, norm, w_in, layer, ncols, *, ntiles, tn):
    Tp, D = xp.shape
    Ts = xs.shape[0]
    tp, ts = Tp // ntiles, Ts // ntiles
    rows = lambda t: pl.BlockSpec((t, D), lambda i, j: (i, 0))
    once = lambda t: pl.BlockSpec((t, D), lambda i, j: (i, 0), pipeline_mode=pl.Buffered(1))
    cols = lambda t: pl.BlockSpec((t, tn), lambda i, j: (i, j))
    return pl.pallas_call(
        _inproj_kernel,
        grid=(ntiles, ncols // tn),
        in_specs=[
            once(tp), once(ts),
            pl.BlockSpec((None, 1, D), lambda i, j: (layer, 0, 0)),
            pl.BlockSpec((None, D, tn), lambda i, j: (layer, 0, j)),
        ],
        out_specs=[cols(tp), cols(ts), rows(tp + ts)],
        out_shape=[jax.ShapeDtypeStruct((Tp, ncols), F32), jax.ShapeDtypeStruct((Ts, ncols), F32),
                   jax.ShapeDtypeStruct((Tp + Ts, D), BF16)],
        compiler_params=_params(("arbitrary", "arbitrary"), 60),
        name="inproj",
    )(xp, xs, norm, w_in)


def _merge_kernel(xn_ref, wg0_ref, wg1_ref, wg2_ref, b0p_ref, b0s_ref, b1p_ref, b1s_ref, b2p_ref, b2s_ref,
                  w0_ref, w1_ref, w2_ref, o_ref, lhs_ref):
    tp = b0p_ref.shape[0]

    @pl.when(pl.program_id(1) == 0)
    def _():
        for b, (p_ref, s_ref) in enumerate(((b0p_ref, b0s_ref), (b1p_ref, b1s_ref), (b2p_ref, b2s_ref))):
            lhs_ref[b, 0:tp, :] = p_ref[...]
            lhs_ref[b, tp:, :] = s_ref[...]

    xn = xn_ref[...]

    def term(b, wg_ref, w_ref):
        gate = _sigmoid(_bdot(xn, wg_ref[...].astype(BF16)))
        return gate * _bdot(lhs_ref[b], w_ref[...].astype(BF16))

    m = term(0, wg0_ref, w0_ref) + term(1, wg1_ref, w1_ref) + term(2, wg2_ref, w2_ref)
    o_ref[...] = m.astype(BF16)


def _merge_call(xn, w_in, gate_col0, branches, branch_ws, layer, *, ntiles, tc):
    T, D = xn.shape
    tm = T // ntiles
    W = branches[0][0].shape[1]
    tp, ts = branches[0][0].shape[0] // ntiles, branches[0][1].shape[0] // ntiles
    g0 = gate_col0 // tc
    gstep = D // tc
    gate_w = lambda b: pl.BlockSpec((None, D, tc), lambda i, c: (layer, 0, g0 + b * gstep + c))
    rows = lambda t: pl.BlockSpec((t, W), lambda i, c: (i, 0))
    w_spec = pl.BlockSpec((None, W, tc), lambda i, c: (layer, 0, c))
    return pl.pallas_call(
        _merge_kernel,
        grid=(ntiles, D // tc),
        in_specs=[pl.BlockSpec((tm, D), lambda i, c: (i, 0)), gate_w(0), gate_w(1), gate_w(2)]
                 + [rows(tp), rows(ts)] * 3 + [w_spec] * 3,
        out_specs=pl.BlockSpec((tm, tc), lambda i, c: (i, c)),
        out_shape=jax.ShapeDtypeStruct((T, D), BF16),
        scratch_shapes=[pltpu.VMEM((3, tm, W), BF16)],
        compiler_params=_params(("arbitrary", "arbitrary"), 58),
        name="merge",
    )(xn, w_in, w_in, w_in, *[a for pair in branches for a in pair], *branch_ws)


def _outproj_kernel(xp_ref, xs_ref, m_ref, w_ref, op_ref, os_ref):
    tp = xp_ref.shape[0]
    res = _bdot(m_ref[...], w_ref[...].astype(BF16))
    op_ref[...] = xp_ref[...] + res[:tp]
    os_ref[...] = xs_ref[...] + res[tp:]


def _outproj_call(xp, xs, m, w_out, layer, *, ntiles, tn):
    Tp, D = xp.shape
    Ts = xs.shape[0]
    tp, ts = Tp // ntiles, Ts // ntiles
    cols = lambda t: pl.BlockSpec((t, tn), lambda i, c: (i, c))
    return pl.pallas_call(
        _outproj_kernel,
        grid=(ntiles, D // tn),
        in_specs=[
            cols(tp), cols(ts),
            pl.BlockSpec((tp + ts, D), lambda i, c: (i, 0)),
            pl.BlockSpec((None, D, tn), lambda i, c: (layer, 0, c)),
        ],
        out_specs=[cols(tp), cols(ts)],
        out_shape=[jax.ShapeDtypeStruct((Tp, D), F32), jax.ShapeDtypeStruct((Ts, D), F32)],
        compiler_params=_params(("arbitrary", "arbitrary"), 56),
        name="outproj",
    )(xp, xs, m, w_out)


def _pool_kernel(*refs, bs, tt, nt, pos0, has_past):
    if has_past:
        u_ref, past_ref, w_ref, s_ref, o_ref, np_ref, e_ref = refs
    else:
        u_ref, w_ref, s_ref, o_ref, np_ref, e_ref = refs
    t = pl.program_id(1)
    C = e_ref.shape[-1]
    H = POOL_HIST

    @pl.when(t == 0)
    def _():
        if has_past:
            e_ref[:, 0:H, :] = past_ref[...]
        else:
            e_ref[:, 0:H, :] = jnp.zeros((bs, H, C), F32)

    e_ref[:, H:H + tt, :] = u_ref[...].reshape(bs, tt, C)
    posp1 = lax.broadcasted_iota(jnp.int32, (1, tt, 1), 1) + (t * tt + pos0 + 1)
    gd = C // len(POOL_WINDOWS)
    for g, w in enumerate(POOL_WINDOWS):
        sl = slice(g * gd, (g + 1) * gd)
        e = e_ref[:, :, sl]
        p = e
        s = 1
        while s < w:
            p = p + pltpu.roll(p, s, 1)
            s *= 2
        cnt = jnp.minimum(posp1, w).astype(F32)
        d = (p[:, H:, :] / cnt - e[:, H:, :]).reshape(bs * tt, gd).astype(BF16)
        out = _bdot(d, w_ref[g].astype(BF16)) * s_ref[:, sl]
        o_ref[:, sl] = out.astype(BF16)

    carry = e_ref[:, tt:tt + H, :]
    e_ref[:, 0:H, :] = carry

    @pl.when(t == nt - 1)
    def _():
        np_ref[...] = carry


def _pool_call(z, past, pool_w, pool_scale, layer, *, nseq, seqlen, row0, bs, tt, pos0):
    C = pool_w.shape[1] * pool_w.shape[2]
    nt = seqlen // tt
    rb0 = row0 // (bs * tt)
    has_past = past is not None
    in_specs = [pl.BlockSpec((bs * tt, C), lambda b, t: (rb0 + b * nt + t, 0))]
    args = [z]
    if has_past:
        in_specs.append(pl.BlockSpec((None, bs, POOL_HIST, C), lambda b, t: (layer, b, 0, 0)))
        args.append(past)
    in_specs += [
        pl.BlockSpec((None,) + pool_w.shape[1:], lambda b, t: (layer, 0, 0, 0)),
        pl.BlockSpec((None, 1, C), lambda b, t: (layer, 0, 0)),
    ]
    args += [pool_w, pool_scale]
    return pl.pallas_call(
        functools.partial(_pool_kernel, bs=bs, tt=tt, nt=nt, pos0=pos0, has_past=has_past),
        grid=(nseq // bs, nt),
        in_specs=in_specs,
        out_specs=[
            pl.BlockSpec((bs * tt, C), lambda b, t: (b * nt + t, 0)),
            pl.BlockSpec((bs, POOL_HIST, C), lambda b, t: (b, 0, 0)),
        ],
        out_shape=[
            jax.ShapeDtypeStruct((nseq * seqlen, C), BF16),
            jax.ShapeDtypeStruct((nseq, POOL_HIST, C), F32),
        ],
        scratch_shapes=[pltpu.VMEM((bs, POOL_HIST + tt, C), F32)],
        compiler_params=_params(("arbitrary", "arbitrary"), 48),
        name="pool",
    )(*args)


def _lru_kernel(*refs, bs, tt, nt, has_past):
    if has_past:
        (x_ref, g_ref, cp_ref, h0_ref, cw_ref, cb_ref, wa_ref, ba_ref, wx_ref, bx_ref, lam_ref,
         y_ref, nc_ref, nh_ref, xe_ref, h_ref, a_ref, b_ref, bd_ref) = refs
    else:
        (x_ref, g_ref, cw_ref, cb_ref, wa_ref, ba_ref, wx_ref, bx_ref, lam_ref,
         y_ref, nc_ref, nh_ref, xe_ref, h_ref, a_ref, b_ref, bd_ref) = refs
    t = pl.program_id(2)
    Cc = xe_ref.shape[-1]
    R = bs * tt
    ncol = Cc // LANES
    HC = CONV_HIST
    blk = LRU_BLOCK_DIM

    @pl.when(t == 0)
    def _():
        if has_past:
            xe_ref[:, 0:HC, :] = cp_ref[...]
            h_ref[...] = jnp.broadcast_to(h0_ref[...], (bs, SUBLANES, Cc))
        else:
            xe_ref[:, 0:HC, :] = jnp.zeros((bs, HC, Cc), F32)
            h_ref[...] = jnp.zeros((bs, SUBLANES, Cc), F32)
        rep = (lax.broadcasted_iota(jnp.int32, (blk, LANES), 0)
               == (lax.broadcasted_iota(jnp.int32, (blk, LANES), 1) & (blk - 1))).astype(BF16)
        diag = ((lax.broadcasted_iota(jnp.int32, (LANES, LANES), 0) >= blk)
                == (lax.broadcasted_iota(jnp.int32, (LANES, LANES), 1) >= blk))
        for p in range(ncol):
            for k, w_ref in enumerate((wa_ref, wx_ref)):
                w2 = w_ref[2 * p:2 * p + 2].reshape(2 * blk, blk).astype(BF16)
                full = _bdot(w2, rep)
                bd_ref[p, :, k * LANES:(k + 1) * LANES] = jnp.where(diag, full, 0.0).astype(BF16)

    xe_ref[:, HC:HC + tt, :] = x_ref[...].reshape(bs, tt, Cc)
    xe = xe_ref[...]
    cw = cw_ref[...]
    xc = cb_ref[...] + pltpu.roll(xe, 3, 1)[:, HC:, :] * cw[0:1]
    xc = xc + pltpu.roll(xe, 2, 1)[:, HC:, :] * cw[1:2]
    xc = xc + pltpu.roll(xe, 1, 1)[:, HC:, :] * cw[2:3]
    xc = xc + xe[:, HC:, :] * cw[3:4]
    xc = xc.reshape(R, Cc)

    row8 = lax.broadcasted_iota(jnp.int32, (1, SUBLANES, 1), 1)
    for p in range(ncol):
        col = slice(p * LANES, (p + 1) * LANES)
        xcp = xc[:, col]
        pre = _bdot(xcp.astype(BF16), bd_ref[p])
        r = _sigmoid(pre[:, :LANES] + ba_ref[:, col])
        i = _sigmoid(pre[:, LANES:] + bx_ref[:, col])
        nl = -lam_ref[:, col]
        sp = jnp.maximum(nl, 0.0) + jnp.log1p(jnp.exp(-jnp.abs(nl)))
        la = (-LRU_C * r) * sp
        a = jnp.exp(la)
        bv = jnp.sqrt(jnp.tanh(-la) * (a * a + 1.0)) * (i * xcp)
        a = a.reshape(R // SUBLANES, SUBLANES, LANES)
        bv = bv.reshape(R // SUBLANES, SUBLANES, LANES)
        for s in (1, 2, 4):
            keep = row8 >= s
            a_sh = pltpu.roll(a, s, 1)
            b_sh = pltpu.roll(bv, s, 1)
            bv = jnp.where(keep, a * b_sh + bv, bv)
            a = jnp.where(keep, a * a_sh, a)
        a_ref[:, :, col] = a.reshape(bs, tt, LANES)
        b_ref[:, :, col] = bv.reshape(bs, tt, LANES)

    def carry_step(k, h):
        o = pl.multiple_of(k * SUBLANES, SUBLANES)
        hb = a_ref[:, pl.ds(o, SUBLANES), :] * h + b_ref[:, pl.ds(o, SUBLANES), :]
        b_ref[:, pl.ds(o, SUBLANES), :] = hb
        return jnp.broadcast_to(hb[:, SUBLANES - 1:SUBLANES, :], hb.shape)

    ngroups = tt // SUBLANES
    h_ref[...] = lax.fori_loop(0, ngroups, carry_step, h_ref[...], unroll=min(ngroups, SUBLANES))
    for p in range(ncol):
        col = slice(p * LANES, (p + 1) * LANES)
        hs = b_ref[:, :, col].reshape(R, LANES)
        y_ref[:, col] = (hs * jax.nn.gelu(g_ref[:, col], approximate=True)).astype(BF16)

    tail = xe_ref[:, tt:tt + HC, :]
    xe_ref[:, 0:HC, :] = tail

    @pl.when(t == nt - 1)
    def _():
        nc_ref[...] = tail
        nh_ref[...] = h_ref[:, 0:1, :]


def _lru_call(z, conv_past, h0, conv_w, conv_b, wa, ba, wx, bx, lam, layer, xcol0, gcol0,
              *, nseq, seqlen, row0, bs, tt, cc):
    C = conv_w.shape[2]
    nt = seqlen // tt
    nc = C // cc
    rb0 = row0 // (bs * tt)
    xb0 = xcol0 // cc
    gb0 = gcol0 // cc
    nblk = cc // LRU_BLOCK_DIM
    has_past = conv_past is not None
    in_specs = [
        pl.BlockSpec((bs * tt, cc), lambda b, c, t: (rb0 + b * nt + t, xb0 + c)),
        pl.BlockSpec((bs * tt, cc), lambda b, c, t: (rb0 + b * nt + t, gb0 + c)),
    ]
    args = [z, z]
    if has_past:
        in_specs += [
            pl.BlockSpec((None, bs, CONV_HIST, cc), lambda b, c, t: (layer, b, 0, c)),
            pl.BlockSpec((None, bs, 1, cc), lambda b, c, t: (layer, b, 0, c)),
        ]
        args += [conv_past, h0]
    vec = pl.BlockSpec((None, 1, cc), lambda b, c, t: (layer, 0, c))
    gw = pl.BlockSpec((None, nblk, LRU_BLOCK_DIM, LRU_BLOCK_DIM), lambda b, c, t: (layer, c, 0, 0))
    in_specs += [pl.BlockSpec((None, CONV_WIDTH, cc), lambda b, c, t: (layer, 0, c)), vec, gw, vec, gw, vec, vec]
    args += [conv_w, conv_b, wa, ba, wx, bx, lam]
    return pl.pallas_call(
        functools.partial(_lru_kernel, bs=bs, tt=tt, nt=nt, has_past=has_past),
        grid=(nseq // bs, nc, nt),
        in_specs=in_specs,
        out_specs=[
            pl.BlockSpec((bs * tt, cc), lambda b, c, t: (b * nt + t, c)),
            pl.BlockSpec((bs, CONV_HIST, cc), lambda b, c, t: (b, 0, c)),
            pl.BlockSpec((bs, 1, cc), lambda b, c, t: (b, 0, c)),
        ],
        out_shape=[
            jax.ShapeDtypeStruct((nseq * seqlen, C), BF16),
            jax.ShapeDtypeStruct((nseq, CONV_HIST, C), F32),
            jax.ShapeDtypeStruct((nseq, 1, C), F32),
        ],
        scratch_shapes=[
            pltpu.VMEM((bs, CONV_HIST + tt, cc), F32),
            pltpu.VMEM((bs, SUBLANES, cc), F32),
            pltpu.VMEM((bs, tt, cc), F32),
            pltpu.VMEM((bs, tt, cc), F32),
            pltpu.VMEM((cc // LANES, LANES, 2 * LANES), BF16),
        ],
        compiler_params=_params(("arbitrary", "arbitrary", "arbitrary"), 48),
        name="lru",
    )(*args)


def _attn_stream_kernel(q_ref, kv_ref, qn_ref, kn_ref, cos_ref, s1_ref, s2_ref, sink_ref,
                        o_ref, nk_ref, nv_ref, kb_ref, vb_ref, *, qb, nb, pos0, layer):
    n = pl.program_id(1)
    KV = kb_ref.shape[-1]
    nkeys = kb_ref.shape[0]
    half = HEAD_DIM
    lane = lax.broadcasted_iota(jnp.int32, (1, LANES), 1)
    seg_ones = ((lax.broadcasted_iota(jnp.int32, (LANES, LANES), 0) >= half)
                == (lax.broadcasted_iota(jnp.int32, (LANES, LANES), 1) >= half)).astype(BF16)
    cosw = cos_ref[...]
    s1w = s1_ref[...]
    s2w = s2_ref[...]
    win = lambda w: slice(w * LANES, (w + 1) * LANES)

    def norm_rot(xw, gain):
        sq = xw * xw
        hi = sq.astype(BF16)
        lo = (sq - hi.astype(F32)).astype(BF16)
        ss = _bdot(hi, seg_ones) + _bdot(lo, seg_ones)
        y = (xw * lax.rsqrt(ss * (1.0 / HEAD_DIM) + RMS_EPS)) * gain
        return y * cosw + pltpu.roll(y, LANES - ROT_DIM // 2, 1) * s1w + pltpu.roll(y, ROT_DIM // 2, 1) * s2w

    qi = lax.broadcasted_iota(jnp.int32, (qb, nkeys), 0)
    si = lax.broadcasted_iota(jnp.int32, (qb, nkeys), 1)
    kpos0 = pos0 + n * qb - WINDOW
    valid = (si >= qi) & (si <= qi + WINDOW) & (si + kpos0 >= 0)

    @pl.when(n == 0)
    def _():
        kb_ref[0:WINDOW, :] = jnp.zeros((WINDOW, KV), F32)
        vb_ref[0:WINDOW, :] = jnp.zeros((WINDOW, KV), F32)

    kvx = kv_ref[...]
    for w in range(KV // LANES):
        kb_ref[WINDOW:, win(w)] = norm_rot(kvx[:, win(w)], kn_ref[...])
    vb_ref[WINDOW:, :] = kvx[:, KV:]
    qx = q_ref[...]
    for kv in range(N_KV_HEADS):
        w, off = divmod(kv, 2)
        in_head = (lane >= half) == (off == 1)

        def spread(ref):
            lo = jnp.where(in_head, ref[:, win(w)], 0.0)
            if off == 1:
                lo = pltpu.roll(lo, half, 1)
            return jnp.concatenate([lo, pltpu.roll(lo, half, 1)], axis=0).astype(BF16)

        kk = spread(kb_ref)
        vv = spread(vb_ref)
        qst = jnp.concatenate([norm_rot(qx[:, win(2 * kv + h)], qn_ref[...]) for h in range(2)],
                              axis=0).astype(BF16)
        sc = lax.dot_general(qst, kk, (((1,), (1,)), ((), ())),
                             preferred_element_type=F32) * (HEAD_DIM ** -0.5)
        prob_rows = []
        for rh in range(2):
            prob_cols = []
            for ch in range(2):
                sink = sink_ref[layer, kv * GQA_GROUP + 2 * rh + ch]
                sblk = jnp.where(valid, sc[rh * qb:(rh + 1) * qb, ch * nkeys:(ch + 1) * nkeys], NEG_INF)
                m = jnp.maximum(jnp.max(sblk, axis=-1, keepdims=True), sink)
                p = jnp.exp(sblk - m)
                den = jnp.sum(p, axis=-1, keepdims=True) + jnp.exp(sink - m)
                prob_cols.append(p * (1.0 / den))
            prob_rows.append(jnp.concatenate(prob_cols, axis=1))
        probs = jnp.concatenate(prob_rows, axis=0).astype(BF16)
        out = _bdot(probs, vv)
        for rh in range(2):
            o_ref[:, win(2 * kv + rh)] = out[rh * qb:(rh + 1) * qb].astype(BF16)
    new_k = kb_ref[WINDOW:, :]
    new_v = vb_ref[WINDOW:, :]
    kb_ref[0:WINDOW, :] = new_k
    vb_ref[0:WINDOW, :] = new_v

    @pl.when(n == nb - 1)
    def _():
        nk_ref[0] = new_k
        nv_ref[0] = new_v


def _attn_seq_kernel(q_ref, kv_ref, kp_ref, vp_ref, qn_ref, kn_ref, cos_ref, s1_ref, s2_ref, sink_ref,
                     o_ref, nk_ref, nv_ref, q_scr, k_scr, o_scr, *, bs, qb, pos0, layer):
    KV = k_scr.shape[-1]
    nkeys = 2 * WINDOW
    half = HEAD_DIM
    lane = lax.broadcasted_iota(jnp.int32, (1, LANES), 1)
    seg_ones = ((lax.broadcasted_iota(jnp.int32, (LANES, LANES), 0) >= half)
                == (lax.broadcasted_iota(jnp.int32, (LANES, LANES), 1) >= half)).astype(BF16)
    R = q_scr.shape[0]
    nq = q_scr.shape[1] // LANES
    nk = KV // LANES
    win = lambda w: slice(w * LANES, (w + 1) * LANES)

    xs = jnp.concatenate([q_ref[:, win(w)] for w in range(nq)] + [kv_ref[:, win(w)] for w in range(nk)], axis=0)
    sq = xs * xs
    hi = sq.astype(BF16)
    lo = (sq - hi.astype(F32)).astype(BF16)
    ss = _bdot(hi, seg_ones) + _bdot(lo, seg_ones)
    y = (xs * lax.rsqrt(ss * (1.0 / HEAD_DIM) + RMS_EPS)).reshape(nq + nk, R, LANES)
    gains = jnp.concatenate([jnp.broadcast_to(qn_ref[...] * (HEAD_DIM ** -0.5), (nq, 1, LANES)),
                             jnp.broadcast_to(kn_ref[...], (nk, 1, LANES))], axis=0)
    y = y * gains
    y = (y * cos_ref[...] + pltpu.roll(y, LANES - ROT_DIM // 2, 2) * s1_ref[...]
         + pltpu.roll(y, ROT_DIM // 2, 2) * s2_ref[...])
    for w in range(nq):
        q_scr[:, win(w)] = y[w]
    for w in range(nk):
        k_scr[:, win(w)] = y[nq + w]

    group = max(1, min(N_KV_HEADS, SOFTMAX_ROWS // (2 * qb)))
    nrow = 2 * group * qb
    qi = lax.broadcasted_iota(jnp.int32, (nrow, nkeys), 0) & (qb - 1)
    si = lax.broadcasted_iota(jnp.int32, (nrow, nkeys), 1)
    valid = (si >= qi) & (si <= qi + WINDOW) & (si + (pos0 - WINDOW) >= 0)
    sink_cols = [
        [jnp.concatenate([jnp.full((qb, 1), sink_ref[layer, kv * GQA_GROUP + 2 * rh + ch], F32)
                          for kv in range(g0, g0 + group) for rh in range(2)], axis=0)
         for ch in range(2)]
        for g0 in range(0, N_KV_HEADS, group)]

    def seq_step(s, c):
        rows = pl.ds(pl.multiple_of(s * qb, qb), qb)
        prev_k = kp_ref[s]
        prev_v = vp_ref[s]
        cur_k = k_scr[rows, :]
        cur_v = kv_ref[rows, KV:]
        pad = jnp.zeros((WINDOW - qb, KV), F32)
        k_all = jnp.concatenate([prev_k, cur_k, pad], axis=0)
        v_all = jnp.concatenate([prev_v, cur_v, pad], axis=0)

        def spread(x, kv):
            w, off = divmod(kv, 2)
            lo = jnp.where((lane >= half) == (off == 1), x[:, win(w)], 0.0)
            if off == 1:
                lo = pltpu.roll(lo, half, 1)
            return jnp.concatenate([lo, pltpu.roll(lo, half, 1)], axis=0).astype(BF16)

        for gi, g0 in enumerate(range(0, N_KV_HEADS, group)):
            scores = []
            for kv in range(g0, g0 + group):
                qst = jnp.concatenate([q_scr[rows, win(2 * kv + h)] for h in range(2)], axis=0).astype(BF16)
                scores.append(lax.dot_general(qst, spread(k_all, kv), (((1,), (1,)), ((), ())),
                                              preferred_element_type=F32))
            sc = jnp.concatenate(scores, axis=0)
            prob_cols = []
            for ch in range(2):
                sink = sink_cols[gi][ch]
                sblk = jnp.where(valid, sc[:, ch * nkeys:(ch + 1) * nkeys], NEG_INF)
                m = jnp.maximum(jnp.max(sblk, axis=-1, keepdims=True), sink)
                p = jnp.exp(sblk - m)
                den = jnp.sum(p, axis=-1, keepdims=True) + jnp.exp(sink - m)
                prob_cols.append(p * (1.0 / den))
            probs = jnp.concatenate(prob_cols, axis=1).astype(BF16)
            for j, kv in enumerate(range(g0, g0 + group)):
                out = _bdot(probs[2 * j * qb:2 * (j + 1) * qb], spread(v_all, kv))
                for rh in range(2):
                    o_scr[rows, win(2 * kv + rh)] = out[rh * qb:(rh + 1) * qb]
        nk_ref[s] = jnp.concatenate([prev_k[qb:], cur_k], axis=0)
        nv_ref[s] = jnp.concatenate([prev_v[qb:], cur_v], axis=0)
        return c

    lax.fori_loop(0, bs, seq_step, 0, unroll=min(bs, 4))
    o_ref[...] = o_scr[...].astype(BF16)


def _attn_call(z, k_past, v_past, qn, kn, cos_t, s1_t, s2_t, sinks, layer, qcol0, kvcol0,
               *, nseq, seqlen, row0, bs, qb, pos0):
    A = N_KV_HEADS * GQA_GROUP * HEAD_DIM
    KV = N_KV_HEADS * HEAD_DIM
    nb = seqlen // qb
    rb0 = row0 // (bs * qb)
    has_past = k_past is not None
    R = bs * qb
    if has_past:
        assert nb == 1 and qb < WINDOW
        body = functools.partial(_attn_seq_kernel, bs=bs, qb=qb, pos0=pos0, layer=layer)
        scratch = [pltpu.VMEM((R, A), F32), pltpu.VMEM((R, KV), F32), pltpu.VMEM((R, A), F32)]
    else:
        assert bs == 1 and qb == WINDOW
        body = functools.partial(_attn_stream_kernel, qb=qb, nb=nb, pos0=pos0, layer=layer)
        scratch = [pltpu.VMEM((2 * WINDOW, KV), F32), pltpu.VMEM((2 * WINDOW, KV), F32)]
    in_specs = [
        pl.BlockSpec((bs * qb, A), lambda b, n: (rb0 + b * nb + n, qcol0 // A)),
        pl.BlockSpec((bs * qb, 2 * KV), lambda b, n: (rb0 + b * nb + n, kvcol0 // (2 * KV))),
    ]
    args = [z, z]
    if has_past:
        past_spec = pl.BlockSpec((None, bs, WINDOW, KV), lambda b, n: (layer, b, 0, 0))
        in_specs += [past_spec, past_spec]
        args += [k_past, v_past]
    gain = pl.BlockSpec((None, 1, LANES), lambda b, n: (layer, 0, 0))
    tab = pl.BlockSpec((R, LANES), lambda b, n: (n, 0))
    in_specs += [gain, gain, tab, tab, tab, pl.BlockSpec(memory_space=pltpu.SMEM)]
    args += [qn, kn, cos_t, s1_t, s2_t, sinks]
    state = pl.BlockSpec((bs, WINDOW, KV), lambda b, n: (b, 0, 0))
    return pl.pallas_call(
        body,
        grid=(nseq // bs, nb),
        in_specs=in_specs,
        out_specs=[pl.BlockSpec((bs * qb, A), lambda b, n: (b * nb + n, 0)), state, state],
        out_shape=[
            jax.ShapeDtypeStruct((nseq * seqlen, A), BF16),
            jax.ShapeDtypeStruct((nseq, WINDOW, KV), F32),
            jax.ShapeDtypeStruct((nseq, WINDOW, KV), F32),
        ],
        scratch_shapes=scratch,
        compiler_params=_params(("arbitrary", "arbitrary"), 56),
        name="attn",
    )(*args)


def _rope_tables(pos, nrep):
    half = ROT_DIM // 2
    inv = ROPE_THETA ** (-jnp.arange(half, dtype=F32) / half)
    ang = pos.astype(F32)[:, None] * inv[None, :]
    cos, sin = jnp.cos(ang), jnp.sin(ang)
    T = pos.shape[0]
    ones = jnp.ones((T, HEAD_DIM - ROT_DIM), F32)
    zeros = jnp.zeros((T, HEAD_DIM - ROT_DIM), F32)
    zh = jnp.zeros((T, half), F32)
    cos_t = jnp.concatenate([cos, cos, ones], axis=1)
    s1_t = jnp.concatenate([-sin, zh, zeros], axis=1)
    s2_t = jnp.concatenate([zh, sin, zeros], axis=1)
    rep = LANES // HEAD_DIM
    return tuple(jnp.tile(t, (nrep, rep)) for t in (cos_t, s1_t, s2_t))


def kernel(x_prompt, x_sample, state_pool, cache_k_win, cache_v_win, state_conv, state_rglru, norm_ffa, ffa_w_gu, ffa_w_down, norm_mix, w_in, pool_w, pool_scale, q_norm, k_norm, attn_sinks, conv_w, conv_b, lru_gate_a_w, lru_gate_a_b, lru_gate_x_w, lru_gate_x_b, lru_lambda, w_branch_pool, w_branch_attn, w_branch_lru, w_out, norm_ffb, ffb_w_gu, ffb_w_down):
    Bp, Sp, D = x_prompt.shape
    Bs, Ss, _ = x_sample.shape
    L = norm_ffa.shape[0]
    Tp, Ts = Bp * Sp, Bs * Ss
    pool_c = pool_scale.shape[1]
    attn_c = N_KV_HEADS * GQA_GROUP * HEAD_DIM
    kv_c = N_KV_HEADS * HEAD_DIM
    lru_c = conv_w.shape[2]
    q0 = pool_c
    kv0 = q0 + attn_c
    xl0 = kv0 + 2 * kv_c
    gl0 = xl0 + lru_c
    gate0 = gl0 + lru_c

    tok = dict(ntiles=TOKEN_TILES)
    xp = x_prompt.reshape(Tp, D)
    xs = x_sample.reshape(Ts, D)

    vec3 = lambda a: a.reshape(L, 1, a.shape[-1])
    norm_ffa3, norm_mix3, norm_ffb3 = vec3(norm_ffa), vec3(norm_mix), vec3(norm_ffb)
    pool_scale3 = vec3(pool_scale)
    conv_b3, ba3, bx3, lam3 = vec3(conv_b), vec3(lru_gate_a_b), vec3(lru_gate_x_b), vec3(lru_lambda)
    qn3 = vec3(jnp.tile(q_norm, (1, LANES // HEAD_DIM)))
    kn3 = vec3(jnp.tile(k_norm, (1, LANES // HEAD_DIM)))
    pool_past = jnp.pad(state_pool, ((0, 0), (0, 0), (POOL_HIST - POOL_KEEP, 0), (0, 0)))
    conv_past = jnp.pad(state_conv, ((0, 0), (0, 0), (CONV_HIST - (CONV_WIDTH - 1), 0), (0, 0)))
    h0 = state_rglru.reshape(L, Bs, 1, lru_c)
    k_past = cache_k_win.reshape(L, Bs, WINDOW, kv_c)
    v_past = cache_v_win.reshape(L, Bs, WINDOW, kv_c)
    rope_p = _rope_tables(jnp.arange(Sp), 1)
    rope_s = _rope_tables(PAST_LEN + jnp.arange(Ss), Bs)

    prompt = dict(nseq=Bp, seqlen=Sp, row0=0)
    sample = dict(nseq=Bs, seqlen=Ss, row0=0)
    st_p = ([], [], [], [], [])
    st_s = ([], [], [], [], [])
    for l in range(L):
        xp, xs = _ffn_call(xp, xs, norm_ffa3, ffa_w_gu, ffa_w_down, l, tf=512, **tok)
        zp, zs, xn = _inproj_call(xp, xs, norm_mix3, w_in, l, gate0, tn=1536, **tok)

        pool_p, np_p = _pool_call(zp, None, pool_w, pool_scale3, l, bs=1, tt=512, pos0=0, **prompt)
        pool_s, np_s = _pool_call(zs, pool_past, pool_w, pool_scale3, l, bs=Bs, tt=Ss, pos0=PAST_LEN, **sample)

        att_p, nk_p, nv_p = _attn_call(zp, None, None, qn3, kn3, *rope_p, attn_sinks, l, q0, kv0,
                                       bs=1, qb=WINDOW, pos0=0, **prompt)
        att_s, nk_s, nv_s = _attn_call(zs, k_past, v_past, qn3, kn3, *rope_s, attn_sinks, l, q0, kv0,
                                       bs=Bs, qb=Ss, pos0=PAST_LEN, **sample)

        lru_args = (conv_w, conv_b3, lru_gate_a_w, ba3, lru_gate_x_w, bx3, lam3, l, xl0, gl0)
        lru_p, nc_p, nh_p = _lru_call(zp, None, None, *lru_args, bs=1, tt=512, cc=512, **prompt)
        lru_s, nc_s, nh_s = _lru_call(zs, conv_past, h0, *lru_args, bs=Bs, tt=Ss, cc=512, **sample)

        m = _merge_call(xn, w_in, gate0, ((pool_p, pool_s), (att_p, att_s), (lru_p, lru_s)),
                        (w_branch_pool, w_branch_attn, w_branch_lru), l, tc=256, **tok)
        xp, xs = _outproj_call(xp, xs, m, w_out, l, tn=512, **tok)
        xp, xs = _ffn_call(xp, xs, norm_ffb3, ffb_w_gu, ffb_w_down, l, tf=512, **tok)

        for lst, val in zip(st_p, (np_p, nk_p, nv_p, nc_p, nh_p)):
            lst.append(val)
        for lst, val in zip(st_s, (np_s, nk_s, nv_s, nc_s, nh_s)):
            lst.append(val)

    def states(st, nseq):
        pool = jnp.stack(st[0])[:, :, POOL_HIST - POOL_KEEP:, :]
        k = jnp.stack(st[1]).reshape(L, nseq, WINDOW, N_KV_HEADS, HEAD_DIM)
        v = jnp.stack(st[2]).reshape(L, nseq, WINDOW, N_KV_HEADS, HEAD_DIM)
        conv = jnp.stack(st[3])[:, :, CONV_HIST - (CONV_WIDTH - 1):, :]
        h = jnp.stack(st[4]).reshape(L, nseq, lru_c)
        return pool, k, v, conv, h

    pool_p, k_p, v_p, conv_p, h_p = states(st_p, Bp)
    pool_s, k_s, v_s, conv_s, h_s = states(st_s, Bs)
    y_p = xp.reshape(Bp, Sp, D)
    y_s = xs.reshape(Bs, Ss, D)
    return (y_p, y_s, pool_p, pool_s, k_p, k_s, v_p, v_s, conv_p, conv_s, h_p, h_s)
```

```python
import functools

import jax
import jax.numpy as jnp
from jax import lax
from jax.experimental import pallas as pl
from jax.experimental.pallas import tpu as pltpu

F32 = jnp.float32
BF16 = jnp.bfloat16

RMS_EPS = 1e-6
NEG_INF = -1e30
FFN_RES_WEIGHT = 0.5
POOL_WINDOWS = (2, 4, 8, 16)
POOL_KEEP = max(POOL_WINDOWS) - 1
POOL_HIST = 16
HEAD_DIM = 64
N_KV_HEADS = 4
GQA_GROUP = 4
WINDOW = 128
ROT_DIM = HEAD_DIM // 4
ROPE_THETA = 500000.0
CONV_WIDTH = 4
CONV_HIST = 8
LRU_C = 8.0
LRU_BLOCK_DIM = 64
PAST_LEN = 16384
LANES = 128
SUBLANES = 8
SOFTMAX_ROWS = 256
TOKEN_TILES = 8
MIB = 1024 * 1024


def _bdot(a, b):
    return jnp.dot(a, b, preferred_element_type=F32)


def _rms_bf16(x, g):
    ms = jnp.mean(x * x, axis=-1, keepdims=True)
    return ((x * lax.rsqrt(ms + RMS_EPS)) * g).astype(BF16)


def _sigmoid(x):
    return 0.5 * jnp.tanh(0.5 * x) + 0.5


def _params(sem, vmem_mib):
    return pltpu.CompilerParams(dimension_semantics=sem, vmem_limit_bytes=vmem_mib * MIB)


def _ffn_kernel(xp_ref, xs_ref, g_ref, wg_ref, wu_ref, wd_ref, op_ref, os_ref, xn_ref):
    tp = xp_ref.shape[0]

    @pl.when(pl.program_id(1) == 0)
    def _():
        xp = xp_ref[...]
        xs = xs_ref[...]
        xn_ref[0:tp, :] = _rms_bf16(xp, g_ref[...])
        xn_ref[tp:, :] = _rms_bf16(xs, g_ref[...])
        op_ref[...] = xp
        os_ref[...] = xs

    xn = xn_ref[...]
    g = _bdot(xn, wg_ref[...].astype(BF16))
    u = _bdot(xn, wu_ref[...].astype(BF16))
    h = (FFN_RES_WEIGHT * ((g * _sigmoid(g)) * u)).astype(BF16)
    res = _bdot(h, wd_ref[...].astype(BF16))
    op_ref[...] += res[:tp]
    os_ref[...] += res[tp:]


def _ffn_call(xp, xs, norm, w_gu, w_down, layer, *, ntiles, tf):
    Tp, D = xp.shape
    Ts = xs.shape[0]
    tp, ts = Tp // ntiles, Ts // ntiles
    dff = w_down.shape[1]
    nj = dff // tf
    rows = lambda t: pl.BlockSpec((t, D), lambda i, j: (i, 0))
    once = lambda t: pl.BlockSpec((t, D), lambda i, j: (i, 0), pipeline_mode=pl.Buffered(1))
    return pl.pallas_call(
        _ffn_kernel,
        grid=(ntiles, nj),
        in_specs=[
            once(tp), once(ts),
            pl.BlockSpec((None, 1, D), lambda i, j: (layer, 0, 0)),
            pl.BlockSpec((None, D, tf), lambda i, j: (layer, 0, j)),
            pl.BlockSpec((None, D, tf), lambda i, j: (layer, 0, j + nj)),
            pl.BlockSpec((None, tf, D), lambda i, j: (layer, j, 0)),
        ],
        out_specs=[rows(tp), rows(ts)],
        out_shape=[jax.ShapeDtypeStruct((Tp, D), F32), jax.ShapeDtypeStruct((Ts, D), F32)],
        scratch_shapes=[pltpu.VMEM((tp + ts, D), BF16)],
        compiler_params=_params(("arbitrary", "arbitrary"), 60),
        name="ffn",
    )(xp, xs, norm, w_gu, w_gu, w_down)


def _tile_rows(sub, tp, ts):
    tm = tp + ts
    return [((slice(s * tp, (s + 1) * tp), slice(s * tm, s * tm + tp)),
             (slice(s * ts, (s + 1) * ts), slice(s * tm + tp, (s + 1) * tm))) for s in range(sub)]


def _inproj_kernel(xp_ref, xs_ref, g_ref, w_ref, zp_ref, zs_ref, xn_ref, *, sub):
    tiles = _tile_rows(sub, xp_ref.shape[0] // sub, xs_ref.shape[0] // sub)

    @pl.when(pl.program_id(1) == 0)
    def _():
        for (p_rows, p_int), (s_rows, s_int) in tiles:
            xn_ref[p_int, :] = _rms_bf16(xp_ref[p_rows, :], g_ref[...])
            xn_ref[s_int, :] = _rms_bf16(xs_ref[s_rows, :], g_ref[...])

    res = _bdot(xn_ref[...], w_ref[...].astype(BF16))
    for (p_rows, p_int), (s_rows, s_int) in tiles:
        zp_ref[p_rows, :] = res[p_int]
        zs_ref[s_rows, :] = res[s_int]


def _inproj_call(xp, xs, norm, w_in, layer, ncols, *, ntiles, sub, tn):
    Tp, D = xp.shape
    Ts = xs.shape[0]
    ntiles = ntiles // sub
    tp, ts = Tp // ntiles, Ts // ntiles
    rows = lambda t: pl.BlockSpec((t, D), lambda i, j: (i, 0))
    once = lambda t: pl.BlockSpec((t, D), lambda i, j: (i, 0), pipeline_mode=pl.Buffered(1))
    cols = lambda t: pl.BlockSpec((t, tn), lambda i, j: (i, j))
    return pl.pallas_call(
        functools.partial(_inproj_kernel, sub=sub),
        grid=(ntiles, ncols // tn),
        in_specs=[
            once(tp), once(ts),
            pl.BlockSpec((None, 1, D), lambda i, j: (layer, 0, 0)),
            pl.BlockSpec((None, D, tn), lambda i, j: (layer, 0, j)),
        ],
        out_specs=[cols(tp), cols(ts), rows(tp + ts)],
        out_shape=[jax.ShapeDtypeStruct((Tp, ncols), F32), jax.ShapeDtypeStruct((Ts, ncols), F32),
                   jax.ShapeDtypeStruct((Tp + Ts, D), BF16)],
        compiler_params=_params(("arbitrary", "arbitrary"), 60),
        name="inproj",
    )(xp, xs, norm, w_in)


def _merge_kernel(xn_ref, wg0_ref, wg1_ref, wg2_ref, b0p_ref, b0s_ref, b1p_ref, b1s_ref, b2p_ref, b2s_ref,
                  w0_ref, w1_ref, w2_ref, o_ref, lhs_ref):
    tp = b0p_ref.shape[0]

    @pl.when(pl.program_id(1) == 0)
    def _():
        for b, (p_ref, s_ref) in enumerate(((b0p_ref, b0s_ref), (b1p_ref, b1s_ref), (b2p_ref, b2s_ref))):
            lhs_ref[b, 0:tp, :] = p_ref[...]
            lhs_ref[b, tp:, :] = s_ref[...]

    xn = xn_ref[...]

    def term(b, wg_ref, w_ref):
        gate = _sigmoid(_bdot(xn, wg_ref[...].astype(BF16)))
        return gate * _bdot(lhs_ref[b], w_ref[...].astype(BF16))

    m = term(0, wg0_ref, w0_ref) + term(1, wg1_ref, w1_ref) + term(2, wg2_ref, w2_ref)
    o_ref[...] = m.astype(BF16)


def _merge_call(xn, w_in, gate_col0, branches, branch_ws, layer, *, ntiles, tc):
    T, D = xn.shape
    tm = T // ntiles
    W = branches[0][0].shape[1]
    tp, ts = branches[0][0].shape[0] // ntiles, branches[0][1].shape[0] // ntiles
    g0 = gate_col0 // tc
    gstep = D // tc
    gate_w = lambda b: pl.BlockSpec((None, D, tc), lambda i, c: (layer, 0, g0 + b * gstep + c))
    rows = lambda t: pl.BlockSpec((t, W), lambda i, c: (i, 0))
    w_spec = pl.BlockSpec((None, W, tc), lambda i, c: (layer, 0, c))
    return pl.pallas_call(
        _merge_kernel,
        grid=(ntiles, D // tc),
        in_specs=[pl.BlockSpec((tm, D), lambda i, c: (i, 0)), gate_w(0), gate_w(1), gate_w(2)]
                 + [rows(tp), rows(ts)] * 3 + [w_spec] * 3,
        out_specs=pl.BlockSpec((tm, tc), lambda i, c: (i, c)),
        out_shape=jax.ShapeDtypeStruct((T, D), BF16),
        scratch_shapes=[pltpu.VMEM((3, tm, W), BF16)],
        compiler_params=_params(("arbitrary", "arbitrary"), 58),
        name="merge",
    )(xn, w_in, w_in, w_in, *[a for pair in branches for a in pair], *branch_ws)


def _outproj_kernel(xp_ref, xs_ref, m_ref, w_ref, op_ref, os_ref, *, sub):
    res = _bdot(m_ref[...], w_ref[...].astype(BF16))
    for (p_rows, p_int), (s_rows, s_int) in _tile_rows(sub, xp_ref.shape[0] // sub, xs_ref.shape[0] // sub):
        op_ref[p_rows, :] = xp_ref[p_rows, :] + res[p_int]
        os_ref[s_rows, :] = xs_ref[s_rows, :] + res[s_int]


def _outproj_call(xp, xs, m, w_out, layer, *, ntiles, sub, tn):
    Tp, D = xp.shape
    Ts = xs.shape[0]
    ntiles = ntiles // sub
    tp, ts = Tp // ntiles, Ts // ntiles
    cols = lambda t: pl.BlockSpec((t, tn), lambda i, c: (i, c))
    return pl.pallas_call(
        functools.partial(_outproj_kernel, sub=sub),
        grid=(ntiles, D // tn),
        in_specs=[
            cols(tp), cols(ts),
            pl.BlockSpec((tp + ts, D), lambda i, c: (i, 0)),
            pl.BlockSpec((None, D, tn), lambda i, c: (layer, 0, c)),
        ],
        out_specs=[cols(tp), cols(ts)],
        out_shape=[jax.ShapeDtypeStruct((Tp, D), F32), jax.ShapeDtypeStruct((Ts, D), F32)],
        compiler_params=_params(("arbitrary", "arbitrary"), 56),
        name="outproj",
    )(xp, xs, m, w_out)


def _pool_kernel(*refs, bs, tt, nt, pos0, has_past):
    if has_past:
        u_ref, past_ref, w_ref, s_ref, o_ref, np_ref, e_ref = refs
    else:
        u_ref, w_ref, s_ref, o_ref, np_ref, e_ref = refs
    t = pl.program_id(1)
    C = e_ref.shape[-1]
    H = POOL_HIST

    @pl.when(t == 0)
    def _():
        if has_past:
            e_ref[:, 0:H, :] = past_ref[...]
        else:
            e_ref[:, 0:H, :] = jnp.zeros((bs, H, C), F32)

    e_ref[:, H:H + tt, :] = u_ref[...].reshape(bs, tt, C)
    posp1 = lax.broadcasted_iota(jnp.int32, (1, tt, 1), 1) + (t * tt + pos0 + 1)
    gd = C // len(POOL_WINDOWS)
    for g, w in enumerate(POOL_WINDOWS):
        sl = slice(g * gd, (g + 1) * gd)
        e = e_ref[:, :, sl]
        p = e
        s = 1
        while s < w:
            p = p + pltpu.roll(p, s, 1)
            s *= 2
        cnt = jnp.minimum(posp1, w).astype(F32)
        d = (p[:, H:, :] / cnt - e[:, H:, :]).reshape(bs * tt, gd).astype(BF16)
        out = _bdot(d, w_ref[g].astype(BF16)) * s_ref[:, sl]
        o_ref[:, sl] = out.astype(BF16)

    carry = e_ref[:, tt:tt + H, :]
    e_ref[:, 0:H, :] = carry

    @pl.when(t == nt - 1)
    def _():
        np_ref[...] = carry


def _pool_call(z, past, pool_w, pool_scale, layer, *, nseq, seqlen, row0, bs, tt, pos0):
    C = pool_w.shape[1] * pool_w.shape[2]
    nt = seqlen // tt
    rb0 = row0 // (bs * tt)
    has_past = past is not None
    in_specs = [pl.BlockSpec((bs * tt, C), lambda b, t: (rb0 + b * nt + t, 0))]
    args = [z]
    if has_past:
        in_specs.append(pl.BlockSpec((None, bs, POOL_HIST, C), lambda b, t: (layer, b, 0, 0)))
        args.append(past)
    in_specs += [
        pl.BlockSpec((None,) + pool_w.shape[1:], lambda b, t: (layer, 0, 0, 0)),
        pl.BlockSpec((None, 1, C), lambda b, t: (layer, 0, 0)),
    ]
    args += [pool_w, pool_scale]
    return pl.pallas_call(
        functools.partial(_pool_kernel, bs=bs, tt=tt, nt=nt, pos0=pos0, has_past=has_past),
        grid=(nseq // bs, nt),
        in_specs=in_specs,
        out_specs=[
            pl.BlockSpec((bs * tt, C), lambda b, t: (b * nt + t, 0)),
            pl.BlockSpec((bs, POOL_HIST, C), lambda b, t: (b, 0, 0)),
        ],
        out_shape=[
            jax.ShapeDtypeStruct((nseq * seqlen, C), BF16),
            jax.ShapeDtypeStruct((nseq, POOL_HIST, C), F32),
        ],
        scratch_shapes=[pltpu.VMEM((bs, POOL_HIST + tt, C), F32)],
        compiler_params=_params(("arbitrary", "arbitrary"), 48),
        name="pool",
    )(*args)


def _lru_kernel(*refs, bs, tt, nt, has_past):
    if has_past:
        (x_ref, g_ref, cp_ref, h0_ref, cw_ref, cb_ref, wa_ref, ba_ref, wx_ref, bx_ref, lam_ref,
         y_ref, nc_ref, nh_ref, xe_ref, h_ref, a_ref, b_ref, bd_ref) = refs
    else:
        (x_ref, g_ref, cw_ref, cb_ref, wa_ref, ba_ref, wx_ref, bx_ref, lam_ref,
         y_ref, nc_ref, nh_ref, xe_ref, h_ref, a_ref, b_ref, bd_ref) = refs
    t = pl.program_id(2)
    Cc = xe_ref.shape[-1]
    R = bs * tt
    ncol = Cc // LANES
    HC = CONV_HIST
    blk = LRU_BLOCK_DIM

    @pl.when(t == 0)
    def _():
        if has_past:
            xe_ref[:, 0:HC, :] = cp_ref[...]
            h_ref[...] = jnp.broadcast_to(h0_ref[...], (bs, SUBLANES, Cc))
        else:
            xe_ref[:, 0:HC, :] = jnp.zeros((bs, HC, Cc), F32)
            h_ref[...] = jnp.zeros((bs, SUBLANES, Cc), F32)
        rep = (lax.broadcasted_iota(jnp.int32, (blk, LANES), 0)
               == (lax.broadcasted_iota(jnp.int32, (blk, LANES), 1) & (blk - 1))).astype(BF16)
        diag = ((lax.broadcasted_iota(jnp.int32, (LANES, LANES), 0) >= blk)
                == (lax.broadcasted_iota(jnp.int32, (LANES, LANES), 1) >= blk))
        for p in range(ncol):
            for k, w_ref in enumerate((wa_ref, wx_ref)):
                w2 = w_ref[2 * p:2 * p + 2].reshape(2 * blk, blk).astype(BF16)
                full = _bdot(w2, rep)
                bd_ref[p, :, k * LANES:(k + 1) * LANES] = jnp.where(diag, full, 0.0).astype(BF16)

    xe_ref[:, HC:HC + tt, :] = x_ref[...].reshape(bs, tt, Cc)
    xe = xe_ref[...]
    cw = cw_ref[...]
    xc = cb_ref[...] + pltpu.roll(xe, 3, 1)[:, HC:, :] * cw[0:1]
    xc = xc + pltpu.roll(xe, 2, 1)[:, HC:, :] * cw[1:2]
    xc = xc + pltpu.roll(xe, 1, 1)[:, HC:, :] * cw[2:3]
    xc = xc + xe[:, HC:, :] * cw[3:4]
    xc = xc.reshape(R, Cc)

    row8 = lax.broadcasted_iota(jnp.int32, (1, SUBLANES, 1), 1)
    for p in range(ncol):
        col = slice(p * LANES, (p + 1) * LANES)
        xcp = xc[:, col]
        pre = _bdot(xcp.astype(BF16), bd_ref[p])
        r = _sigmoid(pre[:, :LANES] + ba_ref[:, col])
        i = _sigmoid(pre[:, LANES:] + bx_ref[:, col])
        nl = -lam_ref[:, col]
        sp = jnp.maximum(nl, 0.0) + jnp.log1p(jnp.exp(-jnp.abs(nl)))
        la = (-LRU_C * r) * sp
        a = jnp.exp(la)
        bv = jnp.sqrt(jnp.tanh(-la) * (a * a + 1.0)) * (i * xcp)
        a = a.reshape(R // SUBLANES, SUBLANES, LANES)
        bv = bv.reshape(R // SUBLANES, SUBLANES, LANES)
        for s in (1, 2, 4):
            keep = row8 >= s
            a_sh = pltpu.roll(a, s, 1)
            b_sh = pltpu.roll(bv, s, 1)
            bv = jnp.where(keep, a * b_sh + bv, bv)
            a = jnp.where(keep, a * a_sh, a)
        a_ref[:, :, col] = a.reshape(bs, tt, LANES)
        b_ref[:, :, col] = bv.reshape(bs, tt, LANES)

    def carry_step(k, h):
        o = pl.multiple_of(k * SUBLANES, SUBLANES)
        hb = a_ref[:, pl.ds(o, SUBLANES), :] * h + b_ref[:, pl.ds(o, SUBLANES), :]
        b_ref[:, pl.ds(o, SUBLANES), :] = hb
        return jnp.broadcast_to(hb[:, SUBLANES - 1:SUBLANES, :], hb.shape)

    ngroups = tt // SUBLANES
    h_ref[...] = lax.fori_loop(0, ngroups, carry_step, h_ref[...], unroll=min(ngroups, SUBLANES))
    for p in range(ncol):
        col = slice(p * LANES, (p + 1) * LANES)
        hs = b_ref[:, :, col].reshape(R, LANES)
        y_ref[:, col] = (hs * jax.nn.gelu(g_ref[:, col], approximate=True)).astype(BF16)

    tail = xe_ref[:, tt:tt + HC, :]
    xe_ref[:, 0:HC, :] = tail

    @pl.when(t == nt - 1)
    def _():
        nc_ref[...] = tail
        nh_ref[...] = h_ref[:, 0:1, :]


def _lru_call(z, conv_past, h0, conv_w, conv_b, wa, ba, wx, bx, lam, layer, xcol0, gcol0,
              *, nseq, seqlen, row0, bs, tt, cc):
    C = conv_w.shape[2]
    nt = seqlen // tt
    nc = C // cc
    rb0 = row0 // (bs * tt)
    xb0 = xcol0 // cc
    gb0 = gcol0 // cc
    nblk = cc // LRU_BLOCK_DIM
    has_past = conv_past is not None
    in_specs = [
        pl.BlockSpec((bs * tt, cc), lambda b, c, t: (rb0 + b * nt + t, xb0 + c)),
        pl.BlockSpec((bs * tt, cc), lambda b, c, t: (rb0 + b * nt + t, gb0 + c)),
    ]
    args = [z, z]
    if has_past:
        in_specs += [
            pl.BlockSpec((None, bs, CONV_HIST, cc), lambda b, c, t: (layer, b, 0, c)),
            pl.BlockSpec((None, bs, 1, cc), lambda b, c, t: (layer, b, 0, c)),
        ]
        args += [conv_past, h0]
    vec = pl.BlockSpec((None, 1, cc), lambda b, c, t: (layer, 0, c))
    gw = pl.BlockSpec((None, nblk, LRU_BLOCK_DIM, LRU_BLOCK_DIM), lambda b, c, t: (layer, c, 0, 0))
    in_specs += [pl.BlockSpec((None, CONV_WIDTH, cc), lambda b, c, t: (layer, 0, c)), vec, gw, vec, gw, vec, vec]
    args += [conv_w, conv_b, wa, ba, wx, bx, lam]
    return pl.pallas_call(
        functools.partial(_lru_kernel, bs=bs, tt=tt, nt=nt, has_past=has_past),
        grid=(nseq // bs, nc, nt),
        in_specs=in_specs,
        out_specs=[
            pl.BlockSpec((bs * tt, cc), lambda b, c, t: (b * nt + t, c)),
            pl.BlockSpec((bs, CONV_HIST, cc), lambda b, c, t: (b, 0, c)),
            pl.BlockSpec((bs, 1, cc), lambda b, c, t: (b, 0, c)),
        ],
        out_shape=[
            jax.ShapeDtypeStruct((nseq * seqlen, C), BF16),
            jax.ShapeDtypeStruct((nseq, CONV_HIST, C), F32),
            jax.ShapeDtypeStruct((nseq, 1, C), F32),
        ],
        scratch_shapes=[
            pltpu.VMEM((bs, CONV_HIST + tt, cc), F32),
            pltpu.VMEM((bs, SUBLANES, cc), F32),
            pltpu.VMEM((bs, tt, cc), F32),
            pltpu.VMEM((bs, tt, cc), F32),
            pltpu.VMEM((cc // LANES, LANES, 2 * LANES), BF16),
        ],
        compiler_params=_params(("arbitrary", "arbitrary", "arbitrary"), 48),
        name="lru",
    )(*args)


def _attn_stream_kernel(q_ref, kv_ref, qn_ref, kn_ref, cos_ref, s1_ref, s2_ref, sink_ref,
                        o_ref, nk_ref, nv_ref, kb_ref, vb_ref, *, qb, nb, pos0, layer):
    n = pl.program_id(1)
    KV = kb_ref.shape[-1]
    nkeys = kb_ref.shape[0]
    half = HEAD_DIM
    lane = lax.broadcasted_iota(jnp.int32, (1, LANES), 1)
    seg_ones = ((lax.broadcasted_iota(jnp.int32, (LANES, LANES), 0) >= half)
                == (lax.broadcasted_iota(jnp.int32, (LANES, LANES), 1) >= half)).astype(BF16)
    cosw = cos_ref[...]
    s1w = s1_ref[...]
    s2w = s2_ref[...]
    win = lambda w: slice(w * LANES, (w + 1) * LANES)

    def norm_rot(xw, gain):
        sq = xw * xw
        hi = sq.astype(BF16)
        lo = (sq - hi.astype(F32)).astype(BF16)
        ss = _bdot(hi, seg_ones) + _bdot(lo, seg_ones)
        y = (xw * lax.rsqrt(ss * (1.0 / HEAD_DIM) + RMS_EPS)) * gain
        return y * cosw + pltpu.roll(y, LANES - ROT_DIM // 2, 1) * s1w + pltpu.roll(y, ROT_DIM // 2, 1) * s2w

    qi = lax.broadcasted_iota(jnp.int32, (qb, nkeys), 0)
    si = lax.broadcasted_iota(jnp.int32, (qb, nkeys), 1)
    kpos0 = pos0 + n * qb - WINDOW
    valid = (si >= qi) & (si <= qi + WINDOW) & (si + kpos0 >= 0)

    @pl.when(n == 0)
    def _():
        kb_ref[0:WINDOW, :] = jnp.zeros((WINDOW, KV), F32)
        vb_ref[0:WINDOW, :] = jnp.zeros((WINDOW, KV), F32)

    kvx = kv_ref[...]
    for w in range(KV // LANES):
        kb_ref[WINDOW:, win(w)] = norm_rot(kvx[:, win(w)], kn_ref[...])
    vb_ref[WINDOW:, :] = kvx[:, KV:]
    qx = q_ref[...]
    for kv in range(N_KV_HEADS):
        w, off = divmod(kv, 2)
        in_head = (lane >= half) == (off == 1)

        def spread(ref):
            lo = jnp.where(in_head, ref[:, win(w)], 0.0)
            if off == 1:
                lo = pltpu.roll(lo, half, 1)
            return jnp.concatenate([lo, pltpu.roll(lo, half, 1)], axis=0).astype(BF16)

        kk = spread(kb_ref)
        vv = spread(vb_ref)
        qst = jnp.concatenate([norm_rot(qx[:, win(2 * kv + h)], qn_ref[...]) for h in range(2)],
                              axis=0).astype(BF16)
        sc = lax.dot_general(qst, kk, (((1,), (1,)), ((), ())),
                             preferred_element_type=F32) * (HEAD_DIM ** -0.5)
        prob_rows = []
        for rh in range(2):
            prob_cols = []
            for ch in range(2):
                sink = sink_ref[layer, kv * GQA_GROUP + 2 * rh + ch]
                sblk = jnp.where(valid, sc[rh * qb:(rh + 1) * qb, ch * nkeys:(ch + 1) * nkeys], NEG_INF)
                m = jnp.maximum(jnp.max(sblk, axis=-1, keepdims=True), sink)
                p = jnp.exp(sblk - m)
                den = jnp.sum(p, axis=-1, keepdims=True) + jnp.exp(sink - m)
                prob_cols.append(p * (1.0 / den))
            prob_rows.append(jnp.concatenate(prob_cols, axis=1))
        probs = jnp.concatenate(prob_rows, axis=0).astype(BF16)
        out = _bdot(probs, vv)
        for rh in range(2):
            o_ref[:, win(2 * kv + rh)] = out[rh * qb:(rh + 1) * qb].astype(BF16)
    new_k = kb_ref[WINDOW:, :]
    new_v = vb_ref[WINDOW:, :]
    kb_ref[0:WINDOW, :] = new_k
    vb_ref[0:WINDOW, :] = new_v

    @pl.when(n == nb - 1)
    def _():
        nk_ref[0] = new_k
        nv_ref[0] = new_v


def _attn_seq_kernel(q_ref, kv_ref, kp_ref, vp_ref, qn_ref, kn_ref, cos_ref, s1_ref, s2_ref, sink_ref,
                     o_ref, nk_ref, nv_ref, q_scr, k_scr, o_scr, *, bs, qb, pos0, layer):
    KV = k_scr.shape[-1]
    nkeys = 2 * WINDOW
    half = HEAD_DIM
    lane = lax.broadcasted_iota(jnp.int32, (1, LANES), 1)
    seg_ones = ((lax.broadcasted_iota(jnp.int32, (LANES, LANES), 0) >= half)
                == (lax.broadcasted_iota(jnp.int32, (LANES, LANES), 1) >= half)).astype(BF16)
    R = q_scr.shape[0]
    nq = q_scr.shape[1] // LANES
    nk = KV // LANES
    win = lambda w: slice(w * LANES, (w + 1) * LANES)

    xs = jnp.concatenate([q_ref[:, win(w)] for w in range(nq)] + [kv_ref[:, win(w)] for w in range(nk)], axis=0)
    sq = xs * xs
    hi = sq.astype(BF16)
    lo = (sq - hi.astype(F32)).astype(BF16)
    ss = _bdot(hi, seg_ones) + _bdot(lo, seg_ones)
    y = (xs * lax.rsqrt(ss * (1.0 / HEAD_DIM) + RMS_EPS)).reshape(nq + nk, R, LANES)
    gains = jnp.concatenate([jnp.broadcast_to(qn_ref[...] * (HEAD_DIM ** -0.5), (nq, 1, LANES)),
                             jnp.broadcast_to(kn_ref[...], (nk, 1, LANES))], axis=0)
    y = y * gains
    y = (y * cos_ref[...] + pltpu.roll(y, LANES - ROT_DIM // 2, 2) * s1_ref[...]
         + pltpu.roll(y, ROT_DIM // 2, 2) * s2_ref[...])
    for w in range(nq):
        q_scr[:, win(w)] = y[w]
    for w in range(nk):
        k_scr[:, win(w)] = y[nq + w]

    group = max(1, min(N_KV_HEADS, SOFTMAX_ROWS // (2 * qb)))
    nrow = 2 * group * qb
    qi = lax.broadcasted_iota(jnp.int32, (nrow, nkeys), 0) & (qb - 1)
    si = lax.broadcasted_iota(jnp.int32, (nrow, nkeys), 1)
    valid = (si >= qi) & (si <= qi + WINDOW) & (si + (pos0 - WINDOW) >= 0)
    sink_cols = [
        [jnp.concatenate([jnp.full((qb, 1), sink_ref[layer, kv * GQA_GROUP + 2 * rh + ch], F32)
                          for kv in range(g0, g0 + group) for rh in range(2)], axis=0)
         for ch in range(2)]
        for g0 in range(0, N_KV_HEADS, group)]

    def seq_step(s, c):
        rows = pl.ds(pl.multiple_of(s * qb, qb), qb)
        prev_k = kp_ref[s]
        prev_v = vp_ref[s]
        cur_k = k_scr[rows, :]
        cur_v = kv_ref[rows, KV:]
        pad = jnp.zeros((WINDOW - qb, KV), F32)
        k_all = jnp.concatenate([prev_k, cur_k, pad], axis=0)
        v_all = jnp.concatenate([prev_v, cur_v, pad], axis=0)

        def spread(x, kv):
            w, off = divmod(kv, 2)
            lo = jnp.where((lane >= half) == (off == 1), x[:, win(w)], 0.0)
            if off == 1:
                lo = pltpu.roll(lo, half, 1)
            return jnp.concatenate([lo, pltpu.roll(lo, half, 1)], axis=0).astype(BF16)

        for gi, g0 in enumerate(range(0, N_KV_HEADS, group)):
            scores = []
            for kv in range(g0, g0 + group):
                qst = jnp.concatenate([q_scr[rows, win(2 * kv + h)] for h in range(2)], axis=0).astype(BF16)
                scores.append(lax.dot_general(qst, spread(k_all, kv), (((1,), (1,)), ((), ())),
                                              preferred_element_type=F32))
            sc = jnp.concatenate(scores, axis=0)
            prob_cols = []
            for ch in range(2):
                sink = sink_cols[gi][ch]
                sblk = jnp.where(valid, sc[:, ch * nkeys:(ch + 1) * nkeys], NEG_INF)
                m = jnp.maximum(jnp.max(sblk, axis=-1, keepdims=True), sink)
                p = jnp.exp(sblk - m)
                den = jnp.sum(p, axis=-1, keepdims=True) + jnp.exp(sink - m)
                prob_cols.append(p * (1.0 / den))
            probs = jnp.concatenate(prob_cols, axis=1).astype(BF16)
            for j, kv in enumerate(range(g0, g0 + group)):
                out = _bdot(probs[2 * j * qb:2 * (j + 1) * qb], spread(v_all, kv))
                for rh in range(2):
                    o_scr[rows, win(2 * kv + rh)] = out[rh * qb:(rh + 1) * qb]
        nk_ref[s] = jnp.concatenate([prev_k[qb:], cur_k], axis=0)
        nv_ref[s] = jnp.concatenate([prev_v[qb:], cur_v], axis=0)
        return c

    lax.fori_loop(0, bs, seq_step, 0, unroll=min(bs, 4))
    o_ref[...] = o_scr[...].astype(BF16)


def _attn_call(z, k_past, v_past, qn, kn, cos_t, s1_t, s2_t, sinks, layer, qcol0, kvcol0,
               *, nseq, seqlen, row0, bs, qb, pos0):
    A = N_KV_HEADS * GQA_GROUP * HEAD_DIM
    KV = N_KV_HEADS * HEAD_DIM
    nb = seqlen // qb
    rb0 = row0 // (bs * qb)
    has_past = k_past is not None
    R = bs * qb
    if has_past:
        assert nb == 1 and qb < WINDOW
        body = functools.partial(_attn_seq_kernel, bs=bs, qb=qb, pos0=pos0, layer=layer)
        scratch = [pltpu.VMEM((R, A), F32), pltpu.VMEM((R, KV), F32), pltpu.VMEM((R, A), F32)]
    else:
        assert bs == 1 and qb == WINDOW
        body = functools.partial(_attn_stream_kernel, qb=qb, nb=nb, pos0=pos0, layer=layer)
        scratch = [pltpu.VMEM((2 * WINDOW, KV), F32), pltpu.VMEM((2 * WINDOW, KV), F32)]
    in_specs = [
        pl.BlockSpec((bs * qb, A), lambda b, n: (rb0 + b * nb + n, qcol0 // A)),
        pl.BlockSpec((bs * qb, 2 * KV), lambda b, n: (rb0 + b * nb + n, kvcol0 // (2 * KV))),
    ]
    args = [z, z]
    if has_past:
        past_spec = pl.BlockSpec((None, bs, WINDOW, KV), lambda b, n: (layer, b, 0, 0))
        in_specs += [past_spec, past_spec]
        args += [k_past, v_past]
    gain = pl.BlockSpec((None, 1, LANES), lambda b, n: (layer, 0, 0))
    tab = pl.BlockSpec((R, LANES), lambda b, n: (n, 0))
    in_specs += [gain, gain, tab, tab, tab, pl.BlockSpec(memory_space=pltpu.SMEM)]
    args += [qn, kn, cos_t, s1_t, s2_t, sinks]
    state = pl.BlockSpec((bs, WINDOW, KV), lambda b, n: (b, 0, 0))
    return pl.pallas_call(
        body,
        grid=(nseq // bs, nb),
        in_specs=in_specs,
        out_specs=[pl.BlockSpec((bs * qb, A), lambda b, n: (b * nb + n, 0)), state, state],
        out_shape=[
            jax.ShapeDtypeStruct((nseq * seqlen, A), BF16),
            jax.ShapeDtypeStruct((nseq, WINDOW, KV), F32),
            jax.ShapeDtypeStruct((nseq, WINDOW, KV), F32),
        ],
        scratch_shapes=scratch,
        compiler_params=_params(("arbitrary", "arbitrary"), 56),
        name="attn",
    )(*args)


def _rope_tables(pos, nrep):
    half = ROT_DIM // 2
    inv = ROPE_THETA ** (-jnp.arange(half, dtype=F32) / half)
    ang = pos.astype(F32)[:, None] * inv[None, :]
    cos, sin = jnp.cos(ang), jnp.sin(ang)
    T = pos.shape[0]
    ones = jnp.ones((T, HEAD_DIM - ROT_DIM), F32)
    zeros = jnp.zeros((T, HEAD_DIM - ROT_DIM), F32)
    zh = jnp.zeros((T, half), F32)
    cos_t = jnp.concatenate([cos, cos, ones], axis=1)
    s1_t = jnp.concatenate([-sin, zh, zeros], axis=1)
    s2_t = jnp.concatenate([zh, sin, zeros], axis=1)
    rep = LANES // HEAD_DIM
    return tuple(jnp.tile(t, (nrep, rep)) for t in (cos_t, s1_t, s2_t))


def kernel(x_prompt, x_sample, state_pool, cache_k_win, cache_v_win, state_conv, state_rglru, norm_ffa, ffa_w_gu, ffa_w_down, norm_mix, w_in, pool_w, pool_scale, q_norm, k_norm, attn_sinks, conv_w, conv_b, lru_gate_a_w, lru_gate_a_b, lru_gate_x_w, lru_gate_x_b, lru_lambda, w_branch_pool, w_branch_attn, w_branch_lru, w_out, norm_ffb, ffb_w_gu, ffb_w_down):
    Bp, Sp, D = x_prompt.shape
    Bs, Ss, _ = x_sample.shape
    L = norm_ffa.shape[0]
    Tp, Ts = Bp * Sp, Bs * Ss
    pool_c = pool_scale.shape[1]
    attn_c = N_KV_HEADS * GQA_GROUP * HEAD_DIM
    kv_c = N_KV_HEADS * HEAD_DIM
    lru_c = conv_w.shape[2]
    q0 = pool_c
    kv0 = q0 + attn_c
    xl0 = kv0 + 2 * kv_c
    gl0 = xl0 + lru_c
    gate0 = gl0 + lru_c

    tok = dict(ntiles=TOKEN_TILES)
    xp = x_prompt.reshape(Tp, D)
    xs = x_sample.reshape(Ts, D)

    vec3 = lambda a: a.reshape(L, 1, a.shape[-1])
    norm_ffa3, norm_mix3, norm_ffb3 = vec3(norm_ffa), vec3(norm_mix), vec3(norm_ffb)
    pool_scale3 = vec3(pool_scale)
    conv_b3, ba3, bx3, lam3 = vec3(conv_b), vec3(lru_gate_a_b), vec3(lru_gate_x_b), vec3(lru_lambda)
    qn3 = vec3(jnp.tile(q_norm, (1, LANES // HEAD_DIM)))
    kn3 = vec3(jnp.tile(k_norm, (1, LANES // HEAD_DIM)))
    pool_past = jnp.pad(state_pool, ((0, 0), (0, 0), (POOL_HIST - POOL_KEEP, 0), (0, 0)))
    conv_past = jnp.pad(state_conv, ((0, 0), (0, 0), (CONV_HIST - (CONV_WIDTH - 1), 0), (0, 0)))
    h0 = state_rglru.reshape(L, Bs, 1, lru_c)
    k_past = cache_k_win.reshape(L, Bs, WINDOW, kv_c)
    v_past = cache_v_win.reshape(L, Bs, WINDOW, kv_c)
    rope_p = _rope_tables(jnp.arange(Sp), 1)
    rope_s = _rope_tables(PAST_LEN + jnp.arange(Ss), Bs)

    prompt = dict(nseq=Bp, seqlen=Sp, row0=0)
    sample = dict(nseq=Bs, seqlen=Ss, row0=0)
    st_p = ([], [], [], [], [])
    st_s = ([], [], [], [], [])
    for l in range(L):
        xp, xs = _ffn_call(xp, xs, norm_ffa3, ffa_w_gu, ffa_w_down, l, tf=512, **tok)
        zp, zs, xn = _inproj_call(xp, xs, norm_mix3, w_in, l, gate0, sub=2, tn=512, **tok)

        pool_p, np_p = _pool_call(zp, None, pool_w, pool_scale3, l, bs=1, tt=512, pos0=0, **prompt)
        pool_s, np_s = _pool_call(zs, pool_past, pool_w, pool_scale3, l, bs=Bs, tt=Ss, pos0=PAST_LEN, **sample)

        att_p, nk_p, nv_p = _attn_call(zp, None, None, qn3, kn3, *rope_p, attn_sinks, l, q0, kv0,
                                       bs=1, qb=WINDOW, pos0=0, **prompt)
        att_s, nk_s, nv_s = _attn_call(zs, k_past, v_past, qn3, kn3, *rope_s, attn_sinks, l, q0, kv0,
                                       bs=Bs, qb=Ss, pos0=PAST_LEN, **sample)

        lru_args = (conv_w, conv_b3, lru_gate_a_w, ba3, lru_gate_x_w, bx3, lam3, l, xl0, gl0)
        lru_p, nc_p, nh_p = _lru_call(zp, None, None, *lru_args, bs=1, tt=512, cc=512, **prompt)
        lru_s, nc_s, nh_s = _lru_call(zs, conv_past, h0, *lru_args, bs=Bs, tt=Ss, cc=512, **sample)

        m = _merge_call(xn, w_in, gate0, ((pool_p, pool_s), (att_p, att_s), (lru_p, lru_s)),
                        (w_branch_pool, w_branch_attn, w_branch_lru), l, tc=256, **tok)
        xp, xs = _outproj_call(xp, xs, m, w_out, l, sub=2, tn=512, **tok)
        xp, xs = _ffn_call(xp, xs, norm_ffb3, ffb_w_gu, ffb_w_down, l, tf=512, **tok)

        for lst, val in zip(st_p, (np_p, nk_p, nv_p, nc_p, nh_p)):
            lst.append(val)
        for lst, val in zip(st_s, (np_s, nk_s, nv_s, nc_s, nh_s)):
            lst.append(val)

    def states(st, nseq):
        pool = jnp.stack(st[0])[:, :, POOL_HIST - POOL_KEEP:, :]
        k = jnp.stack(st[1]).reshape(L, nseq, WINDOW, N_KV_HEADS, HEAD_DIM)
        v = jnp.stack(st[2]).reshape(L, nseq, WINDOW, N_KV_HEADS, HEAD_DIM)
        conv = jnp.stack(st[3])[:, :, CONV_HIST - (CONV_WIDTH - 1):, :]
        h = jnp.stack(st[4]).reshape(L, nseq, lru_c)
        return pool, k, v, conv, h

    pool_p, k_p, v_p, conv_p, h_p = states(st_p, Bp)
    pool_s, k_s, v_s, conv_s, h_s = states(st_s, Bs)
    y_p = xp.reshape(Bp, Sp, D)
    y_s = xs.reshape(Bs, Ss, D)
    return (y_p, y_s, pool_p, pool_s, k_p, k_s, v_p, v_s, conv_p, conv_s, h_p, h_s)
```

```python
import functools

import jax
import jax.numpy as jnp
from jax import lax
from jax.experimental import pallas as pl
from jax.experimental.pallas import tpu as pltpu

F32 = jnp.float32
BF16 = jnp.bfloat16

RMS_EPS = 1e-6
NEG_INF = -1e30
FFN_RES_WEIGHT = 0.5
POOL_WINDOWS = (2, 4, 8, 16)
POOL_KEEP = max(POOL_WINDOWS) - 1
POOL_HIST = 16
HEAD_DIM = 64
N_KV_HEADS = 4
GQA_GROUP = 4
WINDOW = 128
ROT_DIM = HEAD_DIM // 4
ROPE_THETA = 500000.0
CONV_WIDTH = 4
CONV_HIST = 8
LRU_C = 8.0
LRU_BLOCK_DIM = 64
PAST_LEN = 16384
LANES = 128
SUBLANES = 8
SOFTMAX_ROWS = 256
TOKEN_TILES = 8
MIB = 1024 * 1024


def _bdot(a, b):
    return jnp.dot(a, b, preferred_element_type=F32)


def _bdot_nt(a, b):
    return lax.dot_general(a, b, (((1,), (1,)), ((), ())), preferred_element_type=F32)


def _rms_bf16(x, g):
    ms = jnp.mean(x * x, axis=-1, keepdims=True)
    return ((x * lax.rsqrt(ms + RMS_EPS)) * g).astype(BF16)


def _sigmoid(x):
    return 0.5 * jnp.tanh(0.5 * x) + 0.5


def _params(sem, vmem_mib):
    return pltpu.CompilerParams(dimension_semantics=sem, vmem_limit_bytes=vmem_mib * MIB)


def _ffn_kernel(xp_ref, xs_ref, g_ref, wg_ref, wu_ref, wd_ref, op_ref, os_ref, xn_ref):
    tp = xp_ref.shape[0]

    @pl.when(pl.program_id(1) == 0)
    def _():
        xp = xp_ref[...]
        xs = xs_ref[...]
        xn_ref[0:tp, :] = _rms_bf16(xp, g_ref[...])
        xn_ref[tp:, :] = _rms_bf16(xs, g_ref[...])
        op_ref[...] = xp
        os_ref[...] = xs

    xn = xn_ref[...]
    g = _bdot(xn, wg_ref[...].astype(BF16))
    u = _bdot(xn, wu_ref[...].astype(BF16))
    h = (FFN_RES_WEIGHT * ((g * _sigmoid(g)) * u)).astype(BF16)
    res = _bdot(h, wd_ref[...].astype(BF16))
    op_ref[...] += res[:tp]
    os_ref[...] += res[tp:]


def _ffn_call(xp, xs, norm, w_gu, w_down, layer, *, ntiles, tf):
    Tp, D = xp.shape
    Ts = xs.shape[0]
    tp, ts = Tp // ntiles, Ts // ntiles
    dff = w_down.shape[1]
    nj = dff // tf
    rows = lambda t: pl.BlockSpec((t, D), lambda i, j: (i, 0))
    once = lambda t: pl.BlockSpec((t, D), lambda i, j: (i, 0), pipeline_mode=pl.Buffered(1))
    return pl.pallas_call(
        _ffn_kernel,
        grid=(ntiles, nj),
        in_specs=[
            once(tp), once(ts),
            pl.BlockSpec((None, 1, D), lambda i, j: (layer, 0, 0)),
            pl.BlockSpec((None, D, tf), lambda i, j: (layer, 0, j)),
            pl.BlockSpec((None, D, tf), lambda i, j: (layer, 0, j + nj)),
            pl.BlockSpec((None, tf, D), lambda i, j: (layer, j, 0)),
        ],
        out_specs=[rows(tp), rows(ts)],
        out_shape=[jax.ShapeDtypeStruct((Tp, D), F32), jax.ShapeDtypeStruct((Ts, D), F32)],
        scratch_shapes=[pltpu.VMEM((tp + ts, D), BF16)],
        compiler_params=_params(("arbitrary", "arbitrary"), 60),
        name="ffn",
    )(xp, xs, norm, w_gu, w_gu, w_down)


def _tile_rows(sub, tp, ts):
    tm = tp + ts
    return [((slice(s * tp, (s + 1) * tp), slice(s * tm, s * tm + tp)),
             (slice(s * ts, (s + 1) * ts), slice(s * tm + tp, (s + 1) * tm))) for s in range(sub)]


def _inproj_kernel(xp_ref, xs_ref, g_ref, w_ref, zp_ref, zs_ref, xn_ref, *, sub):
    tiles = _tile_rows(sub, xp_ref.shape[0] // sub, xs_ref.shape[0] // sub)

    @pl.when(pl.program_id(1) == 0)
    def _():
        for (p_rows, p_int), (s_rows, s_int) in tiles:
            xn_ref[p_int, :] = _rms_bf16(xp_ref[p_rows, :], g_ref[...])
            xn_ref[s_int, :] = _rms_bf16(xs_ref[s_rows, :], g_ref[...])

    res = _bdot(xn_ref[...], w_ref[...].astype(BF16))
    for (p_rows, p_int), (s_rows, s_int) in tiles:
        zp_ref[p_rows, :] = res[p_int]
        zs_ref[s_rows, :] = res[s_int]


def _inproj_call(xp, xs, norm, w_in, layer, ncols, *, ntiles, sub, tn):
    Tp, D = xp.shape
    Ts = xs.shape[0]
    ntiles = ntiles // sub
    tp, ts = Tp // ntiles, Ts // ntiles
    rows = lambda t: pl.BlockSpec((t, D), lambda i, j: (i, 0))
    once = lambda t: pl.BlockSpec((t, D), lambda i, j: (i, 0), pipeline_mode=pl.Buffered(1))
    cols = lambda t: pl.BlockSpec((t, tn), lambda i, j: (i, j))
    return pl.pallas_call(
        functools.partial(_inproj_kernel, sub=sub),
        grid=(ntiles, ncols // tn),
        in_specs=[
            once(tp), once(ts),
            pl.BlockSpec((None, 1, D), lambda i, j: (layer, 0, 0)),
            pl.BlockSpec((None, D, tn), lambda i, j: (layer, 0, j)),
        ],
        out_specs=[cols(tp), cols(ts), rows(tp + ts)],
        out_shape=[jax.ShapeDtypeStruct((Tp, ncols), F32), jax.ShapeDtypeStruct((Ts, ncols), F32),
                   jax.ShapeDtypeStruct((Tp + Ts, D), BF16)],
        compiler_params=_params(("arbitrary", "arbitrary"), 60),
        name="inproj",
    )(xp, xs, norm, w_in)


def _merge_kernel(xn_ref, wg0_ref, wg1_ref, wg2_ref, b0p_ref, b0s_ref, b1p_ref, b1s_ref, b2p_ref, b2s_ref,
                  w0_ref, w1_ref, w2_ref, o_ref, lhs_ref):
    tp = b0p_ref.shape[0]

    @pl.when(pl.program_id(1) == 0)
    def _():
        for b, (p_ref, s_ref) in enumerate(((b0p_ref, b0s_ref), (b1p_ref, b1s_ref), (b2p_ref, b2s_ref))):
            lhs_ref[b, 0:tp, :] = p_ref[...]
            lhs_ref[b, tp:, :] = s_ref[...]

    xn = xn_ref[...]

    def term(b, wg_ref, w_ref):
        gate = _sigmoid(_bdot(xn, wg_ref[...].astype(BF16)))
        return gate * _bdot(lhs_ref[b], w_ref[...].astype(BF16))

    m = term(0, wg0_ref, w0_ref) + term(1, wg1_ref, w1_ref) + term(2, wg2_ref, w2_ref)
    o_ref[...] = m.astype(BF16)


def _merge_call(xn, w_in, gate_col0, branches, branch_ws, layer, *, ntiles, tc):
    T, D = xn.shape
    tm = T // ntiles
    W = branches[0][0].shape[1]
    tp, ts = branches[0][0].shape[0] // ntiles, branches[0][1].shape[0] // ntiles
    g0 = gate_col0 // tc
    gstep = D // tc
    gate_w = lambda b: pl.BlockSpec((None, D, tc), lambda i, c: (layer, 0, g0 + b * gstep + c))
    rows = lambda t: pl.BlockSpec((t, W), lambda i, c: (i, 0))
    w_spec = pl.BlockSpec((None, W, tc), lambda i, c: (layer, 0, c))
    return pl.pallas_call(
        _merge_kernel,
        grid=(ntiles, D // tc),
        in_specs=[pl.BlockSpec((tm, D), lambda i, c: (i, 0)), gate_w(0), gate_w(1), gate_w(2)]
                 + [rows(tp), rows(ts)] * 3 + [w_spec] * 3,
        out_specs=pl.BlockSpec((tm, tc), lambda i, c: (i, c)),
        out_shape=jax.ShapeDtypeStruct((T, D), BF16),
        scratch_shapes=[pltpu.VMEM((3, tm, W), BF16)],
        compiler_params=_params(("arbitrary", "arbitrary"), 58),
        name="merge",
    )(xn, w_in, w_in, w_in, *[a for pair in branches for a in pair], *branch_ws)


def _outproj_kernel(xp_ref, xs_ref, m_ref, w_ref, op_ref, os_ref, *, sub):
    res = _bdot(m_ref[...], w_ref[...].astype(BF16))
    for (p_rows, p_int), (s_rows, s_int) in _tile_rows(sub, xp_ref.shape[0] // sub, xs_ref.shape[0] // sub):
        op_ref[p_rows, :] = xp_ref[p_rows, :] + res[p_int]
        os_ref[s_rows, :] = xs_ref[s_rows, :] + res[s_int]


def _outproj_call(xp, xs, m, w_out, layer, *, ntiles, sub, tn):
    Tp, D = xp.shape
    Ts = xs.shape[0]
    ntiles = ntiles // sub
    tp, ts = Tp // ntiles, Ts // ntiles
    cols = lambda t: pl.BlockSpec((t, tn), lambda i, c: (i, c))
    return pl.pallas_call(
        functools.partial(_outproj_kernel, sub=sub),
        grid=(ntiles, D // tn),
        in_specs=[
            cols(tp), cols(ts),
            pl.BlockSpec((tp + ts, D), lambda i, c: (i, 0)),
            pl.BlockSpec((None, D, tn), lambda i, c: (layer, 0, c)),
        ],
        out_specs=[cols(tp), cols(ts)],
        out_shape=[jax.ShapeDtypeStruct((Tp, D), F32), jax.ShapeDtypeStruct((Ts, D), F32)],
        compiler_params=_params(("arbitrary", "arbitrary"), 56),
        name="outproj",
    )(xp, xs, m, w_out)


def _pool_kernel(*refs, bs, tt, nt, pos0, has_past):
    if has_past:
        u_ref, past_ref, w_ref, s_ref, o_ref, np_ref, e_ref = refs
    else:
        u_ref, w_ref, s_ref, o_ref, np_ref, e_ref = refs
    t = pl.program_id(1)
    C = e_ref.shape[-1]
    H = POOL_HIST

    @pl.when(t == 0)
    def _():
        if has_past:
            e_ref[:, 0:H, :] = past_ref[...]
        else:
            e_ref[:, 0:H, :] = jnp.zeros((bs, H, C), F32)

    e_ref[:, H:H + tt, :] = u_ref[...].reshape(bs, tt, C)
    posp1 = lax.broadcasted_iota(jnp.int32, (1, tt, 1), 1) + (t * tt + pos0 + 1)
    gd = C // len(POOL_WINDOWS)
    for g, w in enumerate(POOL_WINDOWS):
        sl = slice(g * gd, (g + 1) * gd)
        e = e_ref[:, :, sl]
        p = e
        s = 1
        while s < w:
            p = p + pltpu.roll(p, s, 1)
            s *= 2
        cnt = jnp.minimum(posp1, w).astype(F32)
        d = (p[:, H:, :] / cnt - e[:, H:, :]).reshape(bs * tt, gd).astype(BF16)
        out = _bdot(d, w_ref[g].astype(BF16)) * s_ref[:, sl]
        o_ref[:, sl] = out.astype(BF16)

    carry = e_ref[:, tt:tt + H, :]
    e_ref[:, 0:H, :] = carry

    @pl.when(t == nt - 1)
    def _():
        np_ref[...] = carry


def _pool_call(z, past, pool_w, pool_scale, layer, *, nseq, seqlen, row0, bs, tt, pos0):
    C = pool_w.shape[1] * pool_w.shape[2]
    nt = seqlen // tt
    rb0 = row0 // (bs * tt)
    has_past = past is not None
    in_specs = [pl.BlockSpec((bs * tt, C), lambda b, t: (rb0 + b * nt + t, 0))]
    args = [z]
    if has_past:
        in_specs.append(pl.BlockSpec((None, bs, POOL_HIST, C), lambda b, t: (layer, b, 0, 0)))
        args.append(past)
    in_specs += [
        pl.BlockSpec((None,) + pool_w.shape[1:], lambda b, t: (layer, 0, 0, 0)),
        pl.BlockSpec((None, 1, C), lambda b, t: (layer, 0, 0)),
    ]
    args += [pool_w, pool_scale]
    return pl.pallas_call(
        functools.partial(_pool_kernel, bs=bs, tt=tt, nt=nt, pos0=pos0, has_past=has_past),
        grid=(nseq // bs, nt),
        in_specs=in_specs,
        out_specs=[
            pl.BlockSpec((bs * tt, C), lambda b, t: (b * nt + t, 0)),
            pl.BlockSpec((bs, POOL_HIST, C), lambda b, t: (b, 0, 0)),
        ],
        out_shape=[
            jax.ShapeDtypeStruct((nseq * seqlen, C), BF16),
            jax.ShapeDtypeStruct((nseq, POOL_HIST, C), F32),
        ],
        scratch_shapes=[pltpu.VMEM((bs, POOL_HIST + tt, C), F32)],
        compiler_params=_params(("arbitrary", "arbitrary"), 48),
        name="pool",
    )(*args)


def _lru_kernel(*refs, bs, tt, nt, has_past):
    if has_past:
        (x_ref, g_ref, cp_ref, h0_ref, cw_ref, cb_ref, wa_ref, ba_ref, wx_ref, bx_ref, lam_ref,
         y_ref, nc_ref, nh_ref, xe_ref, h_ref, a_ref, b_ref, bd_ref) = refs
    else:
        (x_ref, g_ref, cw_ref, cb_ref, wa_ref, ba_ref, wx_ref, bx_ref, lam_ref,
         y_ref, nc_ref, nh_ref, xe_ref, h_ref, a_ref, b_ref, bd_ref) = refs
    t = pl.program_id(2)
    Cc = xe_ref.shape[-1]
    R = bs * tt
    ncol = Cc // LANES
    HC = CONV_HIST
    blk = LRU_BLOCK_DIM

    @pl.when(t == 0)
    def _():
        if has_past:
            xe_ref[:, 0:HC, :] = cp_ref[...]
            h_ref[...] = jnp.broadcast_to(h0_ref[...], (bs, SUBLANES, Cc))
        else:
            xe_ref[:, 0:HC, :] = jnp.zeros((bs, HC, Cc), F32)
            h_ref[...] = jnp.zeros((bs, SUBLANES, Cc), F32)
        rep = (lax.broadcasted_iota(jnp.int32, (blk, LANES), 0)
               == (lax.broadcasted_iota(jnp.int32, (blk, LANES), 1) & (blk - 1))).astype(BF16)
        diag = ((lax.broadcasted_iota(jnp.int32, (LANES, LANES), 0) >= blk)
                == (lax.broadcasted_iota(jnp.int32, (LANES, LANES), 1) >= blk))
        for p in range(ncol):
            for k, w_ref in enumerate((wa_ref, wx_ref)):
                w2 = w_ref[2 * p:2 * p + 2].reshape(2 * blk, blk).astype(BF16)
                full = _bdot(w2, rep)
                bd_ref[p, :, k * LANES:(k + 1) * LANES] = jnp.where(diag, full, 0.0).astype(BF16)

    xe_ref[:, HC:HC + tt, :] = x_ref[...].reshape(bs, tt, Cc)
    xe = xe_ref[...]
    cw = cw_ref[...]
    xc = cb_ref[...] + pltpu.roll(xe, 3, 1)[:, HC:, :] * cw[0:1]
    xc = xc + pltpu.roll(xe, 2, 1)[:, HC:, :] * cw[1:2]
    xc = xc + pltpu.roll(xe, 1, 1)[:, HC:, :] * cw[2:3]
    xc = xc + xe[:, HC:, :] * cw[3:4]
    xc = xc.reshape(R, Cc)

    row8 = lax.broadcasted_iota(jnp.int32, (1, SUBLANES, 1), 1)
    for p in range(ncol):
        col = slice(p * LANES, (p + 1) * LANES)
        xcp = xc[:, col]
        pre = _bdot(xcp.astype(BF16), bd_ref[p])
        r = _sigmoid(pre[:, :LANES] + ba_ref[:, col])
        i = _sigmoid(pre[:, LANES:] + bx_ref[:, col])
        nl = -lam_ref[:, col]
        sp = jnp.maximum(nl, 0.0) + jnp.log1p(jnp.exp(-jnp.abs(nl)))
        la = (-LRU_C * r) * sp
        a = jnp.exp(la)
        bv = jnp.sqrt(jnp.tanh(-la) * (a * a + 1.0)) * (i * xcp)
        a = a.reshape(R // SUBLANES, SUBLANES, LANES)
        bv = bv.reshape(R // SUBLANES, SUBLANES, LANES)
        for s in (1, 2, 4):
            keep = row8 >= s
            a_sh = pltpu.roll(a, s, 1)
            b_sh = pltpu.roll(bv, s, 1)
            bv = jnp.where(keep, a * b_sh + bv, bv)
            a = jnp.where(keep, a * a_sh, a)
        a_ref[:, :, col] = a.reshape(bs, tt, LANES)
        b_ref[:, :, col] = bv.reshape(bs, tt, LANES)

    def carry_step(k, h):
        o = pl.multiple_of(k * SUBLANES, SUBLANES)
        hb = a_ref[:, pl.ds(o, SUBLANES), :] * h + b_ref[:, pl.ds(o, SUBLANES), :]
        b_ref[:, pl.ds(o, SUBLANES), :] = hb
        return jnp.broadcast_to(hb[:, SUBLANES - 1:SUBLANES, :], hb.shape)

    ngroups = tt // SUBLANES
    h_ref[...] = lax.fori_loop(0, ngroups, carry_step, h_ref[...], unroll=min(ngroups, SUBLANES))
    for p in range(ncol):
        col = slice(p * LANES, (p + 1) * LANES)
        hs = b_ref[:, :, col].reshape(R, LANES)
        y_ref[:, col] = (hs * jax.nn.gelu(g_ref[:, col], approximate=True)).astype(BF16)

    tail = xe_ref[:, tt:tt + HC, :]
    xe_ref[:, 0:HC, :] = tail

    @pl.when(t == nt - 1)
    def _():
        nc_ref[...] = tail
        nh_ref[...] = h_ref[:, 0:1, :]


def _lru_call(z, conv_past, h0, conv_w, conv_b, wa, ba, wx, bx, lam, layer, xcol0, gcol0,
              *, nseq, seqlen, row0, bs, tt, cc):
    C = conv_w.shape[2]
    nt = seqlen // tt
    nc = C // cc
    rb0 = row0 // (bs * tt)
    xb0 = xcol0 // cc
    gb0 = gcol0 // cc
    nblk = cc // LRU_BLOCK_DIM
    has_past = conv_past is not None
    in_specs = [
        pl.BlockSpec((bs * tt, cc), lambda b, c, t: (rb0 + b * nt + t, xb0 + c)),
        pl.BlockSpec((bs * tt, cc), lambda b, c, t: (rb0 + b * nt + t, gb0 + c)),
    ]
    args = [z, z]
    if has_past:
        in_specs += [
            pl.BlockSpec((None, bs, CONV_HIST, cc), lambda b, c, t: (layer, b, 0, c)),
            pl.BlockSpec((None, bs, 1, cc), lambda b, c, t: (layer, b, 0, c)),
        ]
        args += [conv_past, h0]
    vec = pl.BlockSpec((None, 1, cc), lambda b, c, t: (layer, 0, c))
    gw = pl.BlockSpec((None, nblk, LRU_BLOCK_DIM, LRU_BLOCK_DIM), lambda b, c, t: (layer, c, 0, 0))
    in_specs += [pl.BlockSpec((None, CONV_WIDTH, cc), lambda b, c, t: (layer, 0, c)), vec, gw, vec, gw, vec, vec]
    args += [conv_w, conv_b, wa, ba, wx, bx, lam]
    return pl.pallas_call(
        functools.partial(_lru_kernel, bs=bs, tt=tt, nt=nt, has_past=has_past),
        grid=(nseq // bs, nc, nt),
        in_specs=in_specs,
        out_specs=[
            pl.BlockSpec((bs * tt, cc), lambda b, c, t: (b * nt + t, c)),
            pl.BlockSpec((bs, CONV_HIST, cc), lambda b, c, t: (b, 0, c)),
            pl.BlockSpec((bs, 1, cc), lambda b, c, t: (b, 0, c)),
        ],
        out_shape=[
            jax.ShapeDtypeStruct((nseq * seqlen, C), BF16),
            jax.ShapeDtypeStruct((nseq, CONV_HIST, C), F32),
            jax.ShapeDtypeStruct((nseq, 1, C), F32),
        ],
        scratch_shapes=[
            pltpu.VMEM((bs, CONV_HIST + tt, cc), F32),
            pltpu.VMEM((bs, SUBLANES, cc), F32),
            pltpu.VMEM((bs, tt, cc), F32),
            pltpu.VMEM((bs, tt, cc), F32),
            pltpu.VMEM((cc // LANES, LANES, 2 * LANES), BF16),
        ],
        compiler_params=_params(("arbitrary", "arbitrary", "arbitrary"), 48),
        name="lru",
    )(*args)


def _attn_stream_kernel(q_ref, kv_ref, qn_ref, kn_ref, cos_ref, s1_ref, s2_ref, sink_ref,
                        o_ref, nk_ref, nv_ref, kb_ref, vb_ref, vt_ref, sc_ref, p_ref, *, qb, nb, pos0, layer):
    n = pl.program_id(1)
    KV = kb_ref.shape[-1]
    nkeys = kb_ref.shape[0]
    half = HEAD_DIM
    lane = lax.broadcasted_iota(jnp.int32, (1, LANES), 1)
    seg_ones = ((lax.broadcasted_iota(jnp.int32, (LANES, LANES), 0) >= half)
                == (lax.broadcasted_iota(jnp.int32, (LANES, LANES), 1) >= half)).astype(BF16)
    cosw = cos_ref[...]
    s1w = s1_ref[...]
    s2w = s2_ref[...]
    win = lambda w: slice(w * LANES, (w + 1) * LANES)

    def norm_rot(xw, gain):
        sq = xw * xw
        hi = sq.astype(BF16)
        lo = (sq - hi.astype(F32)).astype(BF16)
        ss = _bdot(hi, seg_ones) + _bdot(lo, seg_ones)
        y = (xw * lax.rsqrt(ss * (1.0 / HEAD_DIM) + RMS_EPS)) * gain
        return y * cosw + pltpu.roll(y, LANES - ROT_DIM // 2, 1) * s1w + pltpu.roll(y, ROT_DIM // 2, 1) * s2w

    si = lax.broadcasted_iota(jnp.int32, (nkeys, 2 * qb), 0)
    ci = lax.broadcasted_iota(jnp.int32, (nkeys, 2 * qb), 1)
    qi = ci & (qb - 1)
    kpos0 = pos0 + n * qb - WINDOW
    valid = (si >= qi) & (si <= qi + WINDOW) & (si + kpos0 >= 0)
    first_half = lax.broadcasted_iota(jnp.int32, (1, 2 * qb), 1) < qb

    @pl.when(n == 0)
    def _():
        kb_ref[0:WINDOW, :] = jnp.zeros((WINDOW, KV), F32)
        vb_ref[0:WINDOW, :] = jnp.zeros((WINDOW, KV), F32)
        vt_ref[:, 0:WINDOW] = jnp.zeros((KV, WINDOW), F32)

    kvx = kv_ref[...]
    for w in range(KV // LANES):
        kb_ref[WINDOW:, win(w)] = norm_rot(kvx[:, win(w)], kn_ref[...])
    vb_ref[WINDOW:, :] = kvx[:, KV:]
    vt_ref[:, WINDOW:] = kvx[:, KV:].T
    qx = q_ref[...]
    q_gain = qn_ref[...] * (HEAD_DIM ** -0.5)
    for kv in range(N_KV_HEADS):
        w, off = divmod(kv, 2)
        own = jnp.where((lane >= half) == (off == 1), kb_ref[:, win(w)], 0.0)
        moved = pltpu.roll(own, half, 1)
        kk = jnp.concatenate([moved, own] if off == 1 else [own, moved], axis=0).astype(BF16)
        qst = jnp.concatenate([norm_rot(qx[:, win(2 * kv + h)], q_gain) for h in range(2)],
                              axis=0).astype(BF16)
        sc_ref[kv] = _bdot_nt(kk, qst)
    for kv in range(N_KV_HEADS):
        for ch in range(2):
            rows = slice(ch * nkeys, (ch + 1) * nkeys)
            sink = jnp.where(first_half, sink_ref[layer, kv * GQA_GROUP + ch],
                             sink_ref[layer, kv * GQA_GROUP + 2 + ch])
            sblk = jnp.where(valid, sc_ref[kv, rows, :], NEG_INF)
            m = jnp.maximum(jnp.max(sblk, axis=0, keepdims=True), sink)
            p = jnp.exp(sblk - m)
            den = jnp.sum(p, axis=0, keepdims=True) + jnp.exp(sink - m)
            p_ref[kv, rows, :] = (p * (1.0 / den)).astype(BF16)
    zeros_t = jnp.zeros((half, nkeys), F32)
    for kv in range(N_KV_HEADS):
        vth = vt_ref[kv * half:(kv + 1) * half, :]
        vvt = jnp.concatenate([jnp.concatenate([vth, zeros_t], axis=1),
                               jnp.concatenate([zeros_t, vth], axis=1)], axis=0).astype(BF16)
        out = _bdot(vvt, p_ref[kv]).T
        for rh in range(2):
            o_ref[:, win(2 * kv + rh)] = out[rh * qb:(rh + 1) * qb].astype(BF16)
    new_k = kb_ref[WINDOW:, :]
    new_v = vb_ref[WINDOW:, :]
    kb_ref[0:WINDOW, :] = new_k
    vb_ref[0:WINDOW, :] = new_v
    vt_ref[:, 0:WINDOW] = vt_ref[:, WINDOW:]

    @pl.when(n == nb - 1)
    def _():
        nk_ref[0] = new_k
        nv_ref[0] = new_v


def _attn_seq_kernel(q_ref, kv_ref, kp_ref, vp_ref, qn_ref, kn_ref, cos_ref, s1_ref, s2_ref, sink_ref,
                     o_ref, nk_ref, nv_ref, q_scr, k_scr, o_scr, *, bs, qb, pos0, layer):
    KV = k_scr.shape[-1]
    nkeys = 2 * WINDOW
    half = HEAD_DIM
    lane = lax.broadcasted_iota(jnp.int32, (1, LANES), 1)
    seg_ones = ((lax.broadcasted_iota(jnp.int32, (LANES, LANES), 0) >= half)
                == (lax.broadcasted_iota(jnp.int32, (LANES, LANES), 1) >= half)).astype(BF16)
    R = q_scr.shape[0]
    nq = q_scr.shape[1] // LANES
    nk = KV // LANES
    win = lambda w: slice(w * LANES, (w + 1) * LANES)

    xs = jnp.concatenate([q_ref[:, win(w)] for w in range(nq)] + [kv_ref[:, win(w)] for w in range(nk)], axis=0)
    sq = xs * xs
    hi = sq.astype(BF16)
    lo = (sq - hi.astype(F32)).astype(BF16)
    ss = _bdot(hi, seg_ones) + _bdot(lo, seg_ones)
    y = (xs * lax.rsqrt(ss * (1.0 / HEAD_DIM) + RMS_EPS)).reshape(nq + nk, R, LANES)
    gains = jnp.concatenate([jnp.broadcast_to(qn_ref[...] * (HEAD_DIM ** -0.5), (nq, 1, LANES)),
                             jnp.broadcast_to(kn_ref[...], (nk, 1, LANES))], axis=0)
    y = y * gains
    y = (y * cos_ref[...] + pltpu.roll(y, LANES - ROT_DIM // 2, 2) * s1_ref[...]
         + pltpu.roll(y, ROT_DIM // 2, 2) * s2_ref[...])
    for w in range(nq):
        q_scr[:, win(w)] = y[w]
    for w in range(nk):
        k_scr[:, win(w)] = y[nq + w]

    group = max(1, min(N_KV_HEADS, SOFTMAX_ROWS // (2 * qb)))
    nrow = 2 * group * qb
    qi = lax.broadcasted_iota(jnp.int32, (nrow, nkeys), 0) & (qb - 1)
    si = lax.broadcasted_iota(jnp.int32, (nrow, nkeys), 1)
    valid = (si >= qi) & (si <= qi + WINDOW) & (si + (pos0 - WINDOW) >= 0)
    sink_cols = [
        [jnp.concatenate([jnp.full((qb, 1), sink_ref[layer, kv * GQA_GROUP + 2 * rh + ch], F32)
                          for kv in range(g0, g0 + group) for rh in range(2)], axis=0)
         for ch in range(2)]
        for g0 in range(0, N_KV_HEADS, group)]

    def seq_step(s, c):
        rows = pl.ds(pl.multiple_of(s * qb, qb), qb)
        prev_k = kp_ref[s]
        prev_v = vp_ref[s]
        cur_k = k_scr[rows, :]
        cur_v = kv_ref[rows, KV:]
        pad = jnp.zeros((WINDOW - qb, KV), F32)
        k_all = jnp.concatenate([prev_k, cur_k, pad], axis=0)
        v_all = jnp.concatenate([prev_v, cur_v, pad], axis=0)

        def spread(x, kv):
            w, off = divmod(kv, 2)
            lo = jnp.where((lane >= half) == (off == 1), x[:, win(w)], 0.0)
            if off == 1:
                lo = pltpu.roll(lo, half, 1)
            return jnp.concatenate([lo, pltpu.roll(lo, half, 1)], axis=0).astype(BF16)

        for gi, g0 in enumerate(range(0, N_KV_HEADS, group)):
            scores = []
            for kv in range(g0, g0 + group):
                qst = jnp.concatenate([q_scr[rows, win(2 * kv + h)] for h in range(2)], axis=0).astype(BF16)
                scores.append(_bdot_nt(qst, spread(k_all, kv)))
            sc = jnp.concatenate(scores, axis=0)
            prob_cols = []
            for ch in range(2):
                sink = sink_cols[gi][ch]
                sblk = jnp.where(valid, sc[:, ch * nkeys:(ch + 1) * nkeys], NEG_INF)
                m = jnp.maximum(jnp.max(sblk, axis=-1, keepdims=True), sink)
                p = jnp.exp(sblk - m)
                den = jnp.sum(p, axis=-1, keepdims=True) + jnp.exp(sink - m)
                prob_cols.append(p * (1.0 / den))
            probs = jnp.concatenate(prob_cols, axis=1).astype(BF16)
            for j, kv in enumerate(range(g0, g0 + group)):
                out = _bdot(probs[2 * j * qb:2 * (j + 1) * qb], spread(v_all, kv))
                for rh in range(2):
                    o_scr[rows, win(2 * kv + rh)] = out[rh * qb:(rh + 1) * qb]
        nk_ref[s] = jnp.concatenate([prev_k[qb:], cur_k], axis=0)
        nv_ref[s] = jnp.concatenate([prev_v[qb:], cur_v], axis=0)
        return c

    lax.fori_loop(0, bs, seq_step, 0, unroll=min(bs, 4))
    o_ref[...] = o_scr[...].astype(BF16)


def _attn_call(z, k_past, v_past, qn, kn, cos_t, s1_t, s2_t, sinks, layer, qcol0, kvcol0,
               *, nseq, seqlen, row0, bs, qb, pos0):
    A = N_KV_HEADS * GQA_GROUP * HEAD_DIM
    KV = N_KV_HEADS * HEAD_DIM
    nb = seqlen // qb
    rb0 = row0 // (bs * qb)
    has_past = k_past is not None
    R = bs * qb
    if has_past:
        assert nb == 1 and qb < WINDOW
        body = functools.partial(_attn_seq_kernel, bs=bs, qb=qb, pos0=pos0, layer=layer)
        scratch = [pltpu.VMEM((R, A), F32), pltpu.VMEM((R, KV), F32), pltpu.VMEM((R, A), F32)]
    else:
        assert bs == 1 and qb == WINDOW
        body = functools.partial(_attn_stream_kernel, qb=qb, nb=nb, pos0=pos0, layer=layer)
        scratch = [pltpu.VMEM((2 * WINDOW, KV), F32), pltpu.VMEM((2 * WINDOW, KV), F32),
                   pltpu.VMEM((KV, 2 * WINDOW), F32),
                   pltpu.VMEM((N_KV_HEADS, 4 * WINDOW, 2 * qb), F32),
                   pltpu.VMEM((N_KV_HEADS, 4 * WINDOW, 2 * qb), BF16)]
    in_specs = [
        pl.BlockSpec((bs * qb, A), lambda b, n: (rb0 + b * nb + n, qcol0 // A)),
        pl.BlockSpec((bs * qb, 2 * KV), lambda b, n: (rb0 + b * nb + n, kvcol0 // (2 * KV))),
    ]
    args = [z, z]
    if has_past:
        past_spec = pl.BlockSpec((None, bs, WINDOW, KV), lambda b, n: (layer, b, 0, 0))
        in_specs += [past_spec, past_spec]
        args += [k_past, v_past]
    gain = pl.BlockSpec((None, 1, LANES), lambda b, n: (layer, 0, 0))
    tab = pl.BlockSpec((R, LANES), lambda b, n: (n, 0))
    in_specs += [gain, gain, tab, tab, tab, pl.BlockSpec(memory_space=pltpu.SMEM)]
    args += [qn, kn, cos_t, s1_t, s2_t, sinks]
    state = pl.BlockSpec((bs, WINDOW, KV), lambda b, n: (b, 0, 0))
    return pl.pallas_call(
        body,
        grid=(nseq // bs, nb),
        in_specs=in_specs,
        out_specs=[pl.BlockSpec((bs * qb, A), lambda b, n: (b * nb + n, 0)), state, state],
        out_shape=[
            jax.ShapeDtypeStruct((nseq * seqlen, A), BF16),
            jax.ShapeDtypeStruct((nseq, WINDOW, KV), F32),
            jax.ShapeDtypeStruct((nseq, WINDOW, KV), F32),
        ],
        scratch_shapes=scratch,
        compiler_params=_params(("arbitrary", "arbitrary"), 56),
        name="attn",
    )(*args)


def _rope_tables(pos, nrep):
    half = ROT_DIM // 2
    inv = ROPE_THETA ** (-jnp.arange(half, dtype=F32) / half)
    ang = pos.astype(F32)[:, None] * inv[None, :]
    cos, sin = jnp.cos(ang), jnp.sin(ang)
    T = pos.shape[0]
    ones = jnp.ones((T, HEAD_DIM - ROT_DIM), F32)
    zeros = jnp.zeros((T, HEAD_DIM - ROT_DIM), F32)
    zh = jnp.zeros((T, half), F32)
    cos_t = jnp.concatenate([cos, cos, ones], axis=1)
    s1_t = jnp.concatenate([-sin, zh, zeros], axis=1)
    s2_t = jnp.concatenate([zh, sin, zeros], axis=1)
    rep = LANES // HEAD_DIM
    return tuple(jnp.tile(t, (nrep, rep)) for t in (cos_t, s1_t, s2_t))


def kernel(x_prompt, x_sample, state_pool, cache_k_win, cache_v_win, state_conv, state_rglru, norm_ffa, ffa_w_gu, ffa_w_down, norm_mix, w_in, pool_w, pool_scale, q_norm, k_norm, attn_sinks, conv_w, conv_b, lru_gate_a_w, lru_gate_a_b, lru_gate_x_w, lru_gate_x_b, lru_lambda, w_branch_pool, w_branch_attn, w_branch_lru, w_out, norm_ffb, ffb_w_gu, ffb_w_down):
    Bp, Sp, D = x_prompt.shape
    Bs, Ss, _ = x_sample.shape
    L = norm_ffa.shape[0]
    Tp, Ts = Bp * Sp, Bs * Ss
    pool_c = pool_scale.shape[1]
    attn_c = N_KV_HEADS * GQA_GROUP * HEAD_DIM
    kv_c = N_KV_HEADS * HEAD_DIM
    lru_c = conv_w.shape[2]
    q0 = pool_c
    kv0 = q0 + attn_c
    xl0 = kv0 + 2 * kv_c
    gl0 = xl0 + lru_c
    gate0 = gl0 + lru_c

    tok = dict(ntiles=TOKEN_TILES)
    xp = x_prompt.reshape(Tp, D)
    xs = x_sample.reshape(Ts, D)

    vec3 = lambda a: a.reshape(L, 1, a.shape[-1])
    norm_ffa3, norm_mix3, norm_ffb3 = vec3(norm_ffa), vec3(norm_mix), vec3(norm_ffb)
    pool_scale3 = vec3(pool_scale)
    conv_b3, ba3, bx3, lam3 = vec3(conv_b), vec3(lru_gate_a_b), vec3(lru_gate_x_b), vec3(lru_lambda)
    qn3 = vec3(jnp.tile(q_norm, (1, LANES // HEAD_DIM)))
    kn3 = vec3(jnp.tile(k_norm, (1, LANES // HEAD_DIM)))
    pool_past = jnp.pad(state_pool, ((0, 0), (0, 0), (POOL_HIST - POOL_KEEP, 0), (0, 0)))
    conv_past = jnp.pad(state_conv, ((0, 0), (0, 0), (CONV_HIST - (CONV_WIDTH - 1), 0), (0, 0)))
    h0 = state_rglru.reshape(L, Bs, 1, lru_c)
    k_past = cache_k_win.reshape(L, Bs, WINDOW, kv_c)
    v_past = cache_v_win.reshape(L, Bs, WINDOW, kv_c)
    rope_p = _rope_tables(jnp.arange(Sp), 1)
    rope_s = _rope_tables(PAST_LEN + jnp.arange(Ss), Bs)

    prompt = dict(nseq=Bp, seqlen=Sp, row0=0)
    sample = dict(nseq=Bs, seqlen=Ss, row0=0)
    st_p = ([], [], [], [], [])
    st_s = ([], [], [], [], [])
    for l in range(L):
        xp, xs = _ffn_call(xp, xs, norm_ffa3, ffa_w_gu, ffa_w_down, l, tf=512, **tok)
        zp, zs, xn = _inproj_call(xp, xs, norm_mix3, w_in, l, gate0, sub=2, tn=512, **tok)

        pool_p, np_p = _pool_call(zp, None, pool_w, pool_scale3, l, bs=1, tt=512, pos0=0, **prompt)
        pool_s, np_s = _pool_call(zs, pool_past, pool_w, pool_scale3, l, bs=Bs, tt=Ss, pos0=PAST_LEN, **sample)

        att_p, nk_p, nv_p = _attn_call(zp, None, None, qn3, kn3, *rope_p, attn_sinks, l, q0, kv0,
                                       bs=1, qb=WINDOW, pos0=0, **prompt)
        att_s, nk_s, nv_s = _attn_call(zs, k_past, v_past, qn3, kn3, *rope_s, attn_sinks, l, q0, kv0,
                                       bs=Bs, qb=Ss, pos0=PAST_LEN, **sample)

        lru_args = (conv_w, conv_b3, lru_gate_a_w, ba3, lru_gate_x_w, bx3, lam3, l, xl0, gl0)
        lru_p, nc_p, nh_p = _lru_call(zp, None, None, *lru_args, bs=1, tt=512, cc=512, **prompt)
        lru_s, nc_s, nh_s = _lru_call(zs, conv_past, h0, *lru_args, bs=Bs, tt=Ss, cc=512, **sample)

        m = _merge_call(xn, w_in, gate0, ((pool_p, pool_s), (att_p, att_s), (lru_p, lru_s)),
                        (w_branch_pool, w_branch_attn, w_branch_lru), l, tc=256, **tok)
        xp, xs = _outproj_call(xp, xs, m, w_out, l, sub=2, tn=512, **tok)
        xp, xs = _ffn_call(xp, xs, norm_ffb3, ffb_w_gu, ffb_w_down, l, tf=512, **tok)

        for lst, val in zip(st_p, (np_p, nk_p, nv_p, nc_p, nh_p)):
            lst.append(val)
        for lst, val in zip(st_s, (np_s, nk_s, nv_s, nc_s, nh_s)):
            lst.append(val)

    def states(st, nseq):
        pool = jnp.stack(st[0])[:, :, POOL_HIST - POOL_KEEP:, :]
        k = jnp.stack(st[1]).reshape(L, nseq, WINDOW, N_KV_HEADS, HEAD_DIM)
        v = jnp.stack(st[2]).reshape(L, nseq, WINDOW, N_KV_HEADS, HEAD_DIM)
        conv = jnp.stack(st[3])[:, :, CONV_HIST - (CONV_WIDTH - 1):, :]
        h = jnp.stack(st[4]).reshape(L, nseq, lru_c)
        return pool, k, v, conv, h

    pool_p, k_p, v_p, conv_p, h_p = states(st_p, Bp)
    pool_s, k_s, v_s, conv_s, h_s = states(st_s, Bs)
    y_p = xp.reshape(Bp, Sp, D)
    y_s = xs.reshape(Bs, Ss, D)
    return (y_p, y_s, pool_p, pool_s, k_p, k_s, v_p, v_s, conv_p, conv_s, h_p, h_s)
```

```python
import functools

import jax
import jax.numpy as jnp
from jax import lax
from jax.experimental import pallas as pl
from jax.experimental.pallas import tpu as pltpu

F32 = jnp.float32
BF16 = jnp.bfloat16

RMS_EPS = 1e-6
NEG_INF = -1e30
FFN_RES_WEIGHT = 0.5
POOL_WINDOWS = (2, 4, 8, 16)
POOL_KEEP = max(POOL_WINDOWS) - 1
POOL_HIST = 16
HEAD_DIM = 64
N_KV_HEADS = 4
GQA_GROUP = 4
WINDOW = 128
ROT_DIM = HEAD_DIM // 4
ROPE_THETA = 500000.0
CONV_WIDTH = 4
CONV_HIST = 8
LRU_C = 8.0
LRU_BLOCK_DIM = 64
PAST_LEN = 16384
LANES = 128
SUBLANES = 8
SOFTMAX_ROWS = 256
TOKEN_TILES = 8
MIB = 1024 * 1024


def _bdot(a, b):
    return jnp.dot(a, b, preferred_element_type=F32)


def _bdot_nt(a, b):
    return lax.dot_general(a, b, (((1,), (1,)), ((), ())), preferred_element_type=F32)


def _rms_bf16(x, g):
    ms = jnp.mean(x * x, axis=-1, keepdims=True)
    return ((x * lax.rsqrt(ms + RMS_EPS)) * g).astype(BF16)


def _sigmoid(x):
    return 0.5 * jnp.tanh(0.5 * x) + 0.5


def _params(sem, vmem_mib):
    return pltpu.CompilerParams(dimension_semantics=sem, vmem_limit_bytes=vmem_mib * MIB)


def _ffn_kernel(xp_hbm, xs_ref, g_ref, wg_ref, wu_ref, wd_ref, op_ref, os_ref, xn_ref, xbuf_ref, sem):
    tp = xbuf_ref.shape[0]
    tile_x = _TilePrefetch(xp_hbm, xbuf_ref, sem)

    @pl.when(pl.program_id(1) == 0)
    def _():
        tile_x.wait_current()
        xp = xbuf_ref[...]
        xs = xs_ref[...]
        xn_ref[0:tp, :] = _rms_bf16(xp, g_ref[...])
        xn_ref[tp:, :] = _rms_bf16(xs, g_ref[...])
        op_ref[...] = xp
        os_ref[...] = xs

    xn = xn_ref[...]
    g = _bdot(xn, wg_ref[...].astype(BF16))
    u = _bdot(xn, wu_ref[...].astype(BF16))
    h = (FFN_RES_WEIGHT * ((g * _sigmoid(g)) * u)).astype(BF16)
    res = _bdot(h, wd_ref[...].astype(BF16))
    op_ref[...] += res[:tp]
    os_ref[...] += res[tp:]
    tile_x.start_next()


class _TilePrefetch:
    def __init__(self, hbm_ref, buf_ref, sem):
        self.hbm_ref, self.buf_ref, self.sem = hbm_ref, buf_ref, sem
        self.rows = buf_ref.shape[0]

    def _copy(self, tile):
        start = pl.multiple_of(tile * self.rows, SUBLANES)
        return pltpu.make_async_copy(self.hbm_ref.at[pl.ds(start, self.rows), :], self.buf_ref, self.sem)

    def wait_current(self):
        i = pl.program_id(0)

        @pl.when(i == 0)
        def _():
            self._copy(i).start()

        self._copy(i).wait()

    def start_next(self):
        i = pl.program_id(0)

        @pl.when((pl.program_id(1) == pl.num_programs(1) - 1) & (i + 1 < pl.num_programs(0)))
        def _():
            self._copy(i + 1).start()


def _ffn_call(xp, xs, norm, w_gu, w_down, layer, *, ntiles, tf):
    Tp, D = xp.shape
    Ts = xs.shape[0]
    tp, ts = Tp // ntiles, Ts // ntiles
    dff = w_down.shape[1]
    nj = dff // tf
    rows = lambda t: pl.BlockSpec((t, D), lambda i, j: (i, 0))
    return pl.pallas_call(
        _ffn_kernel,
        grid=(ntiles, nj),
        in_specs=[
            pl.BlockSpec(memory_space=pl.ANY), rows(ts),
            pl.BlockSpec((None, 1, D), lambda i, j: (layer, 0, 0)),
            pl.BlockSpec((None, D, tf), lambda i, j: (layer, 0, j)),
            pl.BlockSpec((None, D, tf), lambda i, j: (layer, 0, j + nj)),
            pl.BlockSpec((None, tf, D), lambda i, j: (layer, j, 0)),
        ],
        out_specs=[rows(tp), rows(ts)],
        out_shape=[jax.ShapeDtypeStruct((Tp, D), F32), jax.ShapeDtypeStruct((Ts, D), F32)],
        scratch_shapes=[pltpu.VMEM((tp + ts, D), BF16), pltpu.VMEM((tp, D), F32),
                        pltpu.SemaphoreType.DMA(())],
        compiler_params=_params(("arbitrary", "arbitrary"), 60),
        name="ffn",
    )(xp, xs, norm, w_gu, w_gu, w_down)


def _tile_rows(sub, tp, ts):
    tm = tp + ts
    return [((slice(s * tp, (s + 1) * tp), slice(s * tm, s * tm + tp)),
             (slice(s * ts, (s + 1) * ts), slice(s * tm + tp, (s + 1) * tm))) for s in range(sub)]


def _inproj_kernel(xp_hbm, xs_ref, g_ref, w_ref, zp_ref, zs_ref, xn_ref, xbuf_ref, sem, *, sub):
    tiles = _tile_rows(sub, xbuf_ref.shape[0] // sub, xs_ref.shape[0] // sub)
    tile_x = _TilePrefetch(xp_hbm, xbuf_ref, sem)

    @pl.when(pl.program_id(1) == 0)
    def _():
        tile_x.wait_current()
        for (p_rows, p_int), (s_rows, s_int) in tiles:
            xn_ref[p_int, :] = _rms_bf16(xbuf_ref[p_rows, :], g_ref[...])
            xn_ref[s_int, :] = _rms_bf16(xs_ref[s_rows, :], g_ref[...])

    res = _bdot(xn_ref[...], w_ref[...].astype(BF16))
    for (p_rows, p_int), (s_rows, s_int) in tiles:
        zp_ref[p_rows, :] = res[p_int]
        zs_ref[s_rows, :] = res[s_int]
    tile_x.start_next()


def _inproj_call(xp, xs, norm, w_in, layer, ncols, *, ntiles, sub, tn):
    Tp, D = xp.shape
    Ts = xs.shape[0]
    ntiles = ntiles // sub
    tp, ts = Tp // ntiles, Ts // ntiles
    rows = lambda t: pl.BlockSpec((t, D), lambda i, j: (i, 0))
    cols = lambda t: pl.BlockSpec((t, tn), lambda i, j: (i, j))
    return pl.pallas_call(
        functools.partial(_inproj_kernel, sub=sub),
        grid=(ntiles, ncols // tn),
        in_specs=[
            pl.BlockSpec(memory_space=pl.ANY), rows(ts),
            pl.BlockSpec((None, 1, D), lambda i, j: (layer, 0, 0)),
            pl.BlockSpec((None, D, tn), lambda i, j: (layer, 0, j)),
        ],
        out_specs=[cols(tp), cols(ts), rows(tp + ts)],
        out_shape=[jax.ShapeDtypeStruct((Tp, ncols), F32), jax.ShapeDtypeStruct((Ts, ncols), F32),
                   jax.ShapeDtypeStruct((Tp + Ts, D), BF16)],
        scratch_shapes=[pltpu.VMEM((tp, D), F32), pltpu.SemaphoreType.DMA(())],
        compiler_params=_params(("arbitrary", "arbitrary"), 60),
        name="inproj",
    )(xp, xs, norm, w_in)


def _merge_kernel(xn_ref, wg0_ref, wg1_ref, wg2_ref, b0p_ref, b0s_ref, b1p_ref, b1s_ref, b2p_ref, b2s_ref,
                  w0_ref, w1_ref, w2_ref, o_ref, lhs_ref):
    tp = b0p_ref.shape[0]

    @pl.when(pl.program_id(1) == 0)
    def _():
        for b, (p_ref, s_ref) in enumerate(((b0p_ref, b0s_ref), (b1p_ref, b1s_ref), (b2p_ref, b2s_ref))):
            lhs_ref[b, 0:tp, :] = p_ref[...]
            lhs_ref[b, tp:, :] = s_ref[...]

    xn = xn_ref[...]

    def term(b, wg_ref, w_ref):
        gate = _sigmoid(_bdot(xn, wg_ref[...].astype(BF16)))
        return gate * _bdot(lhs_ref[b], w_ref[...].astype(BF16))

    m = term(0, wg0_ref, w0_ref) + term(1, wg1_ref, w1_ref) + term(2, wg2_ref, w2_ref)
    o_ref[...] = m.astype(BF16)


def _merge_call(xn, w_in, gate_col0, branches, branch_ws, layer, *, ntiles, tc):
    T, D = xn.shape
    tm = T // ntiles
    W = branches[0][0].shape[1]
    tp, ts = branches[0][0].shape[0] // ntiles, branches[0][1].shape[0] // ntiles
    g0 = gate_col0 // tc
    gstep = D // tc
    gate_w = lambda b: pl.BlockSpec((None, D, tc), lambda i, c: (layer, 0, g0 + b * gstep + c))
    rows = lambda t: pl.BlockSpec((t, W), lambda i, c: (i, 0))
    w_spec = pl.BlockSpec((None, W, tc), lambda i, c: (layer, 0, c))
    return pl.pallas_call(
        _merge_kernel,
        grid=(ntiles, D // tc),
        in_specs=[pl.BlockSpec((tm, D), lambda i, c: (i, 0)), gate_w(0), gate_w(1), gate_w(2)]
                 + [rows(tp), rows(ts)] * 3 + [w_spec] * 3,
        out_specs=pl.BlockSpec((tm, tc), lambda i, c: (i, c)),
        out_shape=jax.ShapeDtypeStruct((T, D), BF16),
        scratch_shapes=[pltpu.VMEM((3, tm, W), BF16)],
        compiler_params=_params(("arbitrary", "arbitrary"), 58),
        name="merge",
    )(xn, w_in, w_in, w_in, *[a for pair in branches for a in pair], *branch_ws)


def _outproj_kernel(xp_ref, xs_ref, m_ref, w_ref, op_ref, os_ref, *, sub):
    res = _bdot(m_ref[...], w_ref[...].astype(BF16))
    for (p_rows, p_int), (s_rows, s_int) in _tile_rows(sub, xp_ref.shape[0] // sub, xs_ref.shape[0] // sub):
        op_ref[p_rows, :] = xp_ref[p_rows, :] + res[p_int]
        os_ref[s_rows, :] = xs_ref[s_rows, :] + res[s_int]


def _outproj_call(xp, xs, m, w_out, layer, *, ntiles, sub, tn):
    Tp, D = xp.shape
    Ts = xs.shape[0]
    ntiles = ntiles // sub
    tp, ts = Tp // ntiles, Ts // ntiles
    cols = lambda t: pl.BlockSpec((t, tn), lambda i, c: (i, c))
    return pl.pallas_call(
        functools.partial(_outproj_kernel, sub=sub),
        grid=(ntiles, D // tn),
        in_specs=[
            cols(tp), cols(ts),
            pl.BlockSpec((tp + ts, D), lambda i, c: (i, 0)),
            pl.BlockSpec((None, D, tn), lambda i, c: (layer, 0, c)),
        ],
        out_specs=[cols(tp), cols(ts)],
        out_shape=[jax.ShapeDtypeStruct((Tp, D), F32), jax.ShapeDtypeStruct((Ts, D), F32)],
        compiler_params=_params(("arbitrary", "arbitrary"), 56),
        name="outproj",
    )(xp, xs, m, w_out)


def _pool_kernel(*refs, bs, tt, nt, pos0, has_past):
    if has_past:
        u_ref, past_ref, w_ref, s_ref, o_ref, np_ref, e_ref = refs
    else:
        u_ref, w_ref, s_ref, o_ref, np_ref, e_ref = refs
    t = pl.program_id(1)
    C = e_ref.shape[-1]
    H = POOL_HIST

    @pl.when(t == 0)
    def _():
        if has_past:
            e_ref[:, 0:H, :] = past_ref[...]
        else:
            e_ref[:, 0:H, :] = jnp.zeros((bs, H, C), F32)

    e_ref[:, H:H + tt, :] = u_ref[...].reshape(bs, tt, C)
    posp1 = lax.broadcasted_iota(jnp.int32, (1, tt, 1), 1) + (t * tt + pos0 + 1)
    gd = C // len(POOL_WINDOWS)
    for g, w in enumerate(POOL_WINDOWS):
        sl = slice(g * gd, (g + 1) * gd)
        e = e_ref[:, :, sl]
        p = e
        s = 1
        while s < w:
            p = p + pltpu.roll(p, s, 1)
            s *= 2
        inv_cnt = 1.0 / jnp.minimum(posp1, w).astype(F32)
        d = (p[:, H:, :] * inv_cnt - e[:, H:, :]).reshape(bs * tt, gd).astype(BF16)
        out = _bdot(d, w_ref[g].astype(BF16)) * s_ref[:, sl]
        o_ref[:, sl] = out.astype(BF16)

    carry = e_ref[:, tt:tt + H, :]
    e_ref[:, 0:H, :] = carry

    @pl.when(t == nt - 1)
    def _():
        np_ref[...] = carry


def _pool_call(z, past, pool_w, pool_scale, layer, *, nseq, seqlen, row0, bs, tt, pos0):
    C = pool_w.shape[1] * pool_w.shape[2]
    nt = seqlen // tt
    rb0 = row0 // (bs * tt)
    has_past = past is not None
    in_specs = [pl.BlockSpec((bs * tt, C), lambda b, t: (rb0 + b * nt + t, 0))]
    args = [z]
    if has_past:
        in_specs.append(pl.BlockSpec((None, bs, POOL_HIST, C), lambda b, t: (layer, b, 0, 0)))
        args.append(past)
    in_specs += [
        pl.BlockSpec((None,) + pool_w.shape[1:], lambda b, t: (layer, 0, 0, 0)),
        pl.BlockSpec((None, 1, C), lambda b, t: (layer, 0, 0)),
    ]
    args += [pool_w, pool_scale]
    return pl.pallas_call(
        functools.partial(_pool_kernel, bs=bs, tt=tt, nt=nt, pos0=pos0, has_past=has_past),
        grid=(nseq // bs, nt),
        in_specs=in_specs,
        out_specs=[
            pl.BlockSpec((bs * tt, C), lambda b, t: (b * nt + t, 0)),
            pl.BlockSpec((bs, POOL_HIST, C), lambda b, t: (b, 0, 0)),
        ],
        out_shape=[
            jax.ShapeDtypeStruct((nseq * seqlen, C), BF16),
            jax.ShapeDtypeStruct((nseq, POOL_HIST, C), F32),
        ],
        scratch_shapes=[pltpu.VMEM((bs, POOL_HIST + tt, C), F32)],
        compiler_params=_params(("arbitrary", "arbitrary"), 48),
        name="pool",
    )(*args)


def _lru_kernel(*refs, bs, tt, nt, has_past):
    if has_past:
        (x_ref, g_ref, cp_ref, h0_ref, cw_ref, cb_ref, wa_ref, ba_ref, wx_ref, bx_ref, lam_ref,
         y_ref, nc_ref, nh_ref, xe_ref, h_ref, a_ref, b_ref, bd_ref) = refs
    else:
        (x_ref, g_ref, cw_ref, cb_ref, wa_ref, ba_ref, wx_ref, bx_ref, lam_ref,
         y_ref, nc_ref, nh_ref, xe_ref, h_ref, a_ref, b_ref, bd_ref) = refs
    t = pl.program_id(2)
    Cc = xe_ref.shape[-1]
    R = bs * tt
    ncol = Cc // LANES
    HC = CONV_HIST
    blk = LRU_BLOCK_DIM

    @pl.when(t == 0)
    def _():
        if has_past:
            xe_ref[:, 0:HC, :] = cp_ref[...]
            h_ref[...] = jnp.broadcast_to(h0_ref[...], (bs, SUBLANES, Cc))
        else:
            xe_ref[:, 0:HC, :] = jnp.zeros((bs, HC, Cc), F32)
            h_ref[...] = jnp.zeros((bs, SUBLANES, Cc), F32)
        rep = (lax.broadcasted_iota(jnp.int32, (blk, LANES), 0)
               == (lax.broadcasted_iota(jnp.int32, (blk, LANES), 1) & (blk - 1))).astype(BF16)
        diag = ((lax.broadcasted_iota(jnp.int32, (LANES, LANES), 0) >= blk)
                == (lax.broadcasted_iota(jnp.int32, (LANES, LANES), 1) >= blk))
        for p in range(ncol):
            for k, w_ref in enumerate((wa_ref, wx_ref)):
                w2 = w_ref[2 * p:2 * p + 2].reshape(2 * blk, blk).astype(BF16)
                full = _bdot(w2, rep)
                bd_ref[p, :, k * LANES:(k + 1) * LANES] = jnp.where(diag, 0.5 * full, 0.0).astype(BF16)

    xe_ref[:, HC:HC + tt, :] = x_ref[...].reshape(bs, tt, Cc)
    xe = xe_ref[...]
    cw = cw_ref[...]
    xc = cb_ref[...]
    for j in range(CONV_WIDTH):
        shift = CONV_WIDTH - 1 - j
        xs = pltpu.roll(xe, shift, 1) if shift else xe
        xc = xc + xs[:, HC:, :] * cw[j:j + 1]
    xc = xc.reshape(R, Cc)

    row8 = lax.broadcasted_iota(jnp.int32, (1, SUBLANES, 1), 1)
    for p in range(ncol):
        col = slice(p * LANES, (p + 1) * LANES)
        xcp = xc[:, col]
        pre = _bdot(xcp.astype(BF16), bd_ref[p])
        tr = jnp.tanh(pre[:, :LANES] + 0.5 * ba_ref[:, col])
        i = 0.5 * jnp.tanh(pre[:, LANES:] + 0.5 * bx_ref[:, col]) + 0.5
        nl = -lam_ref[:, col]
        sp = jnp.maximum(nl, 0.0) + jnp.log1p(jnp.exp(-jnp.abs(nl)))
        la = (tr + 1.0) * ((-0.5 * LRU_C) * sp)
        a = jnp.exp(la)
        bv = jnp.sqrt(jnp.tanh(-la) * (a * a + 1.0)) * (i * xcp)
        a = a.reshape(R // SUBLANES, SUBLANES, LANES)
        bv = bv.reshape(R // SUBLANES, SUBLANES, LANES)
        for s in (1, 2, 4):
            keep = row8 >= s
            a_sh = pltpu.roll(a, s, 1)
            b_sh = pltpu.roll(bv, s, 1)
            bv = jnp.where(keep, a * b_sh + bv, bv)
            a = jnp.where(keep, a * a_sh, a)
        a_ref[:, :, col] = a.reshape(bs, tt, LANES)
        b_ref[:, :, col] = bv.reshape(bs, tt, LANES)

    def carry_step(k, h):
        o = pl.multiple_of(k * SUBLANES, SUBLANES)
        hb = a_ref[:, pl.ds(o, SUBLANES), :] * h + b_ref[:, pl.ds(o, SUBLANES), :]
        b_ref[:, pl.ds(o, SUBLANES), :] = hb
        return jnp.broadcast_to(hb[:, SUBLANES - 1:SUBLANES, :], hb.shape)

    ngroups = tt // SUBLANES
    h_ref[...] = lax.fori_loop(0, ngroups, carry_step, h_ref[...], unroll=min(ngroups, SUBLANES))
    for p in range(ncol):
        col = slice(p * LANES, (p + 1) * LANES)
        hs = b_ref[:, :, col].reshape(R, LANES)
        y_ref[:, col] = (hs * jax.nn.gelu(g_ref[:, col], approximate=True)).astype(BF16)

    tail = xe_ref[:, tt:tt + HC, :]
    xe_ref[:, 0:HC, :] = tail

    @pl.when(t == nt - 1)
    def _():
        nc_ref[...] = tail
        nh_ref[...] = h_ref[:, 0:1, :]


def _lru_call(z, conv_past, h0, conv_w, conv_b, wa, ba, wx, bx, lam, layer, xcol0, gcol0,
              *, nseq, seqlen, row0, bs, tt, cc):
    C = conv_w.shape[2]
    nt = seqlen // tt
    nc = C // cc
    rb0 = row0 // (bs * tt)
    xb0 = xcol0 // cc
    gb0 = gcol0 // cc
    nblk = cc // LRU_BLOCK_DIM
    has_past = conv_past is not None
    in_specs = [
        pl.BlockSpec((bs * tt, cc), lambda b, c, t: (rb0 + b * nt + t, xb0 + c)),
        pl.BlockSpec((bs * tt, cc), lambda b, c, t: (rb0 + b * nt + t, gb0 + c)),
    ]
    args = [z, z]
    if has_past:
        in_specs += [
            pl.BlockSpec((None, bs, CONV_HIST, cc), lambda b, c, t: (layer, b, 0, c)),
            pl.BlockSpec((None, bs, 1, cc), lambda b, c, t: (layer, b, 0, c)),
        ]
        args += [conv_past, h0]
    vec = pl.BlockSpec((None, 1, cc), lambda b, c, t: (layer, 0, c))
    gw = pl.BlockSpec((None, nblk, LRU_BLOCK_DIM, LRU_BLOCK_DIM), lambda b, c, t: (layer, c, 0, 0))
    in_specs += [pl.BlockSpec((None, CONV_WIDTH, cc), lambda b, c, t: (layer, 0, c)), vec, gw, vec, gw, vec, vec]
    args += [conv_w, conv_b, wa, ba, wx, bx, lam]
    return pl.pallas_call(
        functools.partial(_lru_kernel, bs=bs, tt=tt, nt=nt, has_past=has_past),
        grid=(nseq // bs, nc, nt),
        in_specs=in_specs,
        out_specs=[
            pl.BlockSpec((bs * tt, cc), lambda b, c, t: (b * nt + t, c)),
            pl.BlockSpec((bs, CONV_HIST, cc), lambda b, c, t: (b, 0, c)),
            pl.BlockSpec((bs, 1, cc), lambda b, c, t: (b, 0, c)),
        ],
        out_shape=[
            jax.ShapeDtypeStruct((nseq * seqlen, C), BF16),
            jax.ShapeDtypeStruct((nseq, CONV_HIST, C), F32),
            jax.ShapeDtypeStruct((nseq, 1, C), F32),
        ],
        scratch_shapes=[
            pltpu.VMEM((bs, CONV_HIST + tt, cc), F32),
            pltpu.VMEM((bs, SUBLANES, cc), F32),
            pltpu.VMEM((bs, tt, cc), F32),
            pltpu.VMEM((bs, tt, cc), F32),
            pltpu.VMEM((cc // LANES, LANES, 2 * LANES), BF16),
        ],
        compiler_params=_params(("arbitrary", "arbitrary", "arbitrary"), 48),
        name="lru",
    )(*args)


def _attn_stream_kernel(q_ref, kv_ref, qn_ref, kn_ref, cos_ref, s1_ref, s2_ref, sink_ref,
                        o_ref, nk_ref, nv_ref, kb_ref, vb_ref, vt_ref, sc_ref, p_ref, *, qb, nb, pos0, layer):
    n = pl.program_id(1)
    KV = kb_ref.shape[-1]
    nkeys = kb_ref.shape[0]
    half = HEAD_DIM
    lane = lax.broadcasted_iota(jnp.int32, (1, LANES), 1)
    seg_ones = ((lax.broadcasted_iota(jnp.int32, (LANES, LANES), 0) >= half)
                == (lax.broadcasted_iota(jnp.int32, (LANES, LANES), 1) >= half)).astype(BF16)
    cosw = cos_ref[...]
    s1w = s1_ref[...]
    s2w = s2_ref[...]
    win = lambda w: slice(w * LANES, (w + 1) * LANES)

    def norm_rot(xw, gain):
        sq = xw * xw
        hi = sq.astype(BF16)
        lo = (sq - hi.astype(F32)).astype(BF16)
        ss = _bdot(hi, seg_ones) + _bdot(lo, seg_ones)
        y = (xw * lax.rsqrt(ss * (1.0 / HEAD_DIM) + RMS_EPS)) * gain
        return y * cosw + pltpu.roll(y, LANES - ROT_DIM // 2, 1) * s1w + pltpu.roll(y, ROT_DIM // 2, 1) * s2w

    si = lax.broadcasted_iota(jnp.int32, (nkeys, 2 * qb), 0)
    ci = lax.broadcasted_iota(jnp.int32, (nkeys, 2 * qb), 1)
    qi = ci & (qb - 1)
    kpos0 = pos0 + n * qb - WINDOW
    valid = (si >= qi) & (si <= qi + WINDOW) & (si + kpos0 >= 0)
    first_half = lax.broadcasted_iota(jnp.int32, (1, 2 * qb), 1) < qb

    @pl.when(n == 0)
    def _():
        kb_ref[0:WINDOW, :] = jnp.zeros((WINDOW, KV), F32)
        vb_ref[0:WINDOW, :] = jnp.zeros((WINDOW, KV), F32)
        vt_ref[:, 0:WINDOW] = jnp.zeros((KV, WINDOW), F32)

    kvx = kv_ref[...]
    for w in range(KV // LANES):
        kb_ref[WINDOW:, win(w)] = norm_rot(kvx[:, win(w)], kn_ref[...])
    vb_ref[WINDOW:, :] = kvx[:, KV:]
    vt_ref[:, WINDOW:] = kvx[:, KV:].T
    qx = q_ref[...]
    q_gain = qn_ref[...] * (HEAD_DIM ** -0.5)
    for kv in range(N_KV_HEADS):
        w, off = divmod(kv, 2)
        own = jnp.where((lane >= half) == (off == 1), kb_ref[:, win(w)], 0.0)
        moved = pltpu.roll(own, half, 1)
        kk = jnp.concatenate([moved, own] if off == 1 else [own, moved], axis=0).astype(BF16)
        qst = jnp.concatenate([norm_rot(qx[:, win(2 * kv + h)], q_gain) for h in range(2)],
                              axis=0).astype(BF16)
        sc_ref[kv] = _bdot_nt(kk, qst)
    for kv in range(N_KV_HEADS):
        for ch in range(2):
            rows = slice(ch * nkeys, (ch + 1) * nkeys)
            sink = jnp.where(first_half, sink_ref[layer, kv * GQA_GROUP + ch],
                             sink_ref[layer, kv * GQA_GROUP + 2 + ch])
            sblk = jnp.where(valid, sc_ref[kv, rows, :], NEG_INF)
            m = jnp.maximum(jnp.max(sblk, axis=0, keepdims=True), sink)
            p = jnp.exp(sblk - m)
            den = jnp.sum(p, axis=0, keepdims=True) + jnp.exp(sink - m)
            p_ref[kv, rows, :] = (p * (1.0 / den)).astype(BF16)
    zeros_t = jnp.zeros((half, nkeys), F32)
    for kv in range(N_KV_HEADS):
        vth = vt_ref[kv * half:(kv + 1) * half, :]
        vvt = jnp.concatenate([jnp.concatenate([vth, zeros_t], axis=1),
                               jnp.concatenate([zeros_t, vth], axis=1)], axis=0).astype(BF16)
        out = _bdot(vvt, p_ref[kv]).T
        for rh in range(2):
            o_ref[:, win(2 * kv + rh)] = out[rh * qb:(rh + 1) * qb].astype(BF16)
    new_k = kb_ref[WINDOW:, :]
    new_v = vb_ref[WINDOW:, :]
    kb_ref[0:WINDOW, :] = new_k
    vb_ref[0:WINDOW, :] = new_v
    vt_ref[:, 0:WINDOW] = vt_ref[:, WINDOW:]

    @pl.when(n == nb - 1)
    def _():
        nk_ref[0] = new_k
        nv_ref[0] = new_v


def _attn_seq_kernel(q_ref, kv_ref, kp_ref, vp_ref, qn_ref, kn_ref, cos_ref, s1_ref, s2_ref, sink_ref,
                     o_ref, nk_ref, nv_ref, q_scr, k_scr, o_scr, *, bs, qb, pos0, layer):
    KV = k_scr.shape[-1]
    nkeys = 2 * WINDOW
    half = HEAD_DIM
    lane = lax.broadcasted_iota(jnp.int32, (1, LANES), 1)
    seg_ones = ((lax.broadcasted_iota(jnp.int32, (LANES, LANES), 0) >= half)
                == (lax.broadcasted_iota(jnp.int32, (LANES, LANES), 1) >= half)).astype(BF16)
    R = q_scr.shape[0]
    nq = q_scr.shape[1] // LANES
    nk = KV // LANES
    win = lambda w: slice(w * LANES, (w + 1) * LANES)

    xs = jnp.concatenate([q_ref[:, win(w)] for w in range(nq)] + [kv_ref[:, win(w)] for w in range(nk)], axis=0)
    sq = xs * xs
    hi = sq.astype(BF16)
    lo = (sq - hi.astype(F32)).astype(BF16)
    ss = _bdot(hi, seg_ones) + _bdot(lo, seg_ones)
    y = (xs * lax.rsqrt(ss * (1.0 / HEAD_DIM) + RMS_EPS)).reshape(nq + nk, R, LANES)
    gains = jnp.concatenate([jnp.broadcast_to(qn_ref[...] * (HEAD_DIM ** -0.5), (nq, 1, LANES)),
                             jnp.broadcast_to(kn_ref[...], (nk, 1, LANES))], axis=0)
    y = y * gains
    y = (y * cos_ref[...] + pltpu.roll(y, LANES - ROT_DIM // 2, 2) * s1_ref[...]
         + pltpu.roll(y, ROT_DIM // 2, 2) * s2_ref[...])
    for w in range(nq):
        q_scr[:, win(w)] = y[w]
    for w in range(nk):
        k_scr[:, win(w)] = y[nq + w]

    group = max(1, min(N_KV_HEADS, SOFTMAX_ROWS // (2 * qb)))
    nrow = 2 * group * qb
    qi = lax.broadcasted_iota(jnp.int32, (nrow, nkeys), 0) & (qb - 1)
    si = lax.broadcasted_iota(jnp.int32, (nrow, nkeys), 1)
    valid = (si >= qi) & (si <= qi + WINDOW) & (si + (pos0 - WINDOW) >= 0)
    sink_cols = [
        [jnp.concatenate([jnp.full((qb, 1), sink_ref[layer, kv * GQA_GROUP + 2 * rh + ch], F32)
                          for kv in range(g0, g0 + group) for rh in range(2)], axis=0)
         for ch in range(2)]
        for g0 in range(0, N_KV_HEADS, group)]

    def seq_step(s, c):
        rows = pl.ds(pl.multiple_of(s * qb, qb), qb)
        prev_k = kp_ref[s]
        prev_v = vp_ref[s]
        cur_k = k_scr[rows, :]
        cur_v = kv_ref[rows, KV:]
        pad = jnp.zeros((WINDOW - qb, KV), F32)
        k_all = jnp.concatenate([prev_k, cur_k, pad], axis=0)
        v_all = jnp.concatenate([prev_v, cur_v, pad], axis=0)

        def spread(x, kv):
            w, off = divmod(kv, 2)
            lo = jnp.where((lane >= half) == (off == 1), x[:, win(w)], 0.0)
            if off == 1:
                lo = pltpu.roll(lo, half, 1)
            return jnp.concatenate([lo, pltpu.roll(lo, half, 1)], axis=0).astype(BF16)

        for gi, g0 in enumerate(range(0, N_KV_HEADS, group)):
            scores = []
            for kv in range(g0, g0 + group):
                qst = jnp.concatenate([q_scr[rows, win(2 * kv + h)] for h in range(2)], axis=0).astype(BF16)
                scores.append(_bdot_nt(qst, spread(k_all, kv)))
            sc = jnp.concatenate(scores, axis=0)
            prob_cols = []
            for ch in range(2):
                sink = sink_cols[gi][ch]
                sblk = jnp.where(valid, sc[:, ch * nkeys:(ch + 1) * nkeys], NEG_INF)
                m = jnp.maximum(jnp.max(sblk, axis=-1, keepdims=True), sink)
                p = jnp.exp(sblk - m)
                den = jnp.sum(p, axis=-1, keepdims=True) + jnp.exp(sink - m)
                prob_cols.append(p * (1.0 / den))
            probs = jnp.concatenate(prob_cols, axis=1).astype(BF16)
            for j, kv in enumerate(range(g0, g0 + group)):
                out = _bdot(probs[2 * j * qb:2 * (j + 1) * qb], spread(v_all, kv))
                for rh in range(2):
                    o_scr[rows, win(2 * kv + rh)] = out[rh * qb:(rh + 1) * qb]
        nk_ref[s] = jnp.concatenate([prev_k[qb:], cur_k], axis=0)
        nv_ref[s] = jnp.concatenate([prev_v[qb:], cur_v], axis=0)
        return c

    lax.fori_loop(0, bs, seq_step, 0, unroll=min(bs, 4))
    o_ref[...] = o_scr[...].astype(BF16)


def _attn_call(z, k_past, v_past, qn, kn, cos_t, s1_t, s2_t, sinks, layer, qcol0, kvcol0,
               *, nseq, seqlen, row0, bs, qb, pos0):
    A = N_KV_HEADS * GQA_GROUP * HEAD_DIM
    KV = N_KV_HEADS * HEAD_DIM
    nb = seqlen // qb
    rb0 = row0 // (bs * qb)
    has_past = k_past is not None
    R = bs * qb
    if has_past:
        assert nb == 1 and qb < WINDOW
        body = functools.partial(_attn_seq_kernel, bs=bs, qb=qb, pos0=pos0, layer=layer)
        scratch = [pltpu.VMEM((R, A), F32), pltpu.VMEM((R, KV), F32), pltpu.VMEM((R, A), F32)]
    else:
        assert bs == 1 and qb == WINDOW
        body = functools.partial(_attn_stream_kernel, qb=qb, nb=nb, pos0=pos0, layer=layer)
        scratch = [pltpu.VMEM((2 * WINDOW, KV), F32), pltpu.VMEM((2 * WINDOW, KV), F32),
                   pltpu.VMEM((KV, 2 * WINDOW), F32),
                   pltpu.VMEM((N_KV_HEADS, 4 * WINDOW, 2 * qb), F32),
                   pltpu.VMEM((N_KV_HEADS, 4 * WINDOW, 2 * qb), BF16)]
    in_specs = [
        pl.BlockSpec((bs * qb, A), lambda b, n: (rb0 + b * nb + n, qcol0 // A)),
        pl.BlockSpec((bs * qb, 2 * KV), lambda b, n: (rb0 + b * nb + n, kvcol0 // (2 * KV))),
    ]
    args = [z, z]
    if has_past:
        past_spec = pl.BlockSpec((None, bs, WINDOW, KV), lambda b, n: (layer, b, 0, 0))
        in_specs += [past_spec, past_spec]
        args += [k_past, v_past]
    gain = pl.BlockSpec((None, 1, LANES), lambda b, n: (layer, 0, 0))
    tab = pl.BlockSpec((R, LANES), lambda b, n: (n, 0))
    in_specs += [gain, gain, tab, tab, tab, pl.BlockSpec(memory_space=pltpu.SMEM)]
    args += [qn, kn, cos_t, s1_t, s2_t, sinks]
    state = pl.BlockSpec((bs, WINDOW, KV), lambda b, n: (b, 0, 0))
    return pl.pallas_call(
        body,
        grid=(nseq // bs, nb),
        in_specs=in_specs,
        out_specs=[pl.BlockSpec((bs * qb, A), lambda b, n: (b * nb + n, 0)), state, state],
        out_shape=[
            jax.ShapeDtypeStruct((nseq * seqlen, A), BF16),
            jax.ShapeDtypeStruct((nseq, WINDOW, KV), F32),
            jax.ShapeDtypeStruct((nseq, WINDOW, KV), F32),
        ],
        scratch_shapes=scratch,
        compiler_params=_params(("arbitrary", "arbitrary"), 56),
        name="attn",
    )(*args)


def _rope_tables(pos, nrep):
    half = ROT_DIM // 2
    inv = ROPE_THETA ** (-jnp.arange(half, dtype=F32) / half)
    ang = pos.astype(F32)[:, None] * inv[None, :]
    cos, sin = jnp.cos(ang), jnp.sin(ang)
    T = pos.shape[0]
    ones = jnp.ones((T, HEAD_DIM - ROT_DIM), F32)
    zeros = jnp.zeros((T, HEAD_DIM - ROT_DIM), F32)
    zh = jnp.zeros((T, half), F32)
    cos_t = jnp.concatenate([cos, cos, ones], axis=1)
    s1_t = jnp.concatenate([-sin, zh, zeros], axis=1)
    s2_t = jnp.concatenate([zh, sin, zeros], axis=1)
    rep = LANES // HEAD_DIM
    return tuple(jnp.tile(t, (nrep, rep)) for t in (cos_t, s1_t, s2_t))


def kernel(x_prompt, x_sample, state_pool, cache_k_win, cache_v_win, state_conv, state_rglru, norm_ffa, ffa_w_gu, ffa_w_down, norm_mix, w_in, pool_w, pool_scale, q_norm, k_norm, attn_sinks, conv_w, conv_b, lru_gate_a_w, lru_gate_a_b, lru_gate_x_w, lru_gate_x_b, lru_lambda, w_branch_pool, w_branch_attn, w_branch_lru, w_out, norm_ffb, ffb_w_gu, ffb_w_down):
    Bp, Sp, D = x_prompt.shape
    Bs, Ss, _ = x_sample.shape
    L = norm_ffa.shape[0]
    Tp, Ts = Bp * Sp, Bs * Ss
    pool_c = pool_scale.shape[1]
    attn_c = N_KV_HEADS * GQA_GROUP * HEAD_DIM
    kv_c = N_KV_HEADS * HEAD_DIM
    lru_c = conv_w.shape[2]
    q0 = pool_c
    kv0 = q0 + attn_c
    xl0 = kv0 + 2 * kv_c
    gl0 = xl0 + lru_c
    gate0 = gl0 + lru_c

    tok = dict(ntiles=TOKEN_TILES)
    xp = x_prompt.reshape(Tp, D)
    xs = x_sample.reshape(Ts, D)

    vec3 = lambda a: a.reshape(L, 1, a.shape[-1])
    norm_ffa3, norm_mix3, norm_ffb3 = vec3(norm_ffa), vec3(norm_mix), vec3(norm_ffb)
    pool_scale3 = vec3(pool_scale)
    conv_b3, ba3, bx3, lam3 = vec3(conv_b), vec3(lru_gate_a_b), vec3(lru_gate_x_b), vec3(lru_lambda)
    qn3 = vec3(jnp.tile(q_norm, (1, LANES // HEAD_DIM)))
    kn3 = vec3(jnp.tile(k_norm, (1, LANES // HEAD_DIM)))
    pool_past = jnp.pad(state_pool, ((0, 0), (0, 0), (POOL_HIST - POOL_KEEP, 0), (0, 0)))
    conv_past = jnp.pad(state_conv, ((0, 0), (0, 0), (CONV_HIST - (CONV_WIDTH - 1), 0), (0, 0)))
    h0 = state_rglru.reshape(L, Bs, 1, lru_c)
    k_past = cache_k_win.reshape(L, Bs, WINDOW, kv_c)
    v_past = cache_v_win.reshape(L, Bs, WINDOW, kv_c)
    rope_p = _rope_tables(jnp.arange(Sp), 1)
    rope_s = _rope_tables(PAST_LEN + jnp.arange(Ss), Bs)

    prompt = dict(nseq=Bp, seqlen=Sp, row0=0)
    sample = dict(nseq=Bs, seqlen=Ss, row0=0)
    st_p = ([], [], [], [], [])
    st_s = ([], [], [], [], [])
    for l in range(L):
        xp, xs = _ffn_call(xp, xs, norm_ffa3, ffa_w_gu, ffa_w_down, l, tf=512, **tok)
        zp, zs, xn = _inproj_call(xp, xs, norm_mix3, w_in, l, gate0, sub=2, tn=512, **tok)

        pool_p, np_p = _pool_call(zp, None, pool_w, pool_scale3, l, bs=1, tt=512, pos0=0, **prompt)
        pool_s, np_s = _pool_call(zs, pool_past, pool_w, pool_scale3, l, bs=Bs, tt=Ss, pos0=PAST_LEN, **sample)

        att_p, nk_p, nv_p = _attn_call(zp, None, None, qn3, kn3, *rope_p, attn_sinks, l, q0, kv0,
                                       bs=1, qb=WINDOW, pos0=0, **prompt)
        att_s, nk_s, nv_s = _attn_call(zs, k_past, v_past, qn3, kn3, *rope_s, attn_sinks, l, q0, kv0,
                                       bs=Bs, qb=Ss, pos0=PAST_LEN, **sample)

        lru_args = (conv_w, conv_b3, lru_gate_a_w, ba3, lru_gate_x_w, bx3, lam3, l, xl0, gl0)
        lru_p, nc_p, nh_p = _lru_call(zp, None, None, *lru_args, bs=1, tt=512, cc=512, **prompt)
        lru_s, nc_s, nh_s = _lru_call(zs, conv_past, h0, *lru_args, bs=Bs, tt=Ss, cc=512, **sample)

        m = _merge_call(xn, w_in, gate0, ((pool_p, pool_s), (att_p, att_s), (lru_p, lru_s)),
                        (w_branch_pool, w_branch_attn, w_branch_lru), l, tc=256, **tok)
        xp, xs = _outproj_call(xp, xs, m, w_out, l, sub=2, tn=512, **tok)
        xp, xs = _ffn_call(xp, xs, norm_ffb3, ffb_w_gu, ffb_w_down, l, tf=512, **tok)

        for lst, val in zip(st_p, (np_p, nk_p, nv_p, nc_p, nh_p)):
            lst.append(val)
        for lst, val in zip(st_s, (np_s, nk_s, nv_s, nc_s, nh_s)):
            lst.append(val)

    def states(st, nseq):
        pool = jnp.stack(st[0])[:, :, POOL_HIST - POOL_KEEP:, :]
        k = jnp.stack(st[1]).reshape(L, nseq, WINDOW, N_KV_HEADS, HEAD_DIM)
        v = jnp.stack(st[2]).reshape(L, nseq, WINDOW, N_KV_HEADS, HEAD_DIM)
        conv = jnp.stack(st[3])[:, :, CONV_HIST - (CONV_WIDTH - 1):, :]
        h = jnp.stack(st[4]).reshape(L, nseq, lru_c)
        return pool, k, v, conv, h

    pool_p, k_p, v_p, conv_p, h_p = states(st_p, Bp)
    pool_s, k_s, v_s, conv_s, h_s = states(st_s, Bs)
    y_p = xp.reshape(Bp, Sp, D)
    y_s = xs.reshape(Bs, Ss, D)
    return (y_p, y_s, pool_p, pool_s, k_p, k_s, v_p, v_s, conv_p, conv_s, h_p, h_s)
```

```python
import functools

import jax
import jax.numpy as jnp
from jax import lax
from jax.experimental import pallas as pl
from jax.experimental.pallas import tpu as pltpu

F32 = jnp.float32
BF16 = jnp.bfloat16

RMS_EPS = 1e-6
NEG_INF = -1e30
FFN_RES_WEIGHT = 0.5
POOL_WINDOWS = (2, 4, 8, 16)
POOL_KEEP = max(POOL_WINDOWS) - 1
POOL_HIST = 16
HEAD_DIM = 64
N_KV_HEADS = 4
GQA_GROUP = 4
WINDOW = 128
ROT_DIM = HEAD_DIM // 4
ROPE_THETA = 500000.0
CONV_WIDTH = 4
CONV_HIST = 8
LRU_C = 8.0
LRU_BLOCK_DIM = 64
PAST_LEN = 16384
LANES = 128
SUBLANES = 8
SOFTMAX_ROWS = 256
TOKEN_TILES = 8
MIB = 1024 * 1024


def _bdot(a, b):
    return jnp.dot(a, b, preferred_element_type=F32)


def _bdot_nt(a, b):
    return lax.dot_general(a, b, (((1,), (1,)), ((), ())), preferred_element_type=F32)


def _rms_bf16(x, g):
    ms = jnp.mean(x * x, axis=-1, keepdims=True)
    return ((x * lax.rsqrt(ms + RMS_EPS)) * g).astype(BF16)


def _sigmoid(x):
    return 0.5 * jnp.tanh(0.5 * x) + 0.5


def _params(sem, vmem_mib):
    return pltpu.CompilerParams(dimension_semantics=sem, vmem_limit_bytes=vmem_mib * MIB)


def _ffn_kernel(xp_hbm, xs_ref, g_ref, wg_ref, wu_ref, wd_ref, op_ref, os_ref, xn_ref, xbuf_ref, sem):
    tp = xbuf_ref.shape[0]
    tile_x = _TilePrefetch(xp_hbm, xbuf_ref, sem)

    @pl.when(pl.program_id(1) == 0)
    def _():
        tile_x.wait_current()
        xp = xbuf_ref[...]
        xs = xs_ref[...]
        xn_ref[0:tp, :] = _rms_bf16(xp, g_ref[...])
        xn_ref[tp:, :] = _rms_bf16(xs, g_ref[...])
        op_ref[...] = xp
        os_ref[...] = xs

    xn = xn_ref[...]
    g = _bdot(xn, wg_ref[...].astype(BF16))
    u = _bdot(xn, wu_ref[...].astype(BF16))
    h = (FFN_RES_WEIGHT * ((g * _sigmoid(g)) * u)).astype(BF16)
    res = _bdot(h, wd_ref[...].astype(BF16))
    op_ref[...] += res[:tp]
    os_ref[...] += res[tp:]
    tile_x.start_next()


class _TilePrefetch:
    def __init__(self, hbm_ref, buf_ref, sem):
        self.hbm_ref, self.buf_ref, self.sem = hbm_ref, buf_ref, sem
        self.rows = buf_ref.shape[0]

    def _copy(self, tile):
        start = pl.multiple_of(tile * self.rows, SUBLANES)
        return pltpu.make_async_copy(self.hbm_ref.at[pl.ds(start, self.rows), :], self.buf_ref, self.sem)

    def wait_current(self):
        i = pl.program_id(0)

        @pl.when(i == 0)
        def _():
            self._copy(i).start()

        self._copy(i).wait()

    def start_next(self):
        i = pl.program_id(0)

        @pl.when((pl.program_id(1) == pl.num_programs(1) - 1) & (i + 1 < pl.num_programs(0)))
        def _():
            self._copy(i + 1).start()


def _ffn_call(xp, xs, norm, w_gu, w_down, layer, *, ntiles, tf):
    Tp, D = xp.shape
    Ts = xs.shape[0]
    tp, ts = Tp // ntiles, Ts // ntiles
    dff = w_down.shape[1]
    nj = dff // tf
    rows = lambda t: pl.BlockSpec((t, D), lambda i, j: (i, 0))
    return pl.pallas_call(
        _ffn_kernel,
        grid=(ntiles, nj),
        in_specs=[
            pl.BlockSpec(memory_space=pl.ANY), rows(ts),
            pl.BlockSpec((None, 1, D), lambda i, j: (layer, 0, 0)),
            pl.BlockSpec((None, D, tf), lambda i, j: (layer, 0, j)),
            pl.BlockSpec((None, D, tf), lambda i, j: (layer, 0, j + nj)),
            pl.BlockSpec((None, tf, D), lambda i, j: (layer, j, 0)),
        ],
        out_specs=[rows(tp), rows(ts)],
        out_shape=[jax.ShapeDtypeStruct((Tp, D), F32), jax.ShapeDtypeStruct((Ts, D), F32)],
        scratch_shapes=[pltpu.VMEM((tp + ts, D), BF16), pltpu.VMEM((tp, D), F32),
                        pltpu.SemaphoreType.DMA(())],
        compiler_params=_params(("arbitrary", "arbitrary"), 60),
        name="ffn",
    )(xp, xs, norm, w_gu, w_gu, w_down)


def _tile_rows(sub, tp, ts):
    tm = tp + ts
    return [((slice(s * tp, (s + 1) * tp), slice(s * tm, s * tm + tp)),
             (slice(s * ts, (s + 1) * ts), slice(s * tm + tp, (s + 1) * tm))) for s in range(sub)]


def _inproj_kernel(xp_hbm, xs_ref, g_ref, w_ref, zp_ref, zs_ref, xn_ref, xbuf_ref, sem, *, sub):
    tiles = _tile_rows(sub, xbuf_ref.shape[0] // sub, xs_ref.shape[0] // sub)
    tile_x = _TilePrefetch(xp_hbm, xbuf_ref, sem)

    @pl.when(pl.program_id(1) == 0)
    def _():
        tile_x.wait_current()
        for (p_rows, p_int), (s_rows, s_int) in tiles:
            xn_ref[p_int, :] = _rms_bf16(xbuf_ref[p_rows, :], g_ref[...])
            xn_ref[s_int, :] = _rms_bf16(xs_ref[s_rows, :], g_ref[...])

    res = _bdot(xn_ref[...], w_ref[...].astype(BF16))
    for (p_rows, p_int), (s_rows, s_int) in tiles:
        zp_ref[p_rows, :] = res[p_int]
        zs_ref[s_rows, :] = res[s_int]
    tile_x.start_next()


def _inproj_call(xp, xs, norm, w_in, layer, ncols, *, ntiles, sub, tn):
    Tp, D = xp.shape
    Ts = xs.shape[0]
    ntiles = ntiles // sub
    tp, ts = Tp // ntiles, Ts // ntiles
    rows = lambda t: pl.BlockSpec((t, D), lambda i, j: (i, 0))
    cols = lambda t: pl.BlockSpec((t, tn), lambda i, j: (i, j))
    return pl.pallas_call(
        functools.partial(_inproj_kernel, sub=sub),
        grid=(ntiles, ncols // tn),
        in_specs=[
            pl.BlockSpec(memory_space=pl.ANY), rows(ts),
            pl.BlockSpec((None, 1, D), lambda i, j: (layer, 0, 0)),
            pl.BlockSpec((None, D, tn), lambda i, j: (layer, 0, j)),
        ],
        out_specs=[cols(tp), cols(ts), rows(tp + ts)],
        out_shape=[jax.ShapeDtypeStruct((Tp, ncols), F32), jax.ShapeDtypeStruct((Ts, ncols), F32),
                   jax.ShapeDtypeStruct((Tp + Ts, D), BF16)],
        scratch_shapes=[pltpu.VMEM((tp, D), F32), pltpu.SemaphoreType.DMA(())],
        compiler_params=_params(("arbitrary", "arbitrary"), 60),
        name="inproj",
    )(xp, xs, norm, w_in)


def _merge_kernel(xn_ref, wg0_ref, wg1_ref, wg2_ref, b0p_ref, b0s_ref, b1p_ref, b1s_ref, b2p_ref, b2s_ref,
                  w0_ref, w1_ref, w2_ref, o_ref, lhs_ref):
    tp = b0p_ref.shape[0]

    @pl.when(pl.program_id(1) == 0)
    def _():
        for b, (p_ref, s_ref) in enumerate(((b0p_ref, b0s_ref), (b1p_ref, b1s_ref), (b2p_ref, b2s_ref))):
            lhs_ref[b, 0:tp, :] = p_ref[...]
            lhs_ref[b, tp:, :] = s_ref[...]

    xn = xn_ref[...]

    def term(b, wg_ref, w_ref):
        gate = _sigmoid(_bdot(xn, wg_ref[...].astype(BF16)))
        return gate * _bdot(lhs_ref[b], w_ref[...].astype(BF16))

    m = term(0, wg0_ref, w0_ref) + term(1, wg1_ref, w1_ref) + term(2, wg2_ref, w2_ref)
    o_ref[...] = m.astype(BF16)


def _merge_call(xn, w_in, gate_col0, branches, branch_ws, layer, *, ntiles, tc):
    T, D = xn.shape
    tm = T // ntiles
    W = branches[0][0].shape[1]
    tp, ts = branches[0][0].shape[0] // ntiles, branches[0][1].shape[0] // ntiles
    g0 = gate_col0 // tc
    gstep = D // tc
    gate_w = lambda b: pl.BlockSpec((None, D, tc), lambda i, c: (layer, 0, g0 + b * gstep + c))
    rows = lambda t: pl.BlockSpec((t, W), lambda i, c: (i, 0))
    w_spec = pl.BlockSpec((None, W, tc), lambda i, c: (layer, 0, c))
    return pl.pallas_call(
        _merge_kernel,
        grid=(ntiles, D // tc),
        in_specs=[pl.BlockSpec((tm, D), lambda i, c: (i, 0)), gate_w(0), gate_w(1), gate_w(2)]
                 + [rows(tp), rows(ts)] * 3 + [w_spec] * 3,
        out_specs=pl.BlockSpec((tm, tc), lambda i, c: (i, c)),
        out_shape=jax.ShapeDtypeStruct((T, D), BF16),
        scratch_shapes=[pltpu.VMEM((3, tm, W), BF16)],
        compiler_params=_params(("arbitrary", "arbitrary"), 58),
        name="merge",
    )(xn, w_in, w_in, w_in, *[a for pair in branches for a in pair], *branch_ws)


def _outproj_kernel(xp_ref, xs_ref, m_ref, w_ref, op_ref, os_ref, *, sub):
    res = _bdot(m_ref[...], w_ref[...].astype(BF16))
    for (p_rows, p_int), (s_rows, s_int) in _tile_rows(sub, xp_ref.shape[0] // sub, xs_ref.shape[0] // sub):
        op_ref[p_rows, :] = xp_ref[p_rows, :] + res[p_int]
        os_ref[s_rows, :] = xs_ref[s_rows, :] + res[s_int]


def _outproj_call(xp, xs, m, w_out, layer, *, ntiles, sub, tn):
    Tp, D = xp.shape
    Ts = xs.shape[0]
    ntiles = ntiles // sub
    tp, ts = Tp // ntiles, Ts // ntiles
    cols = lambda t: pl.BlockSpec((t, tn), lambda i, c: (i, c))
    return pl.pallas_call(
        functools.partial(_outproj_kernel, sub=sub),
        grid=(ntiles, D // tn),
        in_specs=[
            cols(tp), cols(ts),
            pl.BlockSpec((tp + ts, D), lambda i, c: (i, 0)),
            pl.BlockSpec((None, D, tn), lambda i, c: (layer, 0, c)),
        ],
        out_specs=[cols(tp), cols(ts)],
        out_shape=[jax.ShapeDtypeStruct((Tp, D), F32), jax.ShapeDtypeStruct((Ts, D), F32)],
        compiler_params=_params(("arbitrary", "arbitrary"), 56),
        name="outproj",
    )(xp, xs, m, w_out)


def _pool_kernel(*refs, bs, tt, nt, pos0, has_past):
    if has_past:
        u_ref, past_ref, w_ref, s_ref, o_ref, np_ref, e_ref = refs
    else:
        u_ref, w_ref, s_ref, o_ref, np_ref, e_ref = refs
    t = pl.program_id(1)
    C = e_ref.shape[-1]
    H = POOL_HIST

    @pl.when(t == 0)
    def _():
        if has_past:
            e_ref[:, 0:H, :] = past_ref[...]
        else:
            e_ref[:, 0:H, :] = jnp.zeros((bs, H, C), F32)

    e_ref[:, H:H + tt, :] = u_ref[...].reshape(bs, tt, C)
    posp1 = lax.broadcasted_iota(jnp.int32, (1, tt, 1), 1) + (t * tt + pos0 + 1)
    gd = C // len(POOL_WINDOWS)
    for g, w in enumerate(POOL_WINDOWS):
        sl = slice(g * gd, (g + 1) * gd)
        e = e_ref[:, :, sl]
        p = e
        s = 1
        while s < w:
            p = p + pltpu.roll(p, s, 1)
            s *= 2
        inv_cnt = 1.0 / jnp.minimum(posp1, w).astype(F32)
        d = (p[:, H:, :] * inv_cnt - e[:, H:, :]).reshape(bs * tt, gd).astype(BF16)
        out = _bdot(d, w_ref[g].astype(BF16)) * s_ref[:, sl]
        o_ref[:, sl] = out.astype(BF16)

    carry = e_ref[:, tt:tt + H, :]
    e_ref[:, 0:H, :] = carry

    @pl.when(t == nt - 1)
    def _():
        np_ref[...] = carry


def _pool_call(z, past, pool_w, pool_scale, layer, *, nseq, seqlen, row0, bs, tt, pos0):
    C = pool_w.shape[1] * pool_w.shape[2]
    nt = seqlen // tt
    rb0 = row0 // (bs * tt)
    has_past = past is not None
    in_specs = [pl.BlockSpec((bs * tt, C), lambda b, t: (rb0 + b * nt + t, 0))]
    args = [z]
    if has_past:
        in_specs.append(pl.BlockSpec((None, bs, POOL_HIST, C), lambda b, t: (layer, b, 0, 0)))
        args.append(past)
    in_specs += [
        pl.BlockSpec((None,) + pool_w.shape[1:], lambda b, t: (layer, 0, 0, 0)),
        pl.BlockSpec((None, 1, C), lambda b, t: (layer, 0, 0)),
    ]
    args += [pool_w, pool_scale]
    return pl.pallas_call(
        functools.partial(_pool_kernel, bs=bs, tt=tt, nt=nt, pos0=pos0, has_past=has_past),
        grid=(nseq // bs, nt),
        in_specs=in_specs,
        out_specs=[
            pl.BlockSpec((bs * tt, C), lambda b, t: (b * nt + t, 0)),
            pl.BlockSpec((bs, POOL_HIST, C), lambda b, t: (b, 0, 0)),
        ],
        out_shape=[
            jax.ShapeDtypeStruct((nseq * seqlen, C), BF16),
            jax.ShapeDtypeStruct((nseq, POOL_HIST, C), F32),
        ],
        scratch_shapes=[pltpu.VMEM((bs, POOL_HIST + tt, C), F32)],
        compiler_params=_params(("arbitrary", "arbitrary"), 48),
        name="pool",
    )(*args)


def _lru_kernel(*refs, bs, tt, nt, has_past):
    if has_past:
        (x_ref, g_ref, cp_ref, h0_ref, cw_ref, cb_ref, wa_ref, ba_ref, wx_ref, bx_ref, lam_ref,
         y_ref, nc_ref, nh_ref, xe_ref, h_ref, a_ref, b_ref, bd_ref) = refs
    else:
        (x_ref, g_ref, cw_ref, cb_ref, wa_ref, ba_ref, wx_ref, bx_ref, lam_ref,
         y_ref, nc_ref, nh_ref, xe_ref, h_ref, a_ref, b_ref, bd_ref) = refs
    t = pl.program_id(2)
    Cc = xe_ref.shape[-1]
    R = bs * tt
    ncol = Cc // LANES
    HC = CONV_HIST
    blk = LRU_BLOCK_DIM

    @pl.when(t == 0)
    def _():
        if has_past:
            xe_ref[:, 0:HC, :] = cp_ref[...]
            h_ref[...] = jnp.broadcast_to(h0_ref[...], (bs, SUBLANES, Cc))
        else:
            xe_ref[:, 0:HC, :] = jnp.zeros((bs, HC, Cc), F32)
            h_ref[...] = jnp.zeros((bs, SUBLANES, Cc), F32)
        rep = (lax.broadcasted_iota(jnp.int32, (blk, LANES), 0)
               == (lax.broadcasted_iota(jnp.int32, (blk, LANES), 1) & (blk - 1))).astype(BF16)
        diag = ((lax.broadcasted_iota(jnp.int32, (LANES, LANES), 0) >= blk)
                == (lax.broadcasted_iota(jnp.int32, (LANES, LANES), 1) >= blk))
        for p in range(ncol):
            for k, w_ref in enumerate((wa_ref, wx_ref)):
                w2 = w_ref[2 * p:2 * p + 2].reshape(2 * blk, blk).astype(BF16)
                full = _bdot(w2, rep)
                bd_ref[p, :, k * LANES:(k + 1) * LANES] = jnp.where(diag, 0.5 * full, 0.0).astype(BF16)

    xe_ref[:, HC:HC + tt, :] = x_ref[...].reshape(bs, tt, Cc)
    xe = xe_ref[...]
    cw = cw_ref[...]
    xc = cb_ref[...]
    for j in range(CONV_WIDTH):
        shift = CONV_WIDTH - 1 - j
        xs = pltpu.roll(xe, shift, 1) if shift else xe
        xc = xc + xs[:, HC:, :] * cw[j:j + 1]
    xc = xc.reshape(R, Cc)

    row8 = lax.broadcasted_iota(jnp.int32, (1, SUBLANES, 1), 1)
    for p in range(ncol):
        col = slice(p * LANES, (p + 1) * LANES)
        xcp = xc[:, col]
        pre = _bdot(xcp.astype(BF16), bd_ref[p])
        tr = jnp.tanh(pre[:, :LANES] + 0.5 * ba_ref[:, col])
        i = 0.5 * jnp.tanh(pre[:, LANES:] + 0.5 * bx_ref[:, col]) + 0.5
        nl = -lam_ref[:, col]
        sp = jnp.maximum(nl, 0.0) + jnp.log1p(jnp.exp(-jnp.abs(nl)))
        la = (tr + 1.0) * ((-0.5 * LRU_C) * sp)
        a = jnp.exp(la)
        bv = jnp.sqrt(jnp.tanh(-la) * (a * a + 1.0)) * (i * xcp)
        a = a.reshape(R // SUBLANES, SUBLANES, LANES)
        bv = bv.reshape(R // SUBLANES, SUBLANES, LANES)
        for s in (1, 2, 4):
            keep = row8 >= s
            a_sh = pltpu.roll(a, s, 1)
            b_sh = pltpu.roll(bv, s, 1)
            bv = jnp.where(keep, a * b_sh + bv, bv)
            a = jnp.where(keep, a * a_sh, a)
        a_ref[:, :, col] = a.reshape(bs, tt, LANES)
        b_ref[:, :, col] = bv.reshape(bs, tt, LANES)

    def carry_step(k, h):
        o = pl.multiple_of(k * SUBLANES, SUBLANES)
        hb = a_ref[:, pl.ds(o, SUBLANES), :] * h + b_ref[:, pl.ds(o, SUBLANES), :]
        b_ref[:, pl.ds(o, SUBLANES), :] = hb
        return jnp.broadcast_to(hb[:, SUBLANES - 1:SUBLANES, :], hb.shape)

    ngroups = tt // SUBLANES
    h_ref[...] = lax.fori_loop(0, ngroups, carry_step, h_ref[...], unroll=min(ngroups, SUBLANES))
    for p in range(ncol):
        col = slice(p * LANES, (p + 1) * LANES)
        hs = b_ref[:, :, col].reshape(R, LANES)
        y_ref[:, col] = (hs * jax.nn.gelu(g_ref[:, col], approximate=True)).astype(BF16)

    tail = xe_ref[:, tt:tt + HC, :]
    xe_ref[:, 0:HC, :] = tail

    @pl.when(t == nt - 1)
    def _():
        nc_ref[...] = tail
        nh_ref[...] = h_ref[:, 0:1, :]


def _lru_call(z, conv_past, h0, conv_w, conv_b, wa, ba, wx, bx, lam, layer, xcol0, gcol0,
              *, nseq, seqlen, row0, bs, tt, cc):
    C = conv_w.shape[2]
    nt = seqlen // tt
    nc = C // cc
    rb0 = row0 // (bs * tt)
    xb0 = xcol0 // cc
    gb0 = gcol0 // cc
    nblk = cc // LRU_BLOCK_DIM
    has_past = conv_past is not None
    in_specs = [
        pl.BlockSpec((bs * tt, cc), lambda b, c, t: (rb0 + b * nt + t, xb0 + c)),
        pl.BlockSpec((bs * tt, cc), lambda b, c, t: (rb0 + b * nt + t, gb0 + c)),
    ]
    args = [z, z]
    if has_past:
        in_specs += [
            pl.BlockSpec((None, bs, CONV_HIST, cc), lambda b, c, t: (layer, b, 0, c)),
            pl.BlockSpec((None, bs, 1, cc), lambda b, c, t: (layer, b, 0, c)),
        ]
        args += [conv_past, h0]
    vec = pl.BlockSpec((None, 1, cc), lambda b, c, t: (layer, 0, c))
    gw = pl.BlockSpec((None, nblk, LRU_BLOCK_DIM, LRU_BLOCK_DIM), lambda b, c, t: (layer, c, 0, 0))
    in_specs += [pl.BlockSpec((None, CONV_WIDTH, cc), lambda b, c, t: (layer, 0, c)), vec, gw, vec, gw, vec, vec]
    args += [conv_w, conv_b, wa, ba, wx, bx, lam]
    return pl.pallas_call(
        functools.partial(_lru_kernel, bs=bs, tt=tt, nt=nt, has_past=has_past),
        grid=(nseq // bs, nc, nt),
        in_specs=in_specs,
        out_specs=[
            pl.BlockSpec((bs * tt, cc), lambda b, c, t: (b * nt + t, c)),
            pl.BlockSpec((bs, CONV_HIST, cc), lambda b, c, t: (b, 0, c)),
            pl.BlockSpec((bs, 1, cc), lambda b, c, t: (b, 0, c)),
        ],
        out_shape=[
            jax.ShapeDtypeStruct((nseq * seqlen, C), BF16),
            jax.ShapeDtypeStruct((nseq, CONV_HIST, C), F32),
            jax.ShapeDtypeStruct((nseq, 1, C), F32),
        ],
        scratch_shapes=[
            pltpu.VMEM((bs, CONV_HIST + tt, cc), F32),
            pltpu.VMEM((bs, SUBLANES, cc), F32),
            pltpu.VMEM((bs, tt, cc), F32),
            pltpu.VMEM((bs, tt, cc), F32),
            pltpu.VMEM((cc // LANES, LANES, 2 * LANES), BF16),
        ],
        compiler_params=_params(("arbitrary", "arbitrary", "arbitrary"), 48),
        name="lru",
    )(*args)


def _attn_stream_kernel(q_ref, kv_ref, qn_ref, kn_ref, cos_ref, s1_ref, s2_ref, sink_ref,
                        o_ref, nk_ref, nv_ref, kb_ref, vb_ref, vt_ref, sc_ref, p_ref,
                        *, qb, nsub, nb, pos0, layer):
    n = pl.program_id(1)
    KV = kb_ref.shape[-1]
    nkeys = 2 * WINDOW
    half = HEAD_DIM
    lane = lax.broadcasted_iota(jnp.int32, (1, LANES), 1)
    seg_ones = ((lax.broadcasted_iota(jnp.int32, (LANES, LANES), 0) >= half)
                == (lax.broadcasted_iota(jnp.int32, (LANES, LANES), 1) >= half)).astype(BF16)
    win = lambda w: slice(w * LANES, (w + 1) * LANES)

    def norm_rot(xw, gain, rows):
        sq = xw * xw
        hi = sq.astype(BF16)
        lo = (sq - hi.astype(F32)).astype(BF16)
        ss = _bdot(hi, seg_ones) + _bdot(lo, seg_ones)
        y = (xw * lax.rsqrt(ss * (1.0 / HEAD_DIM) + RMS_EPS)) * gain
        return (y * cos_ref[rows, :] + pltpu.roll(y, LANES - ROT_DIM // 2, 1) * s1_ref[rows, :]
                + pltpu.roll(y, ROT_DIM // 2, 1) * s2_ref[rows, :])

    si = lax.broadcasted_iota(jnp.int32, (nkeys, 2 * qb), 0)
    ci = lax.broadcasted_iota(jnp.int32, (nkeys, 2 * qb), 1)
    qi = ci & (qb - 1)
    in_band = (si >= qi) & (si <= qi + WINDOW)
    first_half = lax.broadcasted_iota(jnp.int32, (1, 2 * qb), 1) < qb

    @pl.when(n == 0)
    def _():
        kb_ref[0:WINDOW, :] = jnp.zeros((WINDOW, KV), F32)
        vb_ref[0:WINDOW, :] = jnp.zeros((WINDOW, KV), F32)
        vt_ref[:, 0:WINDOW] = jnp.zeros((KV, WINDOW), F32)

    kvx = kv_ref[...]
    for w in range(KV // LANES):
        kb_ref[WINDOW:, win(w)] = norm_rot(kvx[:, win(w)], kn_ref[...], slice(None))
    vb_ref[WINDOW:, :] = kvx[:, KV:]
    vt_ref[:, WINDOW:] = kvx[:, KV:].T
    q_gain = qn_ref[...] * (HEAD_DIM ** -0.5)
    for sb in range(nsub):
        qrows = slice(sb * qb, (sb + 1) * qb)
        krows = slice(sb * qb, sb * qb + nkeys)
        for kv in range(N_KV_HEADS):
            w, off = divmod(kv, 2)
            own = jnp.where((lane >= half) == (off == 1), kb_ref[krows, win(w)], 0.0)
            moved = pltpu.roll(own, half, 1)
            kk = jnp.concatenate([moved, own] if off == 1 else [own, moved], axis=0).astype(BF16)
            qst = jnp.concatenate([norm_rot(q_ref[qrows, win(2 * kv + h)], q_gain, qrows) for h in range(2)],
                                  axis=0).astype(BF16)
            sc_ref[sb * N_KV_HEADS + kv] = _bdot_nt(kk, qst)
    for sb in range(nsub):
        kpos0 = pos0 + (n * nsub + sb) * qb - WINDOW
        valid = in_band & (si + kpos0 >= 0)
        for kv in range(N_KV_HEADS):
            for ch in range(2):
                rows = slice(ch * nkeys, (ch + 1) * nkeys)
                sink = jnp.where(first_half, sink_ref[layer, kv * GQA_GROUP + ch],
                                 sink_ref[layer, kv * GQA_GROUP + 2 + ch])
                sblk = jnp.where(valid, sc_ref[sb * N_KV_HEADS + kv, rows, :], NEG_INF)
                m = jnp.maximum(jnp.max(sblk, axis=0, keepdims=True), sink)
                p = jnp.exp(sblk - m)
                den = jnp.sum(p, axis=0, keepdims=True) + jnp.exp(sink - m)
                p_ref[sb * N_KV_HEADS + kv, rows, :] = (p * (1.0 / den)).astype(BF16)
    zeros_t = jnp.zeros((half, nkeys), F32)
    for sb in range(nsub):
        for kv in range(N_KV_HEADS):
            vth = vt_ref[kv * half:(kv + 1) * half, sb * qb:sb * qb + nkeys]
            vvt = jnp.concatenate([jnp.concatenate([vth, zeros_t], axis=1),
                                   jnp.concatenate([zeros_t, vth], axis=1)], axis=0).astype(BF16)
            out = _bdot(vvt, p_ref[sb * N_KV_HEADS + kv]).T
            for rh in range(2):
                o_ref[sb * qb:(sb + 1) * qb, win(2 * kv + rh)] = out[rh * qb:(rh + 1) * qb].astype(BF16)
    last = slice(nsub * WINDOW, (nsub + 1) * WINDOW)
    new_k = kb_ref[last, :]
    new_v = vb_ref[last, :]
    kb_ref[0:WINDOW, :] = new_k
    vb_ref[0:WINDOW, :] = new_v
    vt_ref[:, 0:WINDOW] = vt_ref[:, last]

    @pl.when(n == nb - 1)
    def _():
        nk_ref[0] = new_k
        nv_ref[0] = new_v


def _attn_seq_kernel(q_ref, kv_ref, kp_ref, vp_ref, qn_ref, kn_ref, cos_ref, s1_ref, s2_ref, sink_ref,
                     o_ref, nk_ref, nv_ref, q_scr, k_scr, o_scr, *, bs, qb, pos0, layer):
    KV = k_scr.shape[-1]
    nkeys = 2 * WINDOW
    half = HEAD_DIM
    lane = lax.broadcasted_iota(jnp.int32, (1, LANES), 1)
    seg_ones = ((lax.broadcasted_iota(jnp.int32, (LANES, LANES), 0) >= half)
                == (lax.broadcasted_iota(jnp.int32, (LANES, LANES), 1) >= half)).astype(BF16)
    R = q_scr.shape[0]
    nq = q_scr.shape[1] // LANES
    nk = KV // LANES
    win = lambda w: slice(w * LANES, (w + 1) * LANES)

    xs = jnp.concatenate([q_ref[:, win(w)] for w in range(nq)] + [kv_ref[:, win(w)] for w in range(nk)], axis=0)
    sq = xs * xs
    hi = sq.astype(BF16)
    lo = (sq - hi.astype(F32)).astype(BF16)
    ss = _bdot(hi, seg_ones) + _bdot(lo, seg_ones)
    y = (xs * lax.rsqrt(ss * (1.0 / HEAD_DIM) + RMS_EPS)).reshape(nq + nk, R, LANES)
    gains = jnp.concatenate([jnp.broadcast_to(qn_ref[...] * (HEAD_DIM ** -0.5), (nq, 1, LANES)),
                             jnp.broadcast_to(kn_ref[...], (nk, 1, LANES))], axis=0)
    y = y * gains
    y = (y * cos_ref[...] + pltpu.roll(y, LANES - ROT_DIM // 2, 2) * s1_ref[...]
         + pltpu.roll(y, ROT_DIM // 2, 2) * s2_ref[...])
    for w in range(nq):
        q_scr[:, win(w)] = y[w]
    for w in range(nk):
        k_scr[:, win(w)] = y[nq + w]

    group = max(1, min(N_KV_HEADS, SOFTMAX_ROWS // (2 * qb)))
    nrow = 2 * group * qb
    qi = lax.broadcasted_iota(jnp.int32, (nrow, nkeys), 0) & (qb - 1)
    si = lax.broadcasted_iota(jnp.int32, (nrow, nkeys), 1)
    valid = (si >= qi) & (si <= qi + WINDOW) & (si + (pos0 - WINDOW) >= 0)
    sink_cols = [
        [jnp.concatenate([jnp.full((qb, 1), sink_ref[layer, kv * GQA_GROUP + 2 * rh + ch], F32)
                          for kv in range(g0, g0 + group) for rh in range(2)], axis=0)
         for ch in range(2)]
        for g0 in range(0, N_KV_HEADS, group)]

    def seq_step(s, c):
        rows = pl.ds(pl.multiple_of(s * qb, qb), qb)
        prev_k = kp_ref[s]
        prev_v = vp_ref[s]
        cur_k = k_scr[rows, :]
        cur_v = kv_ref[rows, KV:]
        pad = jnp.zeros((WINDOW - qb, KV), F32)
        k_all = jnp.concatenate([prev_k, cur_k, pad], axis=0)
        v_all = jnp.concatenate([prev_v, cur_v, pad], axis=0)

        def spread(x, kv):
            w, off = divmod(kv, 2)
            lo = jnp.where((lane >= half) == (off == 1), x[:, win(w)], 0.0)
            if off == 1:
                lo = pltpu.roll(lo, half, 1)
            return jnp.concatenate([lo, pltpu.roll(lo, half, 1)], axis=0).astype(BF16)

        for gi, g0 in enumerate(range(0, N_KV_HEADS, group)):
            scores = []
            for kv in range(g0, g0 + group):
                qst = jnp.concatenate([q_scr[rows, win(2 * kv + h)] for h in range(2)], axis=0).astype(BF16)
                scores.append(_bdot_nt(qst, spread(k_all, kv)))
            sc = jnp.concatenate(scores, axis=0)
            prob_cols = []
            for ch in range(2):
                sink = sink_cols[gi][ch]
                sblk = jnp.where(valid, sc[:, ch * nkeys:(ch + 1) * nkeys], NEG_INF)
                m = jnp.maximum(jnp.max(sblk, axis=-1, keepdims=True), sink)
                p = jnp.exp(sblk - m)
                den = jnp.sum(p, axis=-1, keepdims=True) + jnp.exp(sink - m)
                prob_cols.append(p * (1.0 / den))
            probs = jnp.concatenate(prob_cols, axis=1).astype(BF16)
            for j, kv in enumerate(range(g0, g0 + group)):
                out = _bdot(probs[2 * j * qb:2 * (j + 1) * qb], spread(v_all, kv))
                for rh in range(2):
                    o_scr[rows, win(2 * kv + rh)] = out[rh * qb:(rh + 1) * qb]
        nk_ref[s] = jnp.concatenate([prev_k[qb:], cur_k], axis=0)
        nv_ref[s] = jnp.concatenate([prev_v[qb:], cur_v], axis=0)
        return c

    lax.fori_loop(0, bs, seq_step, 0, unroll=min(bs, 4))
    o_ref[...] = o_scr[...].astype(BF16)


def _attn_call(z, k_past, v_past, qn, kn, cos_t, s1_t, s2_t, sinks, layer, qcol0, kvcol0,
               *, nseq, seqlen, row0, bs, qb, pos0, nsub=1):
    A = N_KV_HEADS * GQA_GROUP * HEAD_DIM
    KV = N_KV_HEADS * HEAD_DIM
    R = bs * qb * nsub
    nb = seqlen // (qb * nsub)
    rb0 = row0 // R
    has_past = k_past is not None
    if has_past:
        assert nb == 1 and nsub == 1 and qb < WINDOW
        body = functools.partial(_attn_seq_kernel, bs=bs, qb=qb, pos0=pos0, layer=layer)
        scratch = [pltpu.VMEM((R, A), F32), pltpu.VMEM((R, KV), F32), pltpu.VMEM((R, A), F32)]
    else:
        assert bs == 1 and qb == WINDOW
        body = functools.partial(_attn_stream_kernel, qb=qb, nsub=nsub, nb=nb, pos0=pos0, layer=layer)
        held = (1 + nsub) * WINDOW
        scratch = [pltpu.VMEM((held, KV), F32), pltpu.VMEM((held, KV), F32), pltpu.VMEM((KV, held), F32),
                   pltpu.VMEM((nsub * N_KV_HEADS, 4 * WINDOW, 2 * qb), F32),
                   pltpu.VMEM((nsub * N_KV_HEADS, 4 * WINDOW, 2 * qb), BF16)]
    in_specs = [
        pl.BlockSpec((R, A), lambda b, n: (rb0 + b * nb + n, qcol0 // A)),
        pl.BlockSpec((R, 2 * KV), lambda b, n: (rb0 + b * nb + n, kvcol0 // (2 * KV))),
    ]
    args = [z, z]
    if has_past:
        past_spec = pl.BlockSpec((None, bs, WINDOW, KV), lambda b, n: (layer, b, 0, 0))
        in_specs += [past_spec, past_spec]
        args += [k_past, v_past]
    gain = pl.BlockSpec((None, 1, LANES), lambda b, n: (layer, 0, 0))
    tab = pl.BlockSpec((R, LANES), lambda b, n: (n, 0))
    in_specs += [gain, gain, tab, tab, tab, pl.BlockSpec(memory_space=pltpu.SMEM)]
    args += [qn, kn, cos_t, s1_t, s2_t, sinks]
    state = pl.BlockSpec((bs, WINDOW, KV), lambda b, n: (b, 0, 0))
    return pl.pallas_call(
        body,
        grid=(nseq // bs, nb),
        in_specs=in_specs,
        out_specs=[pl.BlockSpec((R, A), lambda b, n: (b * nb + n, 0)), state, state],
        out_shape=[
            jax.ShapeDtypeStruct((nseq * seqlen, A), BF16),
            jax.ShapeDtypeStruct((nseq, WINDOW, KV), F32),
            jax.ShapeDtypeStruct((nseq, WINDOW, KV), F32),
        ],
        scratch_shapes=scratch,
        compiler_params=_params(("arbitrary", "arbitrary"), 56),
        name="attn",
    )(*args)


def _rope_tables(pos, nrep):
    half = ROT_DIM // 2
    inv = ROPE_THETA ** (-jnp.arange(half, dtype=F32) / half)
    ang = pos.astype(F32)[:, None] * inv[None, :]
    cos, sin = jnp.cos(ang), jnp.sin(ang)
    T = pos.shape[0]
    ones = jnp.ones((T, HEAD_DIM - ROT_DIM), F32)
    zeros = jnp.zeros((T, HEAD_DIM - ROT_DIM), F32)
    zh = jnp.zeros((T, half), F32)
    cos_t = jnp.concatenate([cos, cos, ones], axis=1)
    s1_t = jnp.concatenate([-sin, zh, zeros], axis=1)
    s2_t = jnp.concatenate([zh, sin, zeros], axis=1)
    rep = LANES // HEAD_DIM
    return tuple(jnp.tile(t, (nrep, rep)) for t in (cos_t, s1_t, s2_t))


def kernel(x_prompt, x_sample, state_pool, cache_k_win, cache_v_win, state_conv, state_rglru, norm_ffa, ffa_w_gu, ffa_w_down, norm_mix, w_in, pool_w, pool_scale, q_norm, k_norm, attn_sinks, conv_w, conv_b, lru_gate_a_w, lru_gate_a_b, lru_gate_x_w, lru_gate_x_b, lru_lambda, w_branch_pool, w_branch_attn, w_branch_lru, w_out, norm_ffb, ffb_w_gu, ffb_w_down):
    Bp, Sp, D = x_prompt.shape
    Bs, Ss, _ = x_sample.shape
    L = norm_ffa.shape[0]
    Tp, Ts = Bp * Sp, Bs * Ss
    pool_c = pool_scale.shape[1]
    attn_c = N_KV_HEADS * GQA_GROUP * HEAD_DIM
    kv_c = N_KV_HEADS * HEAD_DIM
    lru_c = conv_w.shape[2]
    q0 = pool_c
    kv0 = q0 + attn_c
    xl0 = kv0 + 2 * kv_c
    gl0 = xl0 + lru_c
    gate0 = gl0 + lru_c

    tok = dict(ntiles=TOKEN_TILES)
    xp = x_prompt.reshape(Tp, D)
    xs = x_sample.reshape(Ts, D)

    vec3 = lambda a: a.reshape(L, 1, a.shape[-1])
    norm_ffa3, norm_mix3, norm_ffb3 = vec3(norm_ffa), vec3(norm_mix), vec3(norm_ffb)
    pool_scale3 = vec3(pool_scale)
    conv_b3, ba3, bx3, lam3 = vec3(conv_b), vec3(lru_gate_a_b), vec3(lru_gate_x_b), vec3(lru_lambda)
    qn3 = vec3(jnp.tile(q_norm, (1, LANES // HEAD_DIM)))
    kn3 = vec3(jnp.tile(k_norm, (1, LANES // HEAD_DIM)))
    pool_past = jnp.pad(state_pool, ((0, 0), (0, 0), (POOL_HIST - POOL_KEEP, 0), (0, 0)))
    conv_past = jnp.pad(state_conv, ((0, 0), (0, 0), (CONV_HIST - (CONV_WIDTH - 1), 0), (0, 0)))
    h0 = state_rglru.reshape(L, Bs, 1, lru_c)
    k_past = cache_k_win.reshape(L, Bs, WINDOW, kv_c)
    v_past = cache_v_win.reshape(L, Bs, WINDOW, kv_c)
    rope_p = _rope_tables(jnp.arange(Sp), 1)
    rope_s = _rope_tables(PAST_LEN + jnp.arange(Ss), Bs)

    prompt = dict(nseq=Bp, seqlen=Sp, row0=0)
    sample = dict(nseq=Bs, seqlen=Ss, row0=0)
    st_p = ([], [], [], [], [])
    st_s = ([], [], [], [], [])
    for l in range(L):
        xp, xs = _ffn_call(xp, xs, norm_ffa3, ffa_w_gu, ffa_w_down, l, tf=512, **tok)
        zp, zs, xn = _inproj_call(xp, xs, norm_mix3, w_in, l, gate0, sub=2, tn=512, **tok)

        pool_p, np_p = _pool_call(zp, None, pool_w, pool_scale3, l, bs=1, tt=1024, pos0=0, **prompt)
        pool_s, np_s = _pool_call(zs, pool_past, pool_w, pool_scale3, l, bs=Bs, tt=Ss, pos0=PAST_LEN, **sample)

        att_p, nk_p, nv_p = _attn_call(zp, None, None, qn3, kn3, *rope_p, attn_sinks, l, q0, kv0,
                                       bs=1, qb=WINDOW, nsub=2, pos0=0, **prompt)
        att_s, nk_s, nv_s = _attn_call(zs, k_past, v_past, qn3, kn3, *rope_s, attn_sinks, l, q0, kv0,
                                       bs=Bs, qb=Ss, pos0=PAST_LEN, **sample)

        lru_args = (conv_w, conv_b3, lru_gate_a_w, ba3, lru_gate_x_w, bx3, lam3, l, xl0, gl0)
        lru_p, nc_p, nh_p = _lru_call(zp, None, None, *lru_args, bs=1, tt=1024, cc=512, **prompt)
        lru_s, nc_s, nh_s = _lru_call(zs, conv_past, h0, *lru_args, bs=Bs, tt=Ss, cc=512, **sample)

        m = _merge_call(xn, w_in, gate0, ((pool_p, pool_s), (att_p, att_s), (lru_p, lru_s)),
                        (w_branch_pool, w_branch_attn, w_branch_lru), l, tc=256, **tok)
        xp, xs = _outproj_call(xp, xs, m, w_out, l, sub=2, tn=512, **tok)
        xp, xs = _ffn_call(xp, xs, norm_ffb3, ffb_w_gu, ffb_w_down, l, tf=512, **tok)

        for lst, val in zip(st_p, (np_p, nk_p, nv_p, nc_p, nh_p)):
            lst.append(val)
        for lst, val in zip(st_s, (np_s, nk_s, nv_s, nc_s, nh_s)):
            lst.append(val)

    def states(st, nseq):
        pool = jnp.stack(st[0])[:, :, POOL_HIST - POOL_KEEP:, :]
        k = jnp.stack(st[1]).reshape(L, nseq, WINDOW, N_KV_HEADS, HEAD_DIM)
        v = jnp.stack(st[2]).reshape(L, nseq, WINDOW, N_KV_HEADS, HEAD_DIM)
        conv = jnp.stack(st[3])[:, :, CONV_HIST - (CONV_WIDTH - 1):, :]
        h = jnp.stack(st[4]).reshape(L, nseq, lru_c)
        return pool, k, v, conv, h

    pool_p, k_p, v_p, conv_p, h_p = states(st_p, Bp)
    pool_s, k_s, v_s, conv_s, h_s = states(st_s, Bs)
    y_p = xp.reshape(Bp, Sp, D)
    y_s = xs.reshape(Bs, Ss, D)
    return (y_p, y_s, pool_p, pool_s, k_p, k_s, v_p, v_s, conv_p, conv_s, h_p, h_s)
```

```python
import functools

import jax
import jax.numpy as jnp
from jax import lax
from jax.experimental import pallas as pl
from jax.experimental.pallas import tpu as pltpu

F32 = jnp.float32
BF16 = jnp.bfloat16

RMS_EPS = 1e-6
NEG_INF = -1e30
FFN_RES_WEIGHT = 0.5
POOL_WINDOWS = (2, 4, 8, 16)
POOL_KEEP = max(POOL_WINDOWS) - 1
POOL_HIST = 16
HEAD_DIM = 64
N_KV_HEADS = 4
GQA_GROUP = 4
WINDOW = 128
ROT_DIM = HEAD_DIM // 4
ROPE_THETA = 500000.0
CONV_WIDTH = 4
CONV_HIST = 8
LRU_C = 8.0
LRU_BLOCK_DIM = 64
PAST_LEN = 16384
LANES = 128
SUBLANES = 8
SOFTMAX_ROWS = 256
MIB = 1024 * 1024

TOKEN_TILES = 8
WIDE_STEP_TILES = 2
FFN_CHUNK = 512
INPROJ_CHUNK = 512
MERGE_CHUNK = 256
OUTPROJ_CHUNK = 512
MIXER_TIME_TILE = 1024
LRU_CHANNEL_TILE = 512
ATTN_BLOCKS_PER_STEP = 4
VMEM_LIMIT_MIB = dict(ffn=60, inproj=60, merge=58, outproj=56, pool=48, lru=48, attn=56)


def _bdot(a, b):
    return jnp.dot(a, b, preferred_element_type=F32)


def _bdot_nt(a, b):
    return lax.dot_general(a, b, (((1,), (1,)), ((), ())), preferred_element_type=F32)


def _rms_bf16(x, g):
    ms = jnp.mean(x * x, axis=-1, keepdims=True)
    return ((x * lax.rsqrt(ms + RMS_EPS)) * g).astype(BF16)


def _sigmoid(x):
    return 0.5 * jnp.tanh(0.5 * x) + 0.5


def _params(name, grid_rank):
    return pltpu.CompilerParams(dimension_semantics=("arbitrary",) * grid_rank,
                                vmem_limit_bytes=VMEM_LIMIT_MIB[name] * MIB)


def _ffn_kernel(xp_hbm, xs_ref, g_ref, wg_ref, wu_ref, wd_ref, op_ref, os_ref, xn_ref, xbuf_ref, sem):
    tp = xbuf_ref.shape[0]
    tile_x = _TilePrefetch(xp_hbm, xbuf_ref, sem)

    @pl.when(pl.program_id(1) == 0)
    def _():
        tile_x.wait_current()
        xp = xbuf_ref[...]
        xs = xs_ref[...]
        xn_ref[0:tp, :] = _rms_bf16(xp, g_ref[...])
        xn_ref[tp:, :] = _rms_bf16(xs, g_ref[...])
        op_ref[...] = xp
        os_ref[...] = xs

    xn = xn_ref[...]
    g = _bdot(xn, wg_ref[...].astype(BF16))
    u = _bdot(xn, wu_ref[...].astype(BF16))
    h = (FFN_RES_WEIGHT * ((g * _sigmoid(g)) * u)).astype(BF16)
    res = _bdot(h, wd_ref[...].astype(BF16))
    op_ref[...] += res[:tp]
    os_ref[...] += res[tp:]
    tile_x.start_next()


class _TilePrefetch:
    def __init__(self, hbm_ref, buf_ref, sem):
        self.hbm_ref, self.buf_ref, self.sem = hbm_ref, buf_ref, sem
        self.rows = buf_ref.shape[0]

    def _copy(self, tile):
        start = pl.multiple_of(tile * self.rows, SUBLANES)
        return pltpu.make_async_copy(self.hbm_ref.at[pl.ds(start, self.rows), :], self.buf_ref, self.sem)

    def wait_current(self):
        i = pl.program_id(0)

        @pl.when(i == 0)
        def _():
            self._copy(i).start()

        self._copy(i).wait()

    def start_next(self):
        i = pl.program_id(0)

        @pl.when((pl.program_id(1) == pl.num_programs(1) - 1) & (i + 1 < pl.num_programs(0)))
        def _():
            self._copy(i + 1).start()


def _ffn_call(xp, xs, norm, w_gu, w_down, layer, *, ntiles, tf):
    Tp, D = xp.shape
    Ts = xs.shape[0]
    tp, ts = Tp // ntiles, Ts // ntiles
    dff = w_down.shape[1]
    nj = dff // tf
    rows = lambda t: pl.BlockSpec((t, D), lambda i, j: (i, 0))
    return pl.pallas_call(
        _ffn_kernel,
        grid=(ntiles, nj),
        in_specs=[
            pl.BlockSpec(memory_space=pl.ANY), rows(ts),
            pl.BlockSpec((None, 1, D), lambda i, j: (layer, 0, 0)),
            pl.BlockSpec((None, D, tf), lambda i, j: (layer, 0, j)),
            pl.BlockSpec((None, D, tf), lambda i, j: (layer, 0, j + nj)),
            pl.BlockSpec((None, tf, D), lambda i, j: (layer, j, 0)),
        ],
        out_specs=[rows(tp), rows(ts)],
        out_shape=[jax.ShapeDtypeStruct((Tp, D), F32), jax.ShapeDtypeStruct((Ts, D), F32)],
        scratch_shapes=[pltpu.VMEM((tp + ts, D), BF16), pltpu.VMEM((tp, D), F32),
                        pltpu.SemaphoreType.DMA(())],
        compiler_params=_params("ffn", 2),
        name="ffn",
    )(xp, xs, norm, w_gu, w_gu, w_down)


def _tile_rows(sub, tp, ts):
    tm = tp + ts
    return [((slice(s * tp, (s + 1) * tp), slice(s * tm, s * tm + tp)),
             (slice(s * ts, (s + 1) * ts), slice(s * tm + tp, (s + 1) * tm))) for s in range(sub)]


def _inproj_kernel(xp_hbm, xs_ref, g_ref, w_ref, zp_ref, zs_ref, xn_ref, xbuf_ref, sem, *, sub):
    tiles = _tile_rows(sub, xbuf_ref.shape[0] // sub, xs_ref.shape[0] // sub)
    tile_x = _TilePrefetch(xp_hbm, xbuf_ref, sem)

    @pl.when(pl.program_id(1) == 0)
    def _():
        tile_x.wait_current()
        for (p_rows, p_int), (s_rows, s_int) in tiles:
            xn_ref[p_int, :] = _rms_bf16(xbuf_ref[p_rows, :], g_ref[...])
            xn_ref[s_int, :] = _rms_bf16(xs_ref[s_rows, :], g_ref[...])

    res = _bdot(xn_ref[...], w_ref[...].astype(BF16))
    for (p_rows, p_int), (s_rows, s_int) in tiles:
        zp_ref[p_rows, :] = res[p_int]
        zs_ref[s_rows, :] = res[s_int]
    tile_x.start_next()


def _inproj_call(xp, xs, norm, w_in, layer, ncols, *, ntiles, sub, tn):
    Tp, D = xp.shape
    Ts = xs.shape[0]
    ntiles = ntiles // sub
    tp, ts = Tp // ntiles, Ts // ntiles
    rows = lambda t: pl.BlockSpec((t, D), lambda i, j: (i, 0))
    cols = lambda t: pl.BlockSpec((t, tn), lambda i, j: (i, j))
    return pl.pallas_call(
        functools.partial(_inproj_kernel, sub=sub),
        grid=(ntiles, ncols // tn),
        in_specs=[
            pl.BlockSpec(memory_space=pl.ANY), rows(ts),
            pl.BlockSpec((None, 1, D), lambda i, j: (layer, 0, 0)),
            pl.BlockSpec((None, D, tn), lambda i, j: (layer, 0, j)),
        ],
        out_specs=[cols(tp), cols(ts), rows(tp + ts)],
        out_shape=[jax.ShapeDtypeStruct((Tp, ncols), F32), jax.ShapeDtypeStruct((Ts, ncols), F32),
                   jax.ShapeDtypeStruct((Tp + Ts, D), BF16)],
        scratch_shapes=[pltpu.VMEM((tp, D), F32), pltpu.SemaphoreType.DMA(())],
        compiler_params=_params("inproj", 2),
        name="inproj",
    )(xp, xs, norm, w_in)


def _merge_kernel(xn_ref, wg0_ref, wg1_ref, wg2_ref, b0p_ref, b0s_ref, b1p_ref, b1s_ref, b2p_ref, b2s_ref,
                  w0_ref, w1_ref, w2_ref, o_ref, lhs_ref):
    tp = b0p_ref.shape[0]

    @pl.when(pl.program_id(1) == 0)
    def _():
        for b, (p_ref, s_ref) in enumerate(((b0p_ref, b0s_ref), (b1p_ref, b1s_ref), (b2p_ref, b2s_ref))):
            lhs_ref[b, 0:tp, :] = p_ref[...]
            lhs_ref[b, tp:, :] = s_ref[...]

    xn = xn_ref[...]

    def term(b, wg_ref, w_ref):
        gate = _sigmoid(_bdot(xn, wg_ref[...].astype(BF16)))
        return gate * _bdot(lhs_ref[b], w_ref[...].astype(BF16))

    m = term(0, wg0_ref, w0_ref) + term(1, wg1_ref, w1_ref) + term(2, wg2_ref, w2_ref)
    o_ref[...] = m.astype(BF16)


def _merge_call(xn, w_in, gate_col0, branches, branch_ws, layer, *, ntiles, tc):
    T, D = xn.shape
    tm = T // ntiles
    W = branches[0][0].shape[1]
    tp, ts = branches[0][0].shape[0] // ntiles, branches[0][1].shape[0] // ntiles
    g0 = gate_col0 // tc
    gstep = D // tc
    gate_w = lambda b: pl.BlockSpec((None, D, tc), lambda i, c: (layer, 0, g0 + b * gstep + c))
    rows = lambda t: pl.BlockSpec((t, W), lambda i, c: (i, 0))
    w_spec = pl.BlockSpec((None, W, tc), lambda i, c: (layer, 0, c))
    return pl.pallas_call(
        _merge_kernel,
        grid=(ntiles, D // tc),
        in_specs=[pl.BlockSpec((tm, D), lambda i, c: (i, 0)), gate_w(0), gate_w(1), gate_w(2)]
                 + [rows(tp), rows(ts)] * 3 + [w_spec] * 3,
        out_specs=pl.BlockSpec((tm, tc), lambda i, c: (i, c)),
        out_shape=jax.ShapeDtypeStruct((T, D), BF16),
        scratch_shapes=[pltpu.VMEM((3, tm, W), BF16)],
        compiler_params=_params("merge", 2),
        name="merge",
    )(xn, w_in, w_in, w_in, *[a for pair in branches for a in pair], *branch_ws)


def _outproj_kernel(xp_ref, xs_ref, m_ref, w_ref, op_ref, os_ref, *, sub):
    res = _bdot(m_ref[...], w_ref[...].astype(BF16))
    for (p_rows, p_int), (s_rows, s_int) in _tile_rows(sub, xp_ref.shape[0] // sub, xs_ref.shape[0] // sub):
        op_ref[p_rows, :] = xp_ref[p_rows, :] + res[p_int]
        os_ref[s_rows, :] = xs_ref[s_rows, :] + res[s_int]


def _outproj_call(xp, xs, m, w_out, layer, *, ntiles, sub, tn):
    Tp, D = xp.shape
    Ts = xs.shape[0]
    ntiles = ntiles // sub
    tp, ts = Tp // ntiles, Ts // ntiles
    cols = lambda t: pl.BlockSpec((t, tn), lambda i, c: (i, c))
    return pl.pallas_call(
        functools.partial(_outproj_kernel, sub=sub),
        grid=(ntiles, D // tn),
        in_specs=[
            cols(tp), cols(ts),
            pl.BlockSpec((tp + ts, D), lambda i, c: (i, 0)),
            pl.BlockSpec((None, D, tn), lambda i, c: (layer, 0, c)),
        ],
        out_specs=[cols(tp), cols(ts)],
        out_shape=[jax.ShapeDtypeStruct((Tp, D), F32), jax.ShapeDtypeStruct((Ts, D), F32)],
        compiler_params=_params("outproj", 2),
        name="outproj",
    )(xp, xs, m, w_out)


def _pool_kernel(*refs, bs, tt, nt, pos0, has_past):
    if has_past:
        u_ref, past_ref, w_ref, s_ref, o_ref, np_ref, e_ref = refs
    else:
        u_ref, w_ref, s_ref, o_ref, np_ref, e_ref = refs
    t = pl.program_id(1)
    C = e_ref.shape[-1]
    H = POOL_HIST

    @pl.when(t == 0)
    def _():
        if has_past:
            e_ref[:, 0:H, :] = past_ref[...]
        else:
            e_ref[:, 0:H, :] = jnp.zeros((bs, H, C), F32)

    e_ref[:, H:H + tt, :] = u_ref[...].reshape(bs, tt, C)
    posp1 = lax.broadcasted_iota(jnp.int32, (1, tt, 1), 1) + (t * tt + pos0 + 1)
    gd = C // len(POOL_WINDOWS)
    for g, w in enumerate(POOL_WINDOWS):
        sl = slice(g * gd, (g + 1) * gd)
        e = e_ref[:, :, sl]
        p = e
        s = 1
        while s < w:
            p = p + pltpu.roll(p, s, 1)
            s *= 2
        inv_cnt = 1.0 / jnp.minimum(posp1, w).astype(F32)
        d = (p[:, H:, :] * inv_cnt - e[:, H:, :]).reshape(bs * tt, gd).astype(BF16)
        out = _bdot(d, w_ref[g].astype(BF16)) * s_ref[:, sl]
        o_ref[:, sl] = out.astype(BF16)

    carry = e_ref[:, tt:tt + H, :]
    e_ref[:, 0:H, :] = carry

    @pl.when(t == nt - 1)
    def _():
        np_ref[...] = carry


def _pool_call(z, past, pool_w, pool_scale, layer, *, nseq, seqlen, row0, bs, tt, pos0):
    C = pool_w.shape[1] * pool_w.shape[2]
    nt = seqlen // tt
    rb0 = row0 // (bs * tt)
    has_past = past is not None
    in_specs = [pl.BlockSpec((bs * tt, C), lambda b, t: (rb0 + b * nt + t, 0))]
    args = [z]
    if has_past:
        in_specs.append(pl.BlockSpec((None, bs, POOL_HIST, C), lambda b, t: (layer, b, 0, 0)))
        args.append(past)
    in_specs += [
        pl.BlockSpec((None,) + pool_w.shape[1:], lambda b, t: (layer, 0, 0, 0)),
        pl.BlockSpec((None, 1, C), lambda b, t: (layer, 0, 0)),
    ]
    args += [pool_w, pool_scale]
    return pl.pallas_call(
        functools.partial(_pool_kernel, bs=bs, tt=tt, nt=nt, pos0=pos0, has_past=has_past),
        grid=(nseq // bs, nt),
        in_specs=in_specs,
        out_specs=[
            pl.BlockSpec((bs * tt, C), lambda b, t: (b * nt + t, 0)),
            pl.BlockSpec((bs, POOL_HIST, C), lambda b, t: (b, 0, 0)),
        ],
        out_shape=[
            jax.ShapeDtypeStruct((nseq * seqlen, C), BF16),
            jax.ShapeDtypeStruct((nseq, POOL_HIST, C), F32),
        ],
        scratch_shapes=[pltpu.VMEM((bs, POOL_HIST + tt, C), F32)],
        compiler_params=_params("pool", 2),
        name="pool",
    )(*args)


def _lru_kernel(*refs, bs, tt, nt, has_past):
    if has_past:
        (x_ref, g_ref, cp_ref, h0_ref, cw_ref, cb_ref, wa_ref, ba_ref, wx_ref, bx_ref, lam_ref,
         y_ref, nc_ref, nh_ref, xe_ref, h_ref, a_ref, b_ref, bd_ref) = refs
    else:
        (x_ref, g_ref, cw_ref, cb_ref, wa_ref, ba_ref, wx_ref, bx_ref, lam_ref,
         y_ref, nc_ref, nh_ref, xe_ref, h_ref, a_ref, b_ref, bd_ref) = refs
    t = pl.program_id(2)
    Cc = xe_ref.shape[-1]
    R = bs * tt
    ncol = Cc // LANES
    HC = CONV_HIST
    blk = LRU_BLOCK_DIM

    @pl.when(t == 0)
    def _():
        if has_past:
            xe_ref[:, 0:HC, :] = cp_ref[...]
            h_ref[...] = jnp.broadcast_to(h0_ref[...], (bs, SUBLANES, Cc))
        else:
            xe_ref[:, 0:HC, :] = jnp.zeros((bs, HC, Cc), F32)
            h_ref[...] = jnp.zeros((bs, SUBLANES, Cc), F32)
        rep = (lax.broadcasted_iota(jnp.int32, (blk, LANES), 0)
               == (lax.broadcasted_iota(jnp.int32, (blk, LANES), 1) & (blk - 1))).astype(BF16)
        diag = ((lax.broadcasted_iota(jnp.int32, (LANES, LANES), 0) >= blk)
                == (lax.broadcasted_iota(jnp.int32, (LANES, LANES), 1) >= blk))
        for p in range(ncol):
            for k, w_ref in enumerate((wa_ref, wx_ref)):
                w2 = w_ref[2 * p:2 * p + 2].reshape(2 * blk, blk).astype(BF16)
                full = _bdot(w2, rep)
                bd_ref[p, :, k * LANES:(k + 1) * LANES] = jnp.where(diag, 0.5 * full, 0.0).astype(BF16)

    xe_ref[:, HC:HC + tt, :] = x_ref[...].reshape(bs, tt, Cc)
    xe = xe_ref[...]
    cw = cw_ref[...]
    xc = cb_ref[...]
    for j in range(CONV_WIDTH):
        shift = CONV_WIDTH - 1 - j
        xs = pltpu.roll(xe, shift, 1) if shift else xe
        xc = xc + xs[:, HC:, :] * cw[j:j + 1]
    xc = xc.reshape(R, Cc)

    row8 = lax.broadcasted_iota(jnp.int32, (1, SUBLANES, 1), 1)
    for p in range(ncol):
        col = slice(p * LANES, (p + 1) * LANES)
        xcp = xc[:, col]
        pre = _bdot(xcp.astype(BF16), bd_ref[p])
        tr = jnp.tanh(pre[:, :LANES] + 0.5 * ba_ref[:, col])
        i = 0.5 * jnp.tanh(pre[:, LANES:] + 0.5 * bx_ref[:, col]) + 0.5
        nl = -lam_ref[:, col]
        sp = jnp.maximum(nl, 0.0) + jnp.log1p(jnp.exp(-jnp.abs(nl)))
        la = (tr + 1.0) * ((-0.5 * LRU_C) * sp)
        a = jnp.exp(la)
        bv = jnp.sqrt(jnp.tanh(-la) * (a * a + 1.0)) * (i * xcp)
        a = a.reshape(R // SUBLANES, SUBLANES, LANES)
        bv = bv.reshape(R // SUBLANES, SUBLANES, LANES)
        for s in (1, 2, 4):
            keep = row8 >= s
            a_sh = pltpu.roll(a, s, 1)
            b_sh = pltpu.roll(bv, s, 1)
            bv = jnp.where(keep, a * b_sh + bv, bv)
            a = jnp.where(keep, a * a_sh, a)
        a_ref[:, :, col] = a.reshape(bs, tt, LANES)
        b_ref[:, :, col] = bv.reshape(bs, tt, LANES)

    def carry_step(k, h):
        o = pl.multiple_of(k * SUBLANES, SUBLANES)
        hb = a_ref[:, pl.ds(o, SUBLANES), :] * h + b_ref[:, pl.ds(o, SUBLANES), :]
        b_ref[:, pl.ds(o, SUBLANES), :] = hb
        return jnp.broadcast_to(hb[:, SUBLANES - 1:SUBLANES, :], hb.shape)

    ngroups = tt // SUBLANES
    h_ref[...] = lax.fori_loop(0, ngroups, carry_step, h_ref[...], unroll=min(ngroups, SUBLANES))
    for p in range(ncol):
        col = slice(p * LANES, (p + 1) * LANES)
        hs = b_ref[:, :, col].reshape(R, LANES)
        y_ref[:, col] = (hs * jax.nn.gelu(g_ref[:, col], approximate=True)).astype(BF16)

    tail = xe_ref[:, tt:tt + HC, :]
    xe_ref[:, 0:HC, :] = tail

    @pl.when(t == nt - 1)
    def _():
        nc_ref[...] = tail
        nh_ref[...] = h_ref[:, 0:1, :]


def _lru_call(z, conv_past, h0, conv_w, conv_b, wa, ba, wx, bx, lam, layer, xcol0, gcol0,
              *, nseq, seqlen, row0, bs, tt, cc):
    C = conv_w.shape[2]
    nt = seqlen // tt
    nc = C // cc
    rb0 = row0 // (bs * tt)
    xb0 = xcol0 // cc
    gb0 = gcol0 // cc
    nblk = cc // LRU_BLOCK_DIM
    has_past = conv_past is not None
    in_specs = [
        pl.BlockSpec((bs * tt, cc), lambda b, c, t: (rb0 + b * nt + t, xb0 + c)),
        pl.BlockSpec((bs * tt, cc), lambda b, c, t: (rb0 + b * nt + t, gb0 + c)),
    ]
    args = [z, z]
    if has_past:
        in_specs += [
            pl.BlockSpec((None, bs, CONV_HIST, cc), lambda b, c, t: (layer, b, 0, c)),
            pl.BlockSpec((None, bs, 1, cc), lambda b, c, t: (layer, b, 0, c)),
        ]
        args += [conv_past, h0]
    vec = pl.BlockSpec((None, 1, cc), lambda b, c, t: (layer, 0, c))
    gw = pl.BlockSpec((None, nblk, LRU_BLOCK_DIM, LRU_BLOCK_DIM), lambda b, c, t: (layer, c, 0, 0))
    in_specs += [pl.BlockSpec((None, CONV_WIDTH, cc), lambda b, c, t: (layer, 0, c)), vec, gw, vec, gw, vec, vec]
    args += [conv_w, conv_b, wa, ba, wx, bx, lam]
    return pl.pallas_call(
        functools.partial(_lru_kernel, bs=bs, tt=tt, nt=nt, has_past=has_past),
        grid=(nseq // bs, nc, nt),
        in_specs=in_specs,
        out_specs=[
            pl.BlockSpec((bs * tt, cc), lambda b, c, t: (b * nt + t, c)),
            pl.BlockSpec((bs, CONV_HIST, cc), lambda b, c, t: (b, 0, c)),
            pl.BlockSpec((bs, 1, cc), lambda b, c, t: (b, 0, c)),
        ],
        out_shape=[
            jax.ShapeDtypeStruct((nseq * seqlen, C), BF16),
            jax.ShapeDtypeStruct((nseq, CONV_HIST, C), F32),
            jax.ShapeDtypeStruct((nseq, 1, C), F32),
        ],
        scratch_shapes=[
            pltpu.VMEM((bs, CONV_HIST + tt, cc), F32),
            pltpu.VMEM((bs, SUBLANES, cc), F32),
            pltpu.VMEM((bs, tt, cc), F32),
            pltpu.VMEM((bs, tt, cc), F32),
            pltpu.VMEM((cc // LANES, LANES, 2 * LANES), BF16),
        ],
        compiler_params=_params("lru", 3),
        name="lru",
    )(*args)


def _attn_stream_kernel(q_ref, kv_ref, qn_ref, kn_ref, cos_ref, s1_ref, s2_ref, sink_ref,
                        o_ref, nk_ref, nv_ref, kb_ref, vb_ref, vt_ref, sc_ref, p_ref,
                        *, qb, nsub, nb, pos0, layer):
    n = pl.program_id(1)
    KV = kb_ref.shape[-1]
    nkeys = 2 * WINDOW
    half = HEAD_DIM
    lane = lax.broadcasted_iota(jnp.int32, (1, LANES), 1)
    seg_ones = ((lax.broadcasted_iota(jnp.int32, (LANES, LANES), 0) >= half)
                == (lax.broadcasted_iota(jnp.int32, (LANES, LANES), 1) >= half)).astype(BF16)
    win = lambda w: slice(w * LANES, (w + 1) * LANES)

    def norm_rot(xw, gain, rows):
        sq = xw * xw
        hi = sq.astype(BF16)
        lo = (sq - hi.astype(F32)).astype(BF16)
        ss = _bdot(hi, seg_ones) + _bdot(lo, seg_ones)
        y = (xw * lax.rsqrt(ss * (1.0 / HEAD_DIM) + RMS_EPS)) * gain
        return (y * cos_ref[rows, :] + pltpu.roll(y, LANES - ROT_DIM // 2, 1) * s1_ref[rows, :]
                + pltpu.roll(y, ROT_DIM // 2, 1) * s2_ref[rows, :])

    si = lax.broadcasted_iota(jnp.int32, (nkeys, 2 * qb), 0)
    ci = lax.broadcasted_iota(jnp.int32, (nkeys, 2 * qb), 1)
    qi = ci & (qb - 1)
    in_band = (si >= qi) & (si <= qi + WINDOW)
    first_half = lax.broadcasted_iota(jnp.int32, (1, 2 * qb), 1) < qb

    @pl.when(n == 0)
    def _():
        kb_ref[0:WINDOW, :] = jnp.zeros((WINDOW, KV), F32)
        vb_ref[0:WINDOW, :] = jnp.zeros((WINDOW, KV), F32)
        vt_ref[:, 0:WINDOW] = jnp.zeros((KV, WINDOW), F32)

    kvx = kv_ref[...]
    for w in range(KV // LANES):
        kb_ref[WINDOW:, win(w)] = norm_rot(kvx[:, win(w)], kn_ref[...], slice(None))
    vb_ref[WINDOW:, :] = kvx[:, KV:]
    vt_ref[:, WINDOW:] = kvx[:, KV:].T
    q_gain = qn_ref[...] * (HEAD_DIM ** -0.5)
    for sb in range(nsub):
        qrows = slice(sb * qb, (sb + 1) * qb)
        krows = slice(sb * qb, sb * qb + nkeys)
        for kv in range(N_KV_HEADS):
            w, off = divmod(kv, 2)
            own = jnp.where((lane >= half) == (off == 1), kb_ref[krows, win(w)], 0.0)
            moved = pltpu.roll(own, half, 1)
            kk = jnp.concatenate([moved, own] if off == 1 else [own, moved], axis=0).astype(BF16)
            qst = jnp.concatenate([norm_rot(q_ref[qrows, win(2 * kv + h)], q_gain, qrows) for h in range(2)],
                                  axis=0).astype(BF16)
            sc_ref[sb * N_KV_HEADS + kv] = _bdot_nt(kk, qst)
    for sb in range(nsub):
        kpos0 = pos0 + (n * nsub + sb) * qb - WINDOW
        valid = in_band & (si + kpos0 >= 0)
        for kv in range(N_KV_HEADS):
            for ch in range(2):
                rows = slice(ch * nkeys, (ch + 1) * nkeys)
                sink = jnp.where(first_half, sink_ref[layer, kv * GQA_GROUP + ch],
                                 sink_ref[layer, kv * GQA_GROUP + 2 + ch])
                sblk = jnp.where(valid, sc_ref[sb * N_KV_HEADS + kv, rows, :], NEG_INF)
                m = jnp.maximum(jnp.max(sblk, axis=0, keepdims=True), sink)
                p = jnp.exp(sblk - m)
                den = jnp.sum(p, axis=0, keepdims=True) + jnp.exp(sink - m)
                p_ref[sb * N_KV_HEADS + kv, rows, :] = (p * (1.0 / den)).astype(BF16)
    zeros_t = jnp.zeros((half, nkeys), F32)
    for sb in range(nsub):
        for kv in range(N_KV_HEADS):
            vth = vt_ref[kv * half:(kv + 1) * half, sb * qb:sb * qb + nkeys]
            vvt = jnp.concatenate([jnp.concatenate([vth, zeros_t], axis=1),
                                   jnp.concatenate([zeros_t, vth], axis=1)], axis=0).astype(BF16)
            out = _bdot(vvt, p_ref[sb * N_KV_HEADS + kv]).T
            for rh in range(2):
                o_ref[sb * qb:(sb + 1) * qb, win(2 * kv + rh)] = out[rh * qb:(rh + 1) * qb].astype(BF16)
    last = slice(nsub * WINDOW, (nsub + 1) * WINDOW)
    new_k = kb_ref[last, :]
    new_v = vb_ref[last, :]
    kb_ref[0:WINDOW, :] = new_k
    vb_ref[0:WINDOW, :] = new_v
    vt_ref[:, 0:WINDOW] = vt_ref[:, last]

    @pl.when(n == nb - 1)
    def _():
        nk_ref[0] = new_k
        nv_ref[0] = new_v


def _attn_seq_kernel(q_ref, kv_ref, kp_ref, vp_ref, qn_ref, kn_ref, cos_ref, s1_ref, s2_ref, sink_ref,
                     o_ref, nk_ref, nv_ref, q_scr, k_scr, o_scr, *, bs, qb, pos0, layer):
    KV = k_scr.shape[-1]
    nkeys = 2 * WINDOW
    half = HEAD_DIM
    lane = lax.broadcasted_iota(jnp.int32, (1, LANES), 1)
    seg_ones = ((lax.broadcasted_iota(jnp.int32, (LANES, LANES), 0) >= half)
                == (lax.broadcasted_iota(jnp.int32, (LANES, LANES), 1) >= half)).astype(BF16)
    R = q_scr.shape[0]
    nq = q_scr.shape[1] // LANES
    nk = KV // LANES
    win = lambda w: slice(w * LANES, (w + 1) * LANES)

    xs = jnp.concatenate([q_ref[:, win(w)] for w in range(nq)] + [kv_ref[:, win(w)] for w in range(nk)], axis=0)
    sq = xs * xs
    hi = sq.astype(BF16)
    lo = (sq - hi.astype(F32)).astype(BF16)
    ss = _bdot(hi, seg_ones) + _bdot(lo, seg_ones)
    y = (xs * lax.rsqrt(ss * (1.0 / HEAD_DIM) + RMS_EPS)).reshape(nq + nk, R, LANES)
    gains = jnp.concatenate([jnp.broadcast_to(qn_ref[...] * (HEAD_DIM ** -0.5), (nq, 1, LANES)),
                             jnp.broadcast_to(kn_ref[...], (nk, 1, LANES))], axis=0)
    y = y * gains
    y = (y * cos_ref[...] + pltpu.roll(y, LANES - ROT_DIM // 2, 2) * s1_ref[...]
         + pltpu.roll(y, ROT_DIM // 2, 2) * s2_ref[...])
    for w in range(nq):
        q_scr[:, win(w)] = y[w]
    for w in range(nk):
        k_scr[:, win(w)] = y[nq + w]

    group = max(1, min(N_KV_HEADS, SOFTMAX_ROWS // (2 * qb)))
    nrow = 2 * group * qb
    qi = lax.broadcasted_iota(jnp.int32, (nrow, nkeys), 0) & (qb - 1)
    si = lax.broadcasted_iota(jnp.int32, (nrow, nkeys), 1)
    valid = (si >= qi) & (si <= qi + WINDOW) & (si + (pos0 - WINDOW) >= 0)
    sink_cols = [
        [jnp.concatenate([jnp.full((qb, 1), sink_ref[layer, kv * GQA_GROUP + 2 * rh + ch], F32)
                          for kv in range(g0, g0 + group) for rh in range(2)], axis=0)
         for ch in range(2)]
        for g0 in range(0, N_KV_HEADS, group)]

    def seq_step(s, c):
        rows = pl.ds(pl.multiple_of(s * qb, qb), qb)
        prev_k = kp_ref[s]
        prev_v = vp_ref[s]
        cur_k = k_scr[rows, :]
        cur_v = kv_ref[rows, KV:]
        pad = jnp.zeros((WINDOW - qb, KV), F32)
        k_all = jnp.concatenate([prev_k, cur_k, pad], axis=0)
        v_all = jnp.concatenate([prev_v, cur_v, pad], axis=0)

        def spread(x, kv):
            w, off = divmod(kv, 2)
            lo = jnp.where((lane >= half) == (off == 1), x[:, win(w)], 0.0)
            if off == 1:
                lo = pltpu.roll(lo, half, 1)
            return jnp.concatenate([lo, pltpu.roll(lo, half, 1)], axis=0).astype(BF16)

        for gi, g0 in enumerate(range(0, N_KV_HEADS, group)):
            scores = []
            for kv in range(g0, g0 + group):
                qst = jnp.concatenate([q_scr[rows, win(2 * kv + h)] for h in range(2)], axis=0).astype(BF16)
                scores.append(_bdot_nt(qst, spread(k_all, kv)))
            sc = jnp.concatenate(scores, axis=0)
            prob_cols = []
            for ch in range(2):
                sink = sink_cols[gi][ch]
                sblk = jnp.where(valid, sc[:, ch * nkeys:(ch + 1) * nkeys], NEG_INF)
                m = jnp.maximum(jnp.max(sblk, axis=-1, keepdims=True), sink)
                p = jnp.exp(sblk - m)
                den = jnp.sum(p, axis=-1, keepdims=True) + jnp.exp(sink - m)
                prob_cols.append(p * (1.0 / den))
            probs = jnp.concatenate(prob_cols, axis=1).astype(BF16)
            for j, kv in enumerate(range(g0, g0 + group)):
                out = _bdot(probs[2 * j * qb:2 * (j + 1) * qb], spread(v_all, kv))
                for rh in range(2):
                    o_scr[rows, win(2 * kv + rh)] = out[rh * qb:(rh + 1) * qb]
        nk_ref[s] = jnp.concatenate([prev_k[qb:], cur_k], axis=0)
        nv_ref[s] = jnp.concatenate([prev_v[qb:], cur_v], axis=0)
        return c

    lax.fori_loop(0, bs, seq_step, 0, unroll=min(bs, 8))
    o_ref[...] = o_scr[...].astype(BF16)


def _attn_call(z, k_past, v_past, qn, kn, cos_t, s1_t, s2_t, sinks, layer, qcol0, kvcol0,
               *, nseq, seqlen, row0, bs, qb, pos0, nsub=1):
    A = N_KV_HEADS * GQA_GROUP * HEAD_DIM
    KV = N_KV_HEADS * HEAD_DIM
    R = bs * qb * nsub
    nb = seqlen // (qb * nsub)
    rb0 = row0 // R
    has_past = k_past is not None
    if has_past:
        assert nb == 1 and nsub == 1 and qb < WINDOW
        body = functools.partial(_attn_seq_kernel, bs=bs, qb=qb, pos0=pos0, layer=layer)
        scratch = [pltpu.VMEM((R, A), F32), pltpu.VMEM((R, KV), F32), pltpu.VMEM((R, A), F32)]
    else:
        assert bs == 1 and qb == WINDOW
        body = functools.partial(_attn_stream_kernel, qb=qb, nsub=nsub, nb=nb, pos0=pos0, layer=layer)
        held = (1 + nsub) * WINDOW
        scratch = [pltpu.VMEM((held, KV), F32), pltpu.VMEM((held, KV), F32), pltpu.VMEM((KV, held), F32),
                   pltpu.VMEM((nsub * N_KV_HEADS, 4 * WINDOW, 2 * qb), F32),
                   pltpu.VMEM((nsub * N_KV_HEADS, 4 * WINDOW, 2 * qb), BF16)]
    in_specs = [
        pl.BlockSpec((R, A), lambda b, n: (rb0 + b * nb + n, qcol0 // A)),
        pl.BlockSpec((R, 2 * KV), lambda b, n: (rb0 + b * nb + n, kvcol0 // (2 * KV))),
    ]
    args = [z, z]
    if has_past:
        past_spec = pl.BlockSpec((None, bs, WINDOW, KV), lambda b, n: (layer, b, 0, 0))
        in_specs += [past_spec, past_spec]
        args += [k_past, v_past]
    gain = pl.BlockSpec((None, 1, LANES), lambda b, n: (layer, 0, 0))
    tab = pl.BlockSpec((R, LANES), lambda b, n: (n, 0))
    in_specs += [gain, gain, tab, tab, tab, pl.BlockSpec(memory_space=pltpu.SMEM)]
    args += [qn, kn, cos_t, s1_t, s2_t, sinks]
    state = pl.BlockSpec((bs, WINDOW, KV), lambda b, n: (b, 0, 0))
    return pl.pallas_call(
        body,
        grid=(nseq // bs, nb),
        in_specs=in_specs,
        out_specs=[pl.BlockSpec((R, A), lambda b, n: (b * nb + n, 0)), state, state],
        out_shape=[
            jax.ShapeDtypeStruct((nseq * seqlen, A), BF16),
            jax.ShapeDtypeStruct((nseq, WINDOW, KV), F32),
            jax.ShapeDtypeStruct((nseq, WINDOW, KV), F32),
        ],
        scratch_shapes=scratch,
        compiler_params=_params("attn", 2),
        name="attn",
    )(*args)


def _rope_tables(pos, nrep):
    half = ROT_DIM // 2
    inv = ROPE_THETA ** (-jnp.arange(half, dtype=F32) / half)
    ang = pos.astype(F32)[:, None] * inv[None, :]
    cos, sin = jnp.cos(ang), jnp.sin(ang)
    T = pos.shape[0]
    ones = jnp.ones((T, HEAD_DIM - ROT_DIM), F32)
    zeros = jnp.zeros((T, HEAD_DIM - ROT_DIM), F32)
    zh = jnp.zeros((T, half), F32)
    cos_t = jnp.concatenate([cos, cos, ones], axis=1)
    s1_t = jnp.concatenate([-sin, zh, zeros], axis=1)
    s2_t = jnp.concatenate([zh, sin, zeros], axis=1)
    rep = LANES // HEAD_DIM
    return tuple(jnp.tile(t, (nrep, rep)) for t in (cos_t, s1_t, s2_t))


def kernel(x_prompt, x_sample, state_pool, cache_k_win, cache_v_win, state_conv, state_rglru, norm_ffa, ffa_w_gu, ffa_w_down, norm_mix, w_in, pool_w, pool_scale, q_norm, k_norm, attn_sinks, conv_w, conv_b, lru_gate_a_w, lru_gate_a_b, lru_gate_x_w, lru_gate_x_b, lru_lambda, w_branch_pool, w_branch_attn, w_branch_lru, w_out, norm_ffb, ffb_w_gu, ffb_w_down):
    Bp, Sp, D = x_prompt.shape
    Bs, Ss, _ = x_sample.shape
    L = norm_ffa.shape[0]
    Tp, Ts = Bp * Sp, Bs * Ss
    pool_c = pool_scale.shape[1]
    attn_c = N_KV_HEADS * GQA_GROUP * HEAD_DIM
    kv_c = N_KV_HEADS * HEAD_DIM
    lru_c = conv_w.shape[2]
    q0 = pool_c
    kv0 = q0 + attn_c
    xl0 = kv0 + 2 * kv_c
    gl0 = xl0 + lru_c
    gate0 = gl0 + lru_c

    tok = dict(ntiles=TOKEN_TILES)
    xp = x_prompt.reshape(Tp, D)
    xs = x_sample.reshape(Ts, D)

    vec3 = lambda a: a.reshape(L, 1, a.shape[-1])
    norm_ffa3, norm_mix3, norm_ffb3 = vec3(norm_ffa), vec3(norm_mix), vec3(norm_ffb)
    pool_scale3 = vec3(pool_scale)
    conv_b3, ba3, bx3, lam3 = vec3(conv_b), vec3(lru_gate_a_b), vec3(lru_gate_x_b), vec3(lru_lambda)
    qn3 = vec3(jnp.tile(q_norm, (1, LANES // HEAD_DIM)))
    kn3 = vec3(jnp.tile(k_norm, (1, LANES // HEAD_DIM)))
    pool_past = jnp.pad(state_pool, ((0, 0), (0, 0), (POOL_HIST - POOL_KEEP, 0), (0, 0)))
    conv_past = jnp.pad(state_conv, ((0, 0), (0, 0), (CONV_HIST - (CONV_WIDTH - 1), 0), (0, 0)))
    h0 = state_rglru.reshape(L, Bs, 1, lru_c)
    k_past = cache_k_win.reshape(L, Bs, WINDOW, kv_c)
    v_past = cache_v_win.reshape(L, Bs, WINDOW, kv_c)
    rope_p = _rope_tables(jnp.arange(Sp), 1)
    rope_s = _rope_tables(PAST_LEN + jnp.arange(Ss), Bs)

    prompt = dict(nseq=Bp, seqlen=Sp, row0=0)
    sample = dict(nseq=Bs, seqlen=Ss, row0=0)
    st_p = ([], [], [], [], [])
    st_s = ([], [], [], [], [])
    for l in range(L):
        xp, xs = _ffn_call(xp, xs, norm_ffa3, ffa_w_gu, ffa_w_down, l, tf=FFN_CHUNK, **tok)
        zp, zs, xn = _inproj_call(xp, xs, norm_mix3, w_in, l, gate0, sub=WIDE_STEP_TILES, tn=INPROJ_CHUNK, **tok)

        tt_p = min(MIXER_TIME_TILE, Sp)
        pool_p, np_p = _pool_call(zp, None, pool_w, pool_scale3, l, bs=1, tt=tt_p, pos0=0, **prompt)
        pool_s, np_s = _pool_call(zs, pool_past, pool_w, pool_scale3, l, bs=Bs, tt=Ss, pos0=PAST_LEN, **sample)

        att_p, nk_p, nv_p = _attn_call(zp, None, None, qn3, kn3, *rope_p, attn_sinks, l, q0, kv0,
                                       bs=1, qb=WINDOW, nsub=ATTN_BLOCKS_PER_STEP, pos0=0, **prompt)
        att_s, nk_s, nv_s = _attn_call(zs, k_past, v_past, qn3, kn3, *rope_s, attn_sinks, l, q0, kv0,
                                       bs=Bs, qb=Ss, pos0=PAST_LEN, **sample)

        lru_args = (conv_w, conv_b3, lru_gate_a_w, ba3, lru_gate_x_w, bx3, lam3, l, xl0, gl0)
        lru_p, nc_p, nh_p = _lru_call(zp, None, None, *lru_args, bs=1, tt=tt_p, cc=LRU_CHANNEL_TILE, **prompt)
        lru_s, nc_s, nh_s = _lru_call(zs, conv_past, h0, *lru_args, bs=Bs, tt=Ss, cc=LRU_CHANNEL_TILE, **sample)

        m = _merge_call(xn, w_in, gate0, ((pool_p, pool_s), (att_p, att_s), (lru_p, lru_s)),
                        (w_branch_pool, w_branch_attn, w_branch_lru), l, tc=MERGE_CHUNK, **tok)
        xp, xs = _outproj_call(xp, xs, m, w_out, l, sub=WIDE_STEP_TILES, tn=OUTPROJ_CHUNK, **tok)
        xp, xs = _ffn_call(xp, xs, norm_ffb3, ffb_w_gu, ffb_w_down, l, tf=FFN_CHUNK, **tok)

        for lst, val in zip(st_p, (np_p, nk_p, nv_p, nc_p, nh_p)):
            lst.append(val)
        for lst, val in zip(st_s, (np_s, nk_s, nv_s, nc_s, nh_s)):
            lst.append(val)

    def states(st, nseq):
        pool = jnp.stack(st[0])[:, :, POOL_HIST - POOL_KEEP:, :]
        k = jnp.stack(st[1]).reshape(L, nseq, WINDOW, N_KV_HEADS, HEAD_DIM)
        v = jnp.stack(st[2]).reshape(L, nseq, WINDOW, N_KV_HEADS, HEAD_DIM)
        conv = jnp.stack(st[3])[:, :, CONV_HIST - (CONV_WIDTH - 1):, :]
        h = jnp.stack(st[4]).reshape(L, nseq, lru_c)
        return pool, k, v, conv, h

    pool_p, k_p, v_p, conv_p, h_p = states(st_p, Bp)
    pool_s, k_s, v_s, conv_s, h_s = states(st_s, Bs)
    y_p = xp.reshape(Bp, Sp, D)
    y_s = xs.reshape(Bs, Ss, D)
    return (y_p, y_s, pool_p, pool_s, k_p, k_s, v_p, v_s, conv_p, conv_s, h_p, h_s)
```

```python
import functools

import jax
import jax.numpy as jnp
from jax import lax
from jax.experimental import pallas as pl
from jax.experimental.pallas import tpu as pltpu

F32 = jnp.float32
BF16 = jnp.bfloat16

RMS_EPS = 1e-6
NEG_INF = -1e30
FFN_RES_WEIGHT = 0.5
POOL_WINDOWS = (2, 4, 8, 16)
POOL_KEEP = max(POOL_WINDOWS) - 1
POOL_HIST = 16
HEAD_DIM = 64
N_KV_HEADS = 4
GQA_GROUP = 4
WINDOW = 128
ROT_DIM = HEAD_DIM // 4
ROPE_THETA = 500000.0
CONV_WIDTH = 4
CONV_HIST = 8
LRU_C = 8.0
LRU_BLOCK_DIM = 64
PAST_LEN = 16384
LANES = 128
SUBLANES = 8
SOFTMAX_ROWS = 256
MIB = 1024 * 1024

TOKEN_TILES = 8
WIDE_STEP_TILES = 2
FFN_CHUNK = 512
INPROJ_CHUNK = 512
MERGE_CHUNK = 256
OUTPROJ_CHUNK = 512
MIXER_TIME_TILE = 1024
LRU_CHANNEL_TILE = 512
ATTN_BLOCKS_PER_STEP = 8
VMEM_LIMIT_MIB = dict(ffn=60, inproj=60, merge=58, outproj=56, pool=48, lru=48, attn=56)


def _exact_div(a, b):
    assert a % b == 0, (a, b)
    return a // b


def _bdot(a, b):
    return jnp.dot(a, b, preferred_element_type=F32)


def _bdot_nt(a, b):
    return lax.dot_general(a, b, (((1,), (1,)), ((), ())), preferred_element_type=F32)


def _rms_bf16(x, g):
    ms = jnp.mean(x * x, axis=-1, keepdims=True)
    return ((x * lax.rsqrt(ms + RMS_EPS)) * g).astype(BF16)


def _sigmoid(x):
    return 0.5 * jnp.tanh(0.5 * x) + 0.5


def _params(name, grid_rank):
    return pltpu.CompilerParams(dimension_semantics=("arbitrary",) * grid_rank,
                                vmem_limit_bytes=VMEM_LIMIT_MIB[name] * MIB)


def _ffn_kernel(xp_hbm, xs_ref, g_ref, wg_ref, wu_ref, wd_ref, op_ref, os_ref, xn_ref, xbuf_ref, sem):
    tp = xbuf_ref.shape[0]
    tile_x = _TilePrefetch(xp_hbm, xbuf_ref, sem)

    @pl.when(pl.program_id(1) == 0)
    def _():
        tile_x.wait_current()
        xp = xbuf_ref[...]
        xs = xs_ref[...]
        xn_ref[0:tp, :] = _rms_bf16(xp, g_ref[...])
        xn_ref[tp:, :] = _rms_bf16(xs, g_ref[...])
        op_ref[...] = xp
        os_ref[...] = xs

    xn = xn_ref[...]
    g = _bdot(xn, wg_ref[...].astype(BF16))
    u = _bdot(xn, wu_ref[...].astype(BF16))
    h = (FFN_RES_WEIGHT * ((g * _sigmoid(g)) * u)).astype(BF16)
    res = _bdot(h, wd_ref[...].astype(BF16))
    op_ref[...] += res[:tp]
    os_ref[...] += res[tp:]
    tile_x.start_next()


class _TilePrefetch:
    def __init__(self, hbm_ref, buf_ref, sem):
        self.hbm_ref, self.buf_ref, self.sem = hbm_ref, buf_ref, sem
        self.rows = buf_ref.shape[0]

    def _copy(self, tile):
        start = pl.multiple_of(tile * self.rows, SUBLANES)
        return pltpu.make_async_copy(self.hbm_ref.at[pl.ds(start, self.rows), :], self.buf_ref, self.sem)

    def wait_current(self):
        i = pl.program_id(0)

        @pl.when(i == 0)
        def _():
            self._copy(i).start()

        self._copy(i).wait()

    def start_next(self):
        i = pl.program_id(0)

        @pl.when((pl.program_id(1) == pl.num_programs(1) - 1) & (i + 1 < pl.num_programs(0)))
        def _():
            self._copy(i + 1).start()


def _ffn_call(xp, xs, norm, w_gu, w_down, layer, *, ntiles, tf):
    Tp, D = xp.shape
    Ts = xs.shape[0]
    tp, ts = _exact_div(Tp, ntiles), _exact_div(Ts, ntiles)
    dff = w_down.shape[1]
    nj = _exact_div(dff, tf)
    rows = lambda t: pl.BlockSpec((t, D), lambda i, j: (i, 0))
    return pl.pallas_call(
        _ffn_kernel,
        grid=(ntiles, nj),
        in_specs=[
            pl.BlockSpec(memory_space=pl.ANY), rows(ts),
            pl.BlockSpec((None, 1, D), lambda i, j: (layer, 0, 0)),
            pl.BlockSpec((None, D, tf), lambda i, j: (layer, 0, j)),
            pl.BlockSpec((None, D, tf), lambda i, j: (layer, 0, j + nj)),
            pl.BlockSpec((None, tf, D), lambda i, j: (layer, j, 0)),
        ],
        out_specs=[rows(tp), rows(ts)],
        out_shape=[jax.ShapeDtypeStruct((Tp, D), F32), jax.ShapeDtypeStruct((Ts, D), F32)],
        scratch_shapes=[pltpu.VMEM((tp + ts, D), BF16), pltpu.VMEM((tp, D), F32),
                        pltpu.SemaphoreType.DMA(())],
        compiler_params=_params("ffn", 2),
        name="ffn",
    )(xp, xs, norm, w_gu, w_gu, w_down)


def _tile_rows(sub, tp, ts):
    tm = tp + ts
    return [((slice(s * tp, (s + 1) * tp), slice(s * tm, s * tm + tp)),
             (slice(s * ts, (s + 1) * ts), slice(s * tm + tp, (s + 1) * tm))) for s in range(sub)]


def _inproj_kernel(xp_hbm, xs_ref, g_ref, w_ref, zp_ref, zs_ref, xn_ref, xbuf_ref, sem, *, sub):
    tiles = _tile_rows(sub, xbuf_ref.shape[0] // sub, xs_ref.shape[0] // sub)
    tile_x = _TilePrefetch(xp_hbm, xbuf_ref, sem)

    @pl.when(pl.program_id(1) == 0)
    def _():
        tile_x.wait_current()
        for (p_rows, p_int), (s_rows, s_int) in tiles:
            xn_ref[p_int, :] = _rms_bf16(xbuf_ref[p_rows, :], g_ref[...])
            xn_ref[s_int, :] = _rms_bf16(xs_ref[s_rows, :], g_ref[...])

    res = _bdot(xn_ref[...], w_ref[...].astype(BF16))
    for (p_rows, p_int), (s_rows, s_int) in tiles:
        zp_ref[p_rows, :] = res[p_int]
        zs_ref[s_rows, :] = res[s_int]
    tile_x.start_next()


def _inproj_call(xp, xs, norm, w_in, layer, ncols, *, ntiles, sub, tn):
    Tp, D = xp.shape
    Ts = xs.shape[0]
    ntiles = _exact_div(ntiles, sub)
    tp, ts = _exact_div(Tp, ntiles), _exact_div(Ts, ntiles)
    rows = lambda t: pl.BlockSpec((t, D), lambda i, j: (i, 0))
    cols = lambda t: pl.BlockSpec((t, tn), lambda i, j: (i, j))
    return pl.pallas_call(
        functools.partial(_inproj_kernel, sub=sub),
        grid=(ntiles, _exact_div(ncols, tn)),
        in_specs=[
            pl.BlockSpec(memory_space=pl.ANY), rows(ts),
            pl.BlockSpec((None, 1, D), lambda i, j: (layer, 0, 0)),
            pl.BlockSpec((None, D, tn), lambda i, j: (layer, 0, j)),
        ],
        out_specs=[cols(tp), cols(ts), rows(tp + ts)],
        out_shape=[jax.ShapeDtypeStruct((Tp, ncols), F32), jax.ShapeDtypeStruct((Ts, ncols), F32),
                   jax.ShapeDtypeStruct((Tp + Ts, D), BF16)],
        scratch_shapes=[pltpu.VMEM((tp, D), F32), pltpu.SemaphoreType.DMA(())],
        compiler_params=_params("inproj", 2),
        name="inproj",
    )(xp, xs, norm, w_in)


def _merge_kernel(xn_ref, wg0_ref, wg1_ref, wg2_ref, b0p_ref, b0s_ref, b1p_ref, b1s_ref, b2p_ref, b2s_ref,
                  w0_ref, w1_ref, w2_ref, o_ref, lhs_ref):
    tp = b0p_ref.shape[0]

    @pl.when(pl.program_id(1) == 0)
    def _():
        for b, (p_ref, s_ref) in enumerate(((b0p_ref, b0s_ref), (b1p_ref, b1s_ref), (b2p_ref, b2s_ref))):
            lhs_ref[b, 0:tp, :] = p_ref[...]
            lhs_ref[b, tp:, :] = s_ref[...]

    xn = xn_ref[...]

    def term(b, wg_ref, w_ref):
        gate = _sigmoid(_bdot(xn, wg_ref[...].astype(BF16)))
        return gate * _bdot(lhs_ref[b], w_ref[...].astype(BF16))

    m = term(0, wg0_ref, w0_ref) + term(1, wg1_ref, w1_ref) + term(2, wg2_ref, w2_ref)
    o_ref[...] = m.astype(BF16)


def _merge_call(xn, w_in, gate_col0, branches, branch_ws, layer, *, ntiles, tc):
    T, D = xn.shape
    tm = _exact_div(T, ntiles)
    W = branches[0][0].shape[1]
    tp, ts = _exact_div(branches[0][0].shape[0], ntiles), _exact_div(branches[0][1].shape[0], ntiles)
    g0 = _exact_div(gate_col0, tc)
    gstep = _exact_div(D, tc)
    gate_w = lambda b: pl.BlockSpec((None, D, tc), lambda i, c: (layer, 0, g0 + b * gstep + c))
    rows = lambda t: pl.BlockSpec((t, W), lambda i, c: (i, 0))
    w_spec = pl.BlockSpec((None, W, tc), lambda i, c: (layer, 0, c))
    return pl.pallas_call(
        _merge_kernel,
        grid=(ntiles, gstep),
        in_specs=[pl.BlockSpec((tm, D), lambda i, c: (i, 0)), gate_w(0), gate_w(1), gate_w(2)]
                 + [rows(tp), rows(ts)] * 3 + [w_spec] * 3,
        out_specs=pl.BlockSpec((tm, tc), lambda i, c: (i, c)),
        out_shape=jax.ShapeDtypeStruct((T, D), BF16),
        scratch_shapes=[pltpu.VMEM((3, tm, W), BF16)],
        compiler_params=_params("merge", 2),
        name="merge",
    )(xn, w_in, w_in, w_in, *[a for pair in branches for a in pair], *branch_ws)


def _outproj_kernel(xp_ref, xs_ref, m_ref, w_ref, op_ref, os_ref, *, sub):
    res = _bdot(m_ref[...], w_ref[...].astype(BF16))
    for (p_rows, p_int), (s_rows, s_int) in _tile_rows(sub, xp_ref.shape[0] // sub, xs_ref.shape[0] // sub):
        op_ref[p_rows, :] = xp_ref[p_rows, :] + res[p_int]
        os_ref[s_rows, :] = xs_ref[s_rows, :] + res[s_int]


def _outproj_call(xp, xs, m, w_out, layer, *, ntiles, sub, tn):
    Tp, D = xp.shape
    Ts = xs.shape[0]
    ntiles = _exact_div(ntiles, sub)
    tp, ts = _exact_div(Tp, ntiles), _exact_div(Ts, ntiles)
    cols = lambda t: pl.BlockSpec((t, tn), lambda i, c: (i, c))
    return pl.pallas_call(
        functools.partial(_outproj_kernel, sub=sub),
        grid=(ntiles, _exact_div(D, tn)),
        in_specs=[
            cols(tp), cols(ts),
            pl.BlockSpec((tp + ts, D), lambda i, c: (i, 0)),
            pl.BlockSpec((None, D, tn), lambda i, c: (layer, 0, c)),
        ],
        out_specs=[cols(tp), cols(ts)],
        out_shape=[jax.ShapeDtypeStruct((Tp, D), F32), jax.ShapeDtypeStruct((Ts, D), F32)],
        compiler_params=_params("outproj", 2),
        name="outproj",
    )(xp, xs, m, w_out)


def _pool_kernel(*refs, bs, tt, nt, pos0, has_past):
    if has_past:
        u_ref, past_ref, w_ref, s_ref, o_ref, np_ref, e_ref = refs
    else:
        u_ref, w_ref, s_ref, o_ref, np_ref, e_ref = refs
    t = pl.program_id(1)
    C = e_ref.shape[-1]
    H = POOL_HIST

    @pl.when(t == 0)
    def _():
        if has_past:
            e_ref[:, 0:H, :] = past_ref[...]
        else:
            e_ref[:, 0:H, :] = jnp.zeros((bs, H, C), F32)

    e_ref[:, H:H + tt, :] = u_ref[...].reshape(bs, tt, C)
    posp1 = lax.broadcasted_iota(jnp.int32, (1, tt, 1), 1) + (t * tt + pos0 + 1)
    gd = C // len(POOL_WINDOWS)
    for g, w in enumerate(POOL_WINDOWS):
        sl = slice(g * gd, (g + 1) * gd)
        e = e_ref[:, :, sl]
        p = e
        s = 1
        while s < w:
            p = p + pltpu.roll(p, s, 1)
            s *= 2
        inv_cnt = 1.0 / jnp.minimum(posp1, w).astype(F32)
        d = (p[:, H:, :] * inv_cnt - e[:, H:, :]).reshape(bs * tt, gd).astype(BF16)
        out = _bdot(d, w_ref[g].astype(BF16)) * s_ref[:, sl]
        o_ref[:, sl] = out.astype(BF16)

    carry = e_ref[:, tt:tt + H, :]
    e_ref[:, 0:H, :] = carry

    @pl.when(t == nt - 1)
    def _():
        np_ref[...] = carry


def _pool_call(z, past, pool_w, pool_scale, layer, *, nseq, seqlen, row0, bs, tt, pos0):
    C = pool_w.shape[1] * pool_w.shape[2]
    nt = _exact_div(seqlen, tt)
    rb0 = _exact_div(row0, bs * tt)
    has_past = past is not None
    in_specs = [pl.BlockSpec((bs * tt, C), lambda b, t: (rb0 + b * nt + t, 0))]
    args = [z]
    if has_past:
        in_specs.append(pl.BlockSpec((None, bs, POOL_HIST, C), lambda b, t: (layer, b, 0, 0)))
        args.append(past)
    in_specs += [
        pl.BlockSpec((None,) + pool_w.shape[1:], lambda b, t: (layer, 0, 0, 0)),
        pl.BlockSpec((None, 1, C), lambda b, t: (layer, 0, 0)),
    ]
    args += [pool_w, pool_scale]
    return pl.pallas_call(
        functools.partial(_pool_kernel, bs=bs, tt=tt, nt=nt, pos0=pos0, has_past=has_past),
        grid=(nseq // bs, nt),
        in_specs=in_specs,
        out_specs=[
            pl.BlockSpec((bs * tt, C), lambda b, t: (b * nt + t, 0)),
            pl.BlockSpec((bs, POOL_HIST, C), lambda b, t: (b, 0, 0)),
        ],
        out_shape=[
            jax.ShapeDtypeStruct((nseq * seqlen, C), BF16),
            jax.ShapeDtypeStruct((nseq, POOL_HIST, C), F32),
        ],
        scratch_shapes=[pltpu.VMEM((bs, POOL_HIST + tt, C), F32)],
        compiler_params=_params("pool", 2),
        name="pool",
    )(*args)


def _lru_kernel(*refs, bs, tt, nt, has_past):
    if has_past:
        (x_ref, g_ref, cp_ref, h0_ref, cw_ref, cb_ref, wa_ref, ba_ref, wx_ref, bx_ref, lam_ref,
         y_ref, nc_ref, nh_ref, xe_ref, h_ref, a_ref, b_ref, bd_ref) = refs
    else:
        (x_ref, g_ref, cw_ref, cb_ref, wa_ref, ba_ref, wx_ref, bx_ref, lam_ref,
         y_ref, nc_ref, nh_ref, xe_ref, h_ref, a_ref, b_ref, bd_ref) = refs
    t = pl.program_id(2)
    Cc = xe_ref.shape[-1]
    R = bs * tt
    ncol = Cc // LANES
    HC = CONV_HIST
    blk = LRU_BLOCK_DIM

    @pl.when(t == 0)
    def _():
        if has_past:
            xe_ref[:, 0:HC, :] = cp_ref[...]
            h_ref[...] = jnp.broadcast_to(h0_ref[...], (bs, SUBLANES, Cc))
        else:
            xe_ref[:, 0:HC, :] = jnp.zeros((bs, HC, Cc), F32)
            h_ref[...] = jnp.zeros((bs, SUBLANES, Cc), F32)
        rep = (lax.broadcasted_iota(jnp.int32, (blk, LANES), 0)
               == (lax.broadcasted_iota(jnp.int32, (blk, LANES), 1) & (blk - 1))).astype(BF16)
        diag = ((lax.broadcasted_iota(jnp.int32, (LANES, LANES), 0) >= blk)
                == (lax.broadcasted_iota(jnp.int32, (LANES, LANES), 1) >= blk))
        for p in range(ncol):
            for k, w_ref in enumerate((wa_ref, wx_ref)):
                w2 = w_ref[2 * p:2 * p + 2].reshape(2 * blk, blk).astype(BF16)
                full = _bdot(w2, rep)
                bd_ref[p, :, k * LANES:(k + 1) * LANES] = jnp.where(diag, 0.5 * full, 0.0).astype(BF16)

    xe_ref[:, HC:HC + tt, :] = x_ref[...].reshape(bs, tt, Cc)
    xe = xe_ref[...]
    cw = cw_ref[...]
    xc = cb_ref[...]
    for j in range(CONV_WIDTH):
        shift = CONV_WIDTH - 1 - j
        xs = pltpu.roll(xe, shift, 1) if shift else xe
        xc = xc + xs[:, HC:, :] * cw[j:j + 1]
    xc = xc.reshape(R, Cc)

    row8 = lax.broadcasted_iota(jnp.int32, (1, SUBLANES, 1), 1)
    for p in range(ncol):
        col = slice(p * LANES, (p + 1) * LANES)
        xcp = xc[:, col]
        pre = _bdot(xcp.astype(BF16), bd_ref[p])
        tr = jnp.tanh(pre[:, :LANES] + 0.5 * ba_ref[:, col])
        i = 0.5 * jnp.tanh(pre[:, LANES:] + 0.5 * bx_ref[:, col]) + 0.5
        nl = -lam_ref[:, col]
        sp = jnp.maximum(nl, 0.0) + jnp.log1p(jnp.exp(-jnp.abs(nl)))
        la = (tr + 1.0) * ((-0.5 * LRU_C) * sp)
        a = jnp.exp(la)
        bv = jnp.sqrt(jnp.tanh(-la) * (a * a + 1.0)) * (i * xcp)
        a = a.reshape(R // SUBLANES, SUBLANES, LANES)
        bv = bv.reshape(R // SUBLANES, SUBLANES, LANES)
        for s in (1, 2, 4):
            keep = row8 >= s
            a_sh = pltpu.roll(a, s, 1)
            b_sh = pltpu.roll(bv, s, 1)
            bv = jnp.where(keep, a * b_sh + bv, bv)
            a = jnp.where(keep, a * a_sh, a)
        a_ref[:, :, col] = a.reshape(bs, tt, LANES)
        b_ref[:, :, col] = bv.reshape(bs, tt, LANES)

    def carry_step(k, h):
        o = pl.multiple_of(k * SUBLANES, SUBLANES)
        hb = a_ref[:, pl.ds(o, SUBLANES), :] * h + b_ref[:, pl.ds(o, SUBLANES), :]
        b_ref[:, pl.ds(o, SUBLANES), :] = hb
        return jnp.broadcast_to(hb[:, SUBLANES - 1:SUBLANES, :], hb.shape)

    ngroups = tt // SUBLANES
    h_ref[...] = lax.fori_loop(0, ngroups, carry_step, h_ref[...], unroll=min(ngroups, SUBLANES))
    for p in range(ncol):
        col = slice(p * LANES, (p + 1) * LANES)
        hs = b_ref[:, :, col].reshape(R, LANES)
        y_ref[:, col] = (hs * jax.nn.gelu(g_ref[:, col], approximate=True)).astype(BF16)

    tail = xe_ref[:, tt:tt + HC, :]
    xe_ref[:, 0:HC, :] = tail

    @pl.when(t == nt - 1)
    def _():
        nc_ref[...] = tail
        nh_ref[...] = h_ref[:, 0:1, :]


def _lru_call(z, conv_past, h0, conv_w, conv_b, wa, ba, wx, bx, lam, layer, xcol0, gcol0,
              *, nseq, seqlen, row0, bs, tt, cc):
    C = conv_w.shape[2]
    nt = _exact_div(seqlen, tt)
    nc = _exact_div(C, cc)
    rb0 = _exact_div(row0, bs * tt)
    xb0 = _exact_div(xcol0, cc)
    gb0 = _exact_div(gcol0, cc)
    nblk = _exact_div(cc, LRU_BLOCK_DIM)
    has_past = conv_past is not None
    in_specs = [
        pl.BlockSpec((bs * tt, cc), lambda b, c, t: (rb0 + b * nt + t, xb0 + c)),
        pl.BlockSpec((bs * tt, cc), lambda b, c, t: (rb0 + b * nt + t, gb0 + c)),
    ]
    args = [z, z]
    if has_past:
        in_specs += [
            pl.BlockSpec((None, bs, CONV_HIST, cc), lambda b, c, t: (layer, b, 0, c)),
            pl.BlockSpec((None, bs, 1, cc), lambda b, c, t: (layer, b, 0, c)),
        ]
        args += [conv_past, h0]
    vec = pl.BlockSpec((None, 1, cc), lambda b, c, t: (layer, 0, c))
    gw = pl.BlockSpec((None, nblk, LRU_BLOCK_DIM, LRU_BLOCK_DIM), lambda b, c, t: (layer, c, 0, 0))
    in_specs += [pl.BlockSpec((None, CONV_WIDTH, cc), lambda b, c, t: (layer, 0, c)), vec, gw, vec, gw, vec, vec]
    args += [conv_w, conv_b, wa, ba, wx, bx, lam]
    return pl.pallas_call(
        functools.partial(_lru_kernel, bs=bs, tt=tt, nt=nt, has_past=has_past),
        grid=(nseq // bs, nc, nt),
        in_specs=in_specs,
        out_specs=[
            pl.BlockSpec((bs * tt, cc), lambda b, c, t: (b * nt + t, c)),
            pl.BlockSpec((bs, CONV_HIST, cc), lambda b, c, t: (b, 0, c)),
            pl.BlockSpec((bs, 1, cc), lambda b, c, t: (b, 0, c)),
        ],
        out_shape=[
            jax.ShapeDtypeStruct((nseq * seqlen, C), BF16),
            jax.ShapeDtypeStruct((nseq, CONV_HIST, C), F32),
            jax.ShapeDtypeStruct((nseq, 1, C), F32),
        ],
        scratch_shapes=[
            pltpu.VMEM((bs, CONV_HIST + tt, cc), F32),
            pltpu.VMEM((bs, SUBLANES, cc), F32),
            pltpu.VMEM((bs, tt, cc), F32),
            pltpu.VMEM((bs, tt, cc), F32),
            pltpu.VMEM((cc // LANES, LANES, 2 * LANES), BF16),
        ],
        compiler_params=_params("lru", 3),
        name="lru",
    )(*args)


def _attn_stream_kernel(q_ref, kv_ref, qn_ref, kn_ref, cos_ref, s1_ref, s2_ref, sink_ref,
                        o_ref, nk_ref, nv_ref, kb_ref, vb_ref, vt_ref, sc_ref, p_ref,
                        *, qb, nsub, nb, pos0, layer):
    n = pl.program_id(1)
    KV = kb_ref.shape[-1]
    nkeys = 2 * WINDOW
    half = HEAD_DIM
    lane = lax.broadcasted_iota(jnp.int32, (1, LANES), 1)
    seg_ones = ((lax.broadcasted_iota(jnp.int32, (LANES, LANES), 0) >= half)
                == (lax.broadcasted_iota(jnp.int32, (LANES, LANES), 1) >= half)).astype(BF16)
    win = lambda w: slice(w * LANES, (w + 1) * LANES)

    def norm_rot(xw, gain, rows):
        sq = xw * xw
        hi = sq.astype(BF16)
        lo = (sq - hi.astype(F32)).astype(BF16)
        ss = _bdot(hi, seg_ones) + _bdot(lo, seg_ones)
        y = (xw * lax.rsqrt(ss * (1.0 / HEAD_DIM) + RMS_EPS)) * gain
        return (y * cos_ref[rows, :] + pltpu.roll(y, LANES - ROT_DIM // 2, 1) * s1_ref[rows, :]
                + pltpu.roll(y, ROT_DIM // 2, 1) * s2_ref[rows, :])

    si = lax.broadcasted_iota(jnp.int32, (nkeys, 2 * qb), 0)
    ci = lax.broadcasted_iota(jnp.int32, (nkeys, 2 * qb), 1)
    qi = ci & (qb - 1)
    in_band = (si >= qi) & (si <= qi + WINDOW)
    first_half = lax.broadcasted_iota(jnp.int32, (1, 2 * qb), 1) < qb

    @pl.when(n == 0)
    def _():
        kb_ref[0:WINDOW, :] = jnp.zeros((WINDOW, KV), F32)
        vb_ref[0:WINDOW, :] = jnp.zeros((WINDOW, KV), F32)
        vt_ref[:, 0:WINDOW] = jnp.zeros((KV, WINDOW), F32)

    kvx = kv_ref[...]
    for w in range(KV // LANES):
        kb_ref[WINDOW:, win(w)] = norm_rot(kvx[:, win(w)], kn_ref[...], slice(None))
    vb_ref[WINDOW:, :] = kvx[:, KV:]
    vt_ref[:, WINDOW:] = kvx[:, KV:].T
    q_gain = qn_ref[...] * (HEAD_DIM ** -0.5)
    for sb in range(nsub):
        qrows = slice(sb * qb, (sb + 1) * qb)
        krows = slice(sb * qb, sb * qb + nkeys)
        for kv in range(N_KV_HEADS):
            w, off = divmod(kv, 2)
            own = jnp.where((lane >= half) == (off == 1), kb_ref[krows, win(w)], 0.0)
            moved = pltpu.roll(own, half, 1)
            kk = jnp.concatenate([moved, own] if off == 1 else [own, moved], axis=0).astype(BF16)
            qst = jnp.concatenate([norm_rot(q_ref[qrows, win(2 * kv + h)], q_gain, qrows) for h in range(2)],
                                  axis=0).astype(BF16)
            sc_ref[sb * N_KV_HEADS + kv] = _bdot_nt(kk, qst)
    for sb in range(nsub):
        kpos0 = pos0 + (n * nsub + sb) * qb - WINDOW
        valid = in_band & (si + kpos0 >= 0)
        for kv in range(N_KV_HEADS):
            for ch in range(2):
                rows = slice(ch * nkeys, (ch + 1) * nkeys)
                sink = jnp.where(first_half, sink_ref[layer, kv * GQA_GROUP + ch],
                                 sink_ref[layer, kv * GQA_GROUP + 2 + ch])
                sblk = jnp.where(valid, sc_ref[sb * N_KV_HEADS + kv, rows, :], NEG_INF)
                m = jnp.maximum(jnp.max(sblk, axis=0, keepdims=True), sink)
                p = jnp.exp(sblk - m)
                den = jnp.sum(p, axis=0, keepdims=True) + jnp.exp(sink - m)
                p_ref[sb * N_KV_HEADS + kv, rows, :] = (p * (1.0 / den)).astype(BF16)
    zeros_t = jnp.zeros((half, nkeys), F32)
    for sb in range(nsub):
        for kv in range(N_KV_HEADS):
            vth = vt_ref[kv * half:(kv + 1) * half, sb * qb:sb * qb + nkeys]
            vvt = jnp.concatenate([jnp.concatenate([vth, zeros_t], axis=1),
                                   jnp.concatenate([zeros_t, vth], axis=1)], axis=0).astype(BF16)
            out = _bdot(vvt, p_ref[sb * N_KV_HEADS + kv]).T
            for rh in range(2):
                o_ref[sb * qb:(sb + 1) * qb, win(2 * kv + rh)] = out[rh * qb:(rh + 1) * qb].astype(BF16)
    last = slice(nsub * WINDOW, (nsub + 1) * WINDOW)
    new_k = kb_ref[last, :]
    new_v = vb_ref[last, :]
    kb_ref[0:WINDOW, :] = new_k
    vb_ref[0:WINDOW, :] = new_v
    vt_ref[:, 0:WINDOW] = vt_ref[:, last]

    @pl.when(n == nb - 1)
    def _():
        nk_ref[0] = new_k
        nv_ref[0] = new_v


def _attn_seq_kernel(q_ref, kv_ref, kp_ref, vp_ref, qn_ref, kn_ref, cos_ref, s1_ref, s2_ref, sink_ref,
                     o_ref, nk_ref, nv_ref, q_scr, k_scr, o_scr, *, bs, qb, pos0, layer):
    KV = k_scr.shape[-1]
    nkeys = 2 * WINDOW
    half = HEAD_DIM
    lane = lax.broadcasted_iota(jnp.int32, (1, LANES), 1)
    seg_ones = ((lax.broadcasted_iota(jnp.int32, (LANES, LANES), 0) >= half)
                == (lax.broadcasted_iota(jnp.int32, (LANES, LANES), 1) >= half)).astype(BF16)
    R = q_scr.shape[0]
    nq = q_scr.shape[1] // LANES
    nk = KV // LANES
    win = lambda w: slice(w * LANES, (w + 1) * LANES)

    xs = jnp.concatenate([q_ref[:, win(w)] for w in range(nq)] + [kv_ref[:, win(w)] for w in range(nk)], axis=0)
    sq = xs * xs
    hi = sq.astype(BF16)
    lo = (sq - hi.astype(F32)).astype(BF16)
    ss = _bdot(hi, seg_ones) + _bdot(lo, seg_ones)
    y = (xs * lax.rsqrt(ss * (1.0 / HEAD_DIM) + RMS_EPS)).reshape(nq + nk, R, LANES)
    gains = jnp.concatenate([jnp.broadcast_to(qn_ref[...] * (HEAD_DIM ** -0.5), (nq, 1, LANES)),
                             jnp.broadcast_to(kn_ref[...], (nk, 1, LANES))], axis=0)
    y = y * gains
    y = (y * cos_ref[...] + pltpu.roll(y, LANES - ROT_DIM // 2, 2) * s1_ref[...]
         + pltpu.roll(y, ROT_DIM // 2, 2) * s2_ref[...])
    for w in range(nq):
        q_scr[:, win(w)] = y[w]
    for w in range(nk):
        k_scr[:, win(w)] = y[nq + w]

    group = max(1, min(N_KV_HEADS, SOFTMAX_ROWS // (2 * qb)))
    nrow = 2 * group * qb
    qi = lax.broadcasted_iota(jnp.int32, (nrow, nkeys), 0) & (qb - 1)
    si = lax.broadcasted_iota(jnp.int32, (nrow, nkeys), 1)
    valid = (si >= qi) & (si <= qi + WINDOW) & (si + (pos0 - WINDOW) >= 0)
    sink_cols = [
        [jnp.concatenate([jnp.full((qb, 1), sink_ref[layer, kv * GQA_GROUP + 2 * rh + ch], F32)
                          for kv in range(g0, g0 + group) for rh in range(2)], axis=0)
         for ch in range(2)]
        for g0 in range(0, N_KV_HEADS, group)]

    def seq_step(s, c):
        rows = pl.ds(pl.multiple_of(s * qb, qb), qb)
        prev_k = kp_ref[s]
        prev_v = vp_ref[s]
        cur_k = k_scr[rows, :]
        cur_v = kv_ref[rows, KV:]
        pad = jnp.zeros((WINDOW - qb, KV), F32)
        k_all = jnp.concatenate([prev_k, cur_k, pad], axis=0)
        v_all = jnp.concatenate([prev_v, cur_v, pad], axis=0)

        def spread(x, kv):
            w, off = divmod(kv, 2)
            lo = jnp.where((lane >= half) == (off == 1), x[:, win(w)], 0.0)
            if off == 1:
                lo = pltpu.roll(lo, half, 1)
            return jnp.concatenate([lo, pltpu.roll(lo, half, 1)], axis=0).astype(BF16)

        for gi, g0 in enumerate(range(0, N_KV_HEADS, group)):
            scores = []
            for kv in range(g0, g0 + group):
                qst = jnp.concatenate([q_scr[rows, win(2 * kv + h)] for h in range(2)], axis=0).astype(BF16)
                scores.append(_bdot_nt(qst, spread(k_all, kv)))
            sc = jnp.concatenate(scores, axis=0)
            prob_cols = []
            for ch in range(2):
                sink = sink_cols[gi][ch]
                sblk = jnp.where(valid, sc[:, ch * nkeys:(ch + 1) * nkeys], NEG_INF)
                m = jnp.maximum(jnp.max(sblk, axis=-1, keepdims=True), sink)
                p = jnp.exp(sblk - m)
                den = jnp.sum(p, axis=-1, keepdims=True) + jnp.exp(sink - m)
                prob_cols.append(p * (1.0 / den))
            probs = jnp.concatenate(prob_cols, axis=1).astype(BF16)
            for j, kv in enumerate(range(g0, g0 + group)):
                out = _bdot(probs[2 * j * qb:2 * (j + 1) * qb], spread(v_all, kv))
                for rh in range(2):
                    o_scr[rows, win(2 * kv + rh)] = out[rh * qb:(rh + 1) * qb]
        nk_ref[s] = jnp.concatenate([prev_k[qb:], cur_k], axis=0)
        nv_ref[s] = jnp.concatenate([prev_v[qb:], cur_v], axis=0)
        return c

    lax.fori_loop(0, bs, seq_step, 0, unroll=min(bs, 8))
    o_ref[...] = o_scr[...].astype(BF16)


def _attn_call(z, k_past, v_past, qn, kn, cos_t, s1_t, s2_t, sinks, layer, qcol0, kvcol0,
               *, nseq, seqlen, row0, bs, qb, pos0, nsub=1):
    A = N_KV_HEADS * GQA_GROUP * HEAD_DIM
    KV = N_KV_HEADS * HEAD_DIM
    R = bs * qb * nsub
    nb = _exact_div(seqlen, qb * nsub)
    rb0 = _exact_div(row0, R)
    assert qcol0 % A == 0 and kvcol0 % (2 * KV) == 0
    has_past = k_past is not None
    if has_past:
        assert nb == 1 and nsub == 1 and qb < WINDOW
        body = functools.partial(_attn_seq_kernel, bs=bs, qb=qb, pos0=pos0, layer=layer)
        scratch = [pltpu.VMEM((R, A), F32), pltpu.VMEM((R, KV), F32), pltpu.VMEM((R, A), F32)]
    else:
        assert bs == 1 and qb == WINDOW
        body = functools.partial(_attn_stream_kernel, qb=qb, nsub=nsub, nb=nb, pos0=pos0, layer=layer)
        held = (1 + nsub) * WINDOW
        scratch = [pltpu.VMEM((held, KV), F32), pltpu.VMEM((held, KV), F32), pltpu.VMEM((KV, held), F32),
                   pltpu.VMEM((nsub * N_KV_HEADS, 4 * WINDOW, 2 * qb), F32),
                   pltpu.VMEM((nsub * N_KV_HEADS, 4 * WINDOW, 2 * qb), BF16)]
    in_specs = [
        pl.BlockSpec((R, A), lambda b, n: (rb0 + b * nb + n, qcol0 // A)),
        pl.BlockSpec((R, 2 * KV), lambda b, n: (rb0 + b * nb + n, kvcol0 // (2 * KV))),
    ]
    args = [z, z]
    if has_past:
        past_spec = pl.BlockSpec((None, bs, WINDOW, KV), lambda b, n: (layer, b, 0, 0))
        in_specs += [past_spec, past_spec]
        args += [k_past, v_past]
    gain = pl.BlockSpec((None, 1, LANES), lambda b, n: (layer, 0, 0))
    tab = pl.BlockSpec((R, LANES), lambda b, n: (n, 0))
    in_specs += [gain, gain, tab, tab, tab, pl.BlockSpec(memory_space=pltpu.SMEM)]
    args += [qn, kn, cos_t, s1_t, s2_t, sinks]
    state = pl.BlockSpec((bs, WINDOW, KV), lambda b, n: (b, 0, 0))
    return pl.pallas_call(
        body,
        grid=(nseq // bs, nb),
        in_specs=in_specs,
        out_specs=[pl.BlockSpec((R, A), lambda b, n: (b * nb + n, 0)), state, state],
        out_shape=[
            jax.ShapeDtypeStruct((nseq * seqlen, A), BF16),
            jax.ShapeDtypeStruct((nseq, WINDOW, KV), F32),
            jax.ShapeDtypeStruct((nseq, WINDOW, KV), F32),
        ],
        scratch_shapes=scratch,
        compiler_params=_params("attn", 2),
        name="attn",
    )(*args)


def _rope_tables(pos, nrep):
    half = ROT_DIM // 2
    inv = ROPE_THETA ** (-jnp.arange(half, dtype=F32) / half)
    ang = pos.astype(F32)[:, None] * inv[None, :]
    cos, sin = jnp.cos(ang), jnp.sin(ang)
    T = pos.shape[0]
    ones = jnp.ones((T, HEAD_DIM - ROT_DIM), F32)
    zeros = jnp.zeros((T, HEAD_DIM - ROT_DIM), F32)
    zh = jnp.zeros((T, half), F32)
    cos_t = jnp.concatenate([cos, cos, ones], axis=1)
    s1_t = jnp.concatenate([-sin, zh, zeros], axis=1)
    s2_t = jnp.concatenate([zh, sin, zeros], axis=1)
    rep = LANES // HEAD_DIM
    return tuple(jnp.tile(t, (nrep, rep)) for t in (cos_t, s1_t, s2_t))


def kernel(x_prompt, x_sample, state_pool, cache_k_win, cache_v_win, state_conv, state_rglru, norm_ffa, ffa_w_gu, ffa_w_down, norm_mix, w_in, pool_w, pool_scale, q_norm, k_norm, attn_sinks, conv_w, conv_b, lru_gate_a_w, lru_gate_a_b, lru_gate_x_w, lru_gate_x_b, lru_lambda, w_branch_pool, w_branch_attn, w_branch_lru, w_out, norm_ffb, ffb_w_gu, ffb_w_down):
    Bp, Sp, D = x_prompt.shape
    Bs, Ss, _ = x_sample.shape
    L = norm_ffa.shape[0]
    Tp, Ts = Bp * Sp, Bs * Ss
    pool_c = pool_scale.shape[1]
    attn_c = N_KV_HEADS * GQA_GROUP * HEAD_DIM
    kv_c = N_KV_HEADS * HEAD_DIM
    lru_c = conv_w.shape[2]
    q0 = pool_c
    kv0 = q0 + attn_c
    xl0 = kv0 + 2 * kv_c
    gl0 = xl0 + lru_c
    gate0 = gl0 + lru_c

    tok = dict(ntiles=TOKEN_TILES)
    xp = x_prompt.reshape(Tp, D)
    xs = x_sample.reshape(Ts, D)

    vec3 = lambda a: a.reshape(L, 1, a.shape[-1])
    norm_ffa3, norm_mix3, norm_ffb3 = vec3(norm_ffa), vec3(norm_mix), vec3(norm_ffb)
    pool_scale3 = vec3(pool_scale)
    conv_b3, ba3, bx3, lam3 = vec3(conv_b), vec3(lru_gate_a_b), vec3(lru_gate_x_b), vec3(lru_lambda)
    qn3 = vec3(jnp.tile(q_norm, (1, LANES // HEAD_DIM)))
    kn3 = vec3(jnp.tile(k_norm, (1, LANES // HEAD_DIM)))
    pool_past = jnp.pad(state_pool, ((0, 0), (0, 0), (POOL_HIST - POOL_KEEP, 0), (0, 0)))
    conv_past = jnp.pad(state_conv, ((0, 0), (0, 0), (CONV_HIST - (CONV_WIDTH - 1), 0), (0, 0)))
    h0 = state_rglru.reshape(L, Bs, 1, lru_c)
    k_past = cache_k_win.reshape(L, Bs, WINDOW, kv_c)
    v_past = cache_v_win.reshape(L, Bs, WINDOW, kv_c)
    rope_p = _rope_tables(jnp.arange(Sp), 1)
    rope_s = _rope_tables(PAST_LEN + jnp.arange(Ss), Bs)

    prompt = dict(nseq=Bp, seqlen=Sp, row0=0)
    sample = dict(nseq=Bs, seqlen=Ss, row0=0)
    st_p = ([], [], [], [], [])
    st_s = ([], [], [], [], [])
    for l in range(L):
        xp, xs = _ffn_call(xp, xs, norm_ffa3, ffa_w_gu, ffa_w_down, l, tf=FFN_CHUNK, **tok)
        zp, zs, xn = _inproj_call(xp, xs, norm_mix3, w_in, l, gate0, sub=WIDE_STEP_TILES, tn=INPROJ_CHUNK, **tok)

        tt_p = min(MIXER_TIME_TILE, Sp)
        pool_p, np_p = _pool_call(zp, None, pool_w, pool_scale3, l, bs=1, tt=tt_p, pos0=0, **prompt)
        pool_s, np_s = _pool_call(zs, pool_past, pool_w, pool_scale3, l, bs=Bs, tt=Ss, pos0=PAST_LEN, **sample)

        att_p, nk_p, nv_p = _attn_call(zp, None, None, qn3, kn3, *rope_p, attn_sinks, l, q0, kv0,
                                       bs=1, qb=WINDOW, nsub=ATTN_BLOCKS_PER_STEP, pos0=0, **prompt)
        att_s, nk_s, nv_s = _attn_call(zs, k_past, v_past, qn3, kn3, *rope_s, attn_sinks, l, q0, kv0,
                                       bs=Bs, qb=Ss, pos0=PAST_LEN, **sample)

        lru_args = (conv_w, conv_b3, lru_gate_a_w, ba3, lru_gate_x_w, bx3, lam3, l, xl0, gl0)
        lru_p, nc_p, nh_p = _lru_call(zp, None, None, *lru_args, bs=1, tt=tt_p, cc=LRU_CHANNEL_TILE, **prompt)
        lru_s, nc_s, nh_s = _lru_call(zs, conv_past, h0, *lru_args, bs=Bs, tt=Ss, cc=LRU_CHANNEL_TILE, **sample)

        m = _merge_call(xn, w_in, gate0, ((pool_p, pool_s), (att_p, att_s), (lru_p, lru_s)),
                        (w_branch_pool, w_branch_attn, w_branch_lru), l, tc=MERGE_CHUNK, **tok)
        xp, xs = _outproj_call(xp, xs, m, w_out, l, sub=WIDE_STEP_TILES, tn=OUTPROJ_CHUNK, **tok)
        xp, xs = _ffn_call(xp, xs, norm_ffb3, ffb_w_gu, ffb_w_down, l, tf=FFN_CHUNK, **tok)

        for lst, val in zip(st_p, (np_p, nk_p, nv_p, nc_p, nh_p)):
            lst.append(val)
        for lst, val in zip(st_s, (np_s, nk_s, nv_s, nc_s, nh_s)):
            lst.append(val)

    def states(st, nseq):
        pool = jnp.stack(st[0])[:, :, POOL_HIST - POOL_KEEP:, :]
        k = jnp.stack(st[1]).reshape(L, nseq, WINDOW, N_KV_HEADS, HEAD_DIM)
        v = jnp.stack(st[2]).reshape(L, nseq, WINDOW, N_KV_HEADS, HEAD_DIM)
        conv = jnp.stack(st[3])[:, :, CONV_HIST - (CONV_WIDTH - 1):, :]
        h = jnp.stack(st[4]).reshape(L, nseq, lru_c)
        return pool, k, v, conv, h

    pool_p, k_p, v_p, conv_p, h_p = states(st_p, Bp)
    pool_s, k_s, v_s, conv_s, h_s = states(st_s, Bs)
    y_p = xp.reshape(Bp, Sp, D)
    y_s = xs.reshape(Bs, Ss, D)
    return (y_p, y_s, pool_p, pool_s, k_p, k_s, v_p, v_s, conv_p, conv_s, h_p, h_s)
```
